```python
import math
import jax
import jax.numpy as jnp
from jax import lax
import numpy as np

D_MODEL = 1024
BATCH = 4
SEQ = 8192
DEPTH = 4

GRID_W = 64
CTX_LEN = 256
N_EVEN = (DEPTH + 1) // 2
N_ODD = DEPTH // 2

GLA_HEADS = 4
GLA_DK = 64
GLA_DV = 128
GLA_GATE_RANK = 16
GLA_TAU = 16.0
GLA_CHUNK = 64

DIFF_HEADS = 4
DIFF_DQK = 64
DIFF_DV = 2 * DIFF_DQK
Q_BLOCK = 128
ROPE_BASE = 10000.0

SSD_D_INNER = 2 * D_MODEL
SSD_HEAD_DIM = 64
SSD_HEADS = SSD_D_INNER // SSD_HEAD_DIM
SSD_GROUPS = 4
SSD_STATE = 128
SSD_CONV = 5
SSD_CHUNK = 128
SSD_CONV_CH = SSD_D_INNER + 2 * SSD_GROUPS * SSD_STATE
SSD_IN = SSD_D_INNER + SSD_CONV_CH + 2 * SSD_HEADS

FFN_HIDDEN = 256 * (-(-8 * D_MODEL // (3 * 256)))

EV_SIZES = (GLA_HEADS * GLA_DK, GLA_HEADS * GLA_DK, GLA_HEADS * GLA_DV, GLA_HEADS * GLA_DV,
            GLA_GATE_RANK, GLA_GATE_RANK,
            DIFF_HEADS * 2 * DIFF_DQK, DIFF_HEADS * 2 * DIFF_DQK, DIFF_HEADS * DIFF_DV)
EV_IN = sum(EV_SIZES)
EV_MIX = GLA_HEADS * GLA_DV + DIFF_HEADS * DIFF_DV

ALPHA = (2 * DEPTH) ** 0.25
BETA = (8 * DEPTH) ** -0.25
LN_EPS = 1e-6
RMS_EPS = 1e-6
F32 = jnp.float32

kernel_name = 'hybrid_gla_diffattn_ssd_deepnorm_prefix'


def _split(t, sizes):
    out, start = [], 0
    for s in sizes:
        out.append(t[..., start:start + s])
        start += s
    return out


def layer_norm(x, g, b):
    xf = x.astype(F32)
    mu = jnp.mean(xf, -1, keepdims=True)
    var = jnp.mean(jnp.square(xf - mu), -1, keepdims=True)
    return ((xf - mu) * lax.rsqrt(var + LN_EPS) * g + b).astype(x.dtype)


def rms_norm(x, g):
    xf = x.astype(F32)
    return (xf * lax.rsqrt(jnp.mean(xf * xf, -1, keepdims=True) + RMS_EPS) * g).astype(x.dtype)


def group_rms_norm(y, g):
    yf = y.astype(F32)
    yg = yf.reshape(yf.shape[:-1] + (SSD_GROUPS, -1))
    yg = yg * lax.rsqrt(jnp.mean(yg * yg, -1, keepdims=True) + RMS_EPS)
    return (yg.reshape(yf.shape) * g).astype(y.dtype)


def to_heads(t, n):
    b, s, _ = t.shape
    return t.reshape(b, s, n, -1).transpose(0, 2, 1, 3)


def from_heads(t):
    b, h, s, d = t.shape
    return t.transpose(0, 2, 1, 3).reshape(b, s, h * d)


def swiglu(h, w_in, w_out):
    gate, up = jnp.split(h @ w_in, 2, axis=-1)
    return (jax.nn.silu(gate) * up) @ w_out


def rope_angles(n_tokens):
    rows = n_tokens // GRID_W
    row = jnp.repeat(jnp.arange(rows, dtype=F32), GRID_W)
    col = jnp.tile(jnp.arange(GRID_W, dtype=F32), rows)
    n_freq = DIFF_DQK // 4
    inv = ROPE_BASE ** (-jnp.arange(n_freq, dtype=F32) / n_freq)
    return row[:, None] * inv, col[:, None] * inv


def _rotate(t, ang):
    t1, t2 = jnp.split(t, 2, axis=-1)
    cos, sin = jnp.cos(ang), jnp.sin(ang)
    return jnp.concatenate([t1 * cos - t2 * sin, t2 * cos + t1 * sin], axis=-1)


def axial_rope(t, ang_row, ang_col):
    tr, tc = jnp.split(t.astype(F32), 2, axis=-1)
    return jnp.concatenate([_rotate(tr, ang_row), _rotate(tc, ang_col)], axis=-1).astype(t.dtype)


def centred_dwconv(t, w):
    pad = SSD_CONV // 2
    return lax.conv_general_dilated(t, w[:, None, :].astype(t.dtype), window_strides=(1,),
                                    padding=[(pad, pad)], dimension_numbers=('NWC', 'WIO', 'NWC'),
                                    feature_group_count=t.shape[-1])


def gla_chunked(q, k, v, log_a, s0):
    bsz, h, t, _ = k.shape
    n, c = t // GLA_CHUNK, GLA_CHUNK
    k, v, log_a = [z.astype(F32).reshape(bsz, h, n, c, -1) for z in (k, v, log_a)]
    b = jnp.cumsum(log_a, axis=3)
    b_last = b[:, :, :, -1:]
    chunk_kv = jnp.einsum('bhnck,bhncv->bhnkv', k * jnp.exp(b_last - b), v)
    decay = jnp.exp(b_last[:, :, :, 0])

    def step(s, inp):
        d, kv = inp
        return d[..., None] * s + kv, s

    s_final, s_start = lax.scan(step, s0.astype(F32),
                                (jnp.moveaxis(decay, 2, 0), jnp.moveaxis(chunk_kv, 2, 0)))
    if q is None:
        return None, s_final
    s_start = jnp.moveaxis(s_start, 0, 2)
    q_dec = q.astype(F32).reshape(bsz, h, n, c, -1) * jnp.exp(b)
    mask = jnp.tril(jnp.ones((c, c), dtype=bool))
    att = jnp.where(mask, jnp.einsum('bhnik,bhnjk->bhnij', q_dec, k * jnp.exp(-b)), 0.0)
    o = jnp.einsum('bhnij,bhnjv->bhniv', att, v) + jnp.einsum('bhnck,bhnkv->bhncv', q_dec, s_start)
    return o.reshape(bsz, h, t, -1), s_final


def gla_mixer(parts_c, parts_x, w_gate2, b_gate, norm_w, need_ctx):
    def prep(parts):
        q, k, v, g, r_f, r_b = parts
        la_f = jax.nn.log_sigmoid((r_f @ w_gate2[0] + b_gate[0]).astype(F32)) / GLA_TAU
        la_b = jax.nn.log_sigmoid((r_b @ w_gate2[1] + b_gate[1]).astype(F32)) / GLA_TAU
        return (to_heads(q, GLA_HEADS) * GLA_DK ** -0.5, to_heads(k, GLA_HEADS), to_heads(v, GLA_HEADS),
                to_heads(la_f, GLA_HEADS), to_heads(la_b, GLA_HEADS), g)

    qc, kc, vc, lfc, lbc, gc = prep(parts_c)
    qx, kx, vx, lfx, lbx, gx = prep(parts_x)
    flip = lambda t: jnp.flip(t, axis=2)
    s0 = jnp.zeros(kc.shape[:2] + (GLA_DK, GLA_DV), F32)
    o_cf, s_f = gla_chunked(qc if need_ctx else None, kc, vc, lfc, s0)
    o_cb, s_b = gla_chunked(flip(qc) if need_ctx else None, flip(kc), flip(vc), flip(lbc), s0)
    o_xf, _ = gla_chunked(qx, kx, vx, lfx, s_f)
    o_xb, _ = gla_chunked(flip(qx), flip(kx), flip(vx), flip(lbx), s_b)

    def finish(o, g):
        return from_heads(rms_norm(o, norm_w)).astype(g.dtype) * jax.nn.silu(g)

    out_x = finish(o_xf + flip(o_xb), gx)
    out_c = finish(o_cf + flip(o_cb), gc) if need_ctx else None
    return out_c, out_x


def diff_attend(q, k, v, lam):
    s = jnp.einsum('bhmqd,bhmkd->bhmqk', q, k).astype(F32) * DIFF_DQK ** -0.5
    p = jax.nn.softmax(s, axis=-1)
    w = p[:, :, 0] - lam * p[:, :, 1]
    return jnp.einsum('bhqk,bhkd->bhqd', w.astype(v.dtype), v)


def diff_mixer(parts_c, parts_x, lam_p, norm_w, layer, ang_row, ang_col, need_ctx):
    def prep(q, k, v):
        b, s, _ = q.shape
        q = q.reshape(b, s, DIFF_HEADS, 2, DIFF_DQK).transpose(0, 2, 3, 1, 4)
        k = k.reshape(b, s, DIFF_HEADS, 2, DIFF_DQK).transpose(0, 2, 3, 1, 4)
        return q, k, to_heads(v, DIFF_HEADS)

    qc, kc, vc = prep(*parts_c)
    qx, kx, vx = prep(*parts_x)
    qx = axial_rope(qx, ang_row, ang_col)
    kx = axial_rope(kx, ang_row, ang_col)
    lam_init = 0.8 - 0.6 * math.exp(-0.3 * layer)
    lp = lam_p.astype(F32)
    lam = jnp.exp(jnp.sum(lp[0] * lp[1])) - jnp.exp(jnp.sum(lp[2] * lp[3])) + lam_init
    k_all = jnp.concatenate([kc, kx], axis=3)
    v_all = jnp.concatenate([vc, vx], axis=2)
    b, h, _, t, d = qx.shape
    nb = t // Q_BLOCK
    q_blocks = jnp.moveaxis(qx.reshape(b, h, 2, nb, Q_BLOCK, d), 3, 0)
    o_blocks = lax.map(lambda qb: diff_attend(qb, k_all, v_all, lam), q_blocks)
    o_x = jnp.moveaxis(o_blocks, 0, 2).reshape(b, h, t, DIFF_DV)

    def finish(o):
        return from_heads(rms_norm(o, norm_w) * (1.0 - lam_init))

    out_x = finish(o_x)
    out_c = finish(diff_attend(qc, kc, vc, lam)) if need_ctx else None
    return out_c, out_x


def even_mixer(hc, hx, w_in, w_out, gla_w_gate2, gla_b_gate, gla_norm_w, diff_lambda, diff_norm_w,
               layer, ang_row, ang_col, need_ctx):
    pc = _split(hc @ w_in, EV_SIZES)
    px = _split(hx @ w_in, EV_SIZES)
    gla_c, gla_x = gla_mixer(pc[:6], px[:6], gla_w_gate2, gla_b_gate, gla_norm_w, need_ctx)
    dif_c, dif_x = diff_mixer(pc[6:], px[6:], diff_lambda, diff_norm_w, layer, ang_row, ang_col, need_ctx)
    out_x = jnp.concatenate([gla_x, dif_x], axis=-1) @ w_out
    out_c = jnp.concatenate([gla_c, dif_c], axis=-1) @ w_out if need_ctx else None
    return out_c, out_x


def ssd_chunked(x, dt, a, bm, cm, s0, with_out):
    bsz, t, h, p = x.shape
    g, s = bm.shape[-2:]
    e, c = h // g, SSD_CHUNK
    n = t // c
    x = x.astype(F32).reshape(bsz, n, c, g, e, p)
    dt = dt.astype(F32).reshape(bsz, n, c, g, e)
    bm = bm.astype(F32).reshape(bsz, n, c, g, s)
    cm = cm.astype(F32).reshape(bsz, n, c, g, s)
    acs = jnp.cumsum(dt * a.astype(F32).reshape(g, e), axis=2)
    a_last = acs[:, :, -1]
    xdt = x * dt[..., None]
    chunk_state = jnp.einsum('bncgs,bncge,bncgep->bngeps', bm, jnp.exp(a_last[:, :, None] - acs), xdt)

    def step(st, inp):
        d, cs = inp
        return jnp.exp(d)[..., None, None] * st + cs, st

    s_final, s_start = lax.scan(step, s0.astype(F32),
                                (jnp.moveaxis(a_last, 1, 0), jnp.moveaxis(chunk_state, 1, 0)))
    if not with_out:
        return None, s_final
    s_start = jnp.moveaxis(s_start, 0, 1)
    acs_t = jnp.moveaxis(acs, 2, -1)
    mask = jnp.tril(jnp.ones((c, c), dtype=bool))
    decay = jnp.exp(jnp.where(mask, acs_t[..., :, None] - acs_t[..., None, :], -jnp.inf))
    cb = jnp.einsum('bnigs,bnjgs->bngij', cm, bm)
    y_diag = jnp.einsum('bngij,bngeij,bnjgep->bnigep', cb, decay, xdt)
    y_off = jnp.einsum('bncgs,bngeps,bncge->bncgep', cm, s_start, jnp.exp(acs))
    return (y_diag + y_off).reshape(bsz, t, h, p), s_final


def odd_mixer(hc, hx, w_in, conv_w, conv_b, dt_bias, a_log, d_skip, norm_w, w_out, need_ctx):
    def prep(hh):
        b, s, _ = hh.shape
        z, xbc, dt = _split(hh @ w_in, (SSD_D_INNER, SSD_CONV_CH, 2 * SSD_HEADS))
        xbc = jax.nn.silu(centred_dwconv(xbc, conv_w) + conv_b)
        xs, bm, cm = _split(xbc, (SSD_D_INNER, SSD_GROUPS * SSD_STATE, SSD_GROUPS * SSD_STATE))
        dt = jax.nn.softplus(dt.astype(F32).reshape(b, s, 2, SSD_HEADS) + dt_bias.astype(F32))
        return (z, xs.reshape(b, s, SSD_HEADS, SSD_HEAD_DIM), bm.reshape(b, s, SSD_GROUPS, SSD_STATE),
                cm.reshape(b, s, SSD_GROUPS, SSD_STATE), dt[:, :, 0], dt[:, :, 1])

    zc, xc, bc, cc, dfc, dbc = prep(hc)
    zx, xx, bx, cx, dfx, dbx = prep(hx)
    a = -jnp.exp(a_log.astype(F32))
    flip = lambda t: jnp.flip(t, axis=1)
    s0 = jnp.zeros((hc.shape[0], SSD_GROUPS, SSD_HEADS // SSD_GROUPS, SSD_HEAD_DIM, SSD_STATE), F32)
    y_cf, s_f = ssd_chunked(xc, dfc, a[0], bc, cc, s0, need_ctx)
    y_cb, s_b = ssd_chunked(flip(xc), flip(dbc), a[1], flip(bc), flip(cc), s0, need_ctx)
    y_xf, _ = ssd_chunked(xx, dfx, a[0], bx, cx, s_f, True)
    y_xb, _ = ssd_chunked(flip(xx), flip(dbx), a[1], flip(bx), flip(cx), s_b, True)

    def finish(yf, yb, xs, z):
        y = yf + flip(yb) + d_skip.astype(F32)[:, None] * xs.astype(F32)
        b, s = y.shape[:2]
        y = y.reshape(b, s, SSD_D_INNER) * jax.nn.silu(z.astype(F32))
        return group_rms_norm(y, norm_w).astype(z.dtype) @ w_out

    out_x = finish(y_xf, y_xb, xx, zx)
    out_c = finish(y_cf, y_cb, xc, zc) if need_ctx else None
    return out_c, out_x


def setup_inputs(seed: int = 0) -> dict:
    key = jax.random.key(seed)
    keys = iter(jax.random.split(key, 40))

    def nrm(shape, scale):
        return scale * jax.random.normal(next(keys), shape, F32)

    def gain(shape):
        return 1.0 + nrm(shape, 0.02)

    D = D_MODEL
    dt0 = jnp.exp(jax.random.uniform(next(keys), (N_ODD, 2, SSD_HEADS), F32, math.log(1e-3), math.log(1e-1)))
    dt_bias = dt0 + jnp.log(-jnp.expm1(-dt0))
    a_log = jnp.log(jax.random.uniform(next(keys), (N_ODD, 2, SSD_HEADS), F32, 1.0, 16.0))
    return {
        'x': nrm((BATCH, SEQ, D), 1.0),
        'c': nrm((BATCH, D), 1.0),
        'ctx': nrm((BATCH, CTX_LEN, D), 1.0),
        'c_ctx': nrm((D,), 1.0),
        'mod_w': nrm((DEPTH, D, 6 * D), D ** -0.5),
        'mod_b': nrm((DEPTH, 6 * D), 0.01),
        'ln_g': gain((DEPTH, 2, D)),
        'ln_b': nrm((DEPTH, 2, D), 0.02),
        'ffn_w_in': nrm((DEPTH, D, 2 * FFN_HIDDEN), D ** -0.5),
        'ffn_w_out': nrm((DEPTH, FFN_HIDDEN, D), BETA * FFN_HIDDEN ** -0.5),
        'ev_w_in': nrm((N_EVEN, D, EV_IN), D ** -0.5),
        'ev_w_out': nrm((N_EVEN, EV_MIX, D), BETA * EV_MIX ** -0.5),
        'gla_w_gate2': nrm((N_EVEN, 2, GLA_GATE_RANK, GLA_HEADS * GLA_DK), GLA_GATE_RANK ** -0.5),
        'gla_b_gate': nrm((N_EVEN, 2, GLA_HEADS * GLA_DK), 0.1),
        'gla_norm_w': gain((N_EVEN, GLA_DV)),
        'diff_lambda': nrm((N_EVEN, 4, DIFF_DQK), 0.1),
        'diff_norm_w': gain((N_EVEN, DIFF_DV)),
        'ssd_w_in': nrm((N_ODD, D, SSD_IN), D ** -0.5),
        'ssd_conv_w': nrm((N_ODD, SSD_CONV, SSD_CONV_CH), SSD_CONV ** -0.5),
        'ssd_conv_b': nrm((N_ODD, SSD_CONV_CH), 0.02),
        'ssd_dt_bias': dt_bias,
        'ssd_a_log': a_log,
        'ssd_d': gain((N_ODD, SSD_HEADS)),
        'ssd_norm_w': gain((N_ODD, SSD_D_INNER)),
        'ssd_w_out': nrm((N_ODD, SSD_D_INNER, D), BETA * SSD_D_INNER ** -0.5),
    }


def reference(x, c, ctx, c_ctx, mod_w, mod_b, ln_g, ln_b, ffn_w_in, ffn_w_out, ev_w_in, ev_w_out,
              gla_w_gate2, gla_b_gate, gla_norm_w, diff_lambda, diff_norm_w, ssd_w_in, ssd_conv_w,
              ssd_conv_b, ssd_dt_bias, ssd_a_log, ssd_d, ssd_norm_w, ssd_w_out):
    ang_row, ang_col = rope_angles(x.shape[1])
    for layer in range(DEPTH):
        need_ctx = layer < DEPTH - 1
        mod_x = jax.nn.silu(c) @ mod_w[layer] + mod_b[layer]
        mod_c = jax.nn.silu(c_ctx) @ mod_w[layer] + mod_b[layer]
        shm_x, scm_x, gm_x, shf_x, scf_x, gf_x = [m[:, None, :] for m in jnp.split(mod_x, 6, axis=-1)]
        shm_c, scm_c, gm_c, shf_c, scf_c, gf_c = jnp.split(mod_c, 6, axis=-1)
        hx = x * (1.0 + scm_x) + shm_x
        hc = ctx * (1.0 + scm_c) + shm_c
        if layer % 2 == 0:
            e = layer // 2
            mix_c, mix_x = even_mixer(hc, hx, ev_w_in[e], ev_w_out[e], gla_w_gate2[e], gla_b_gate[e],
                                      gla_norm_w[e], diff_lambda[e], diff_norm_w[e], layer,
                                      ang_row, ang_col, need_ctx)
        else:
            o = layer // 2
            mix_c, mix_x = odd_mixer(hc, hx, ssd_w_in[o], ssd_conv_w[o], ssd_conv_b[o], ssd_dt_bias[o],
                                     ssd_a_log[o], ssd_d[o], ssd_norm_w[o], ssd_w_out[o], need_ctx)
        x = layer_norm(ALPHA * x + gm_x * mix_x, ln_g[layer, 0], ln_b[layer, 0])
        x = layer_norm(ALPHA * x + gf_x * swiglu(x * (1.0 + scf_x) + shf_x, ffn_w_in[layer], ffn_w_out[layer]),
                       ln_g[layer, 1], ln_b[layer, 1])
        if need_ctx:
            ctx = layer_norm(ALPHA * ctx + gm_c * mix_c, ln_g[layer, 0], ln_b[layer, 0])
            ctx = layer_norm(ALPHA * ctx + gf_c * swiglu(ctx * (1.0 + scf_c) + shf_c, ffn_w_in[layer],
                                                         ffn_w_out[layer]),
                             ln_g[layer, 1], ln_b[layer, 1])
    return x
```

```python
import functools
import math

import jax
import jax.numpy as jnp
from jax import lax
from jax.experimental import pallas as pl
from jax.experimental.pallas import tpu as pltpu

F32 = jnp.float32
BF16 = jnp.bfloat16

GRID_W = 64
GLA_HEADS, GLA_DK, GLA_DV, GLA_RANK, GLA_CHUNK = 4, 64, 128, 16, 64
GLA_INV_TAU = 1.0 / 16.0
DIFF_HEADS, DIFF_DQK, DIFF_DV = 4, 64, 128
ROPE_BASE = 10000.0
SSD_HEAD_DIM, SSD_GROUPS, SSD_STATE, SSD_CONV, SSD_CHUNK = 64, 4, 128, 5, 128
SSD_HEADS_PER_GROUP = 8
LN_EPS = 1e-6
RMS_EPS = 1e-6

EV_GLA_W = GLA_HEADS * GLA_DK
EV_GLA_V = GLA_HEADS * GLA_DV
EV_MAIN = 2 * EV_GLA_W + 2 * EV_GLA_V + 128
EV_DIFF = DIFF_HEADS * 2 * DIFF_DQK
EV_REAL_MAIN = 2 * EV_GLA_W + 2 * EV_GLA_V + 2 * GLA_RANK

ROW_TILE = 512
MIX_BLOCK = 256
ATTN_TQ = 1024
ATTN_TK = 1024
VMEM_LIMIT = 56 * 1024 * 1024


def _cparams(sem):
    return pltpu.CompilerParams(dimension_semantics=sem, vmem_limit_bytes=VMEM_LIMIT)


def _resident(shape):
    nd = len(shape)
    return pl.BlockSpec(shape, lambda *_: (0,) * nd, pipeline_mode=pl.Buffered(1))


def _silu(v):
    return v / (1.0 + jnp.exp(-v))


def _layer_norm(v, g, b):
    mu = jnp.mean(v, axis=-1, keepdims=True)
    d = v - mu
    var = jnp.mean(d * d, axis=-1, keepdims=True)
    return d * lax.rsqrt(var + LN_EPS) * g + b


def _split_dot(mat_bf16, v):
    hi = v.astype(BF16)
    lo = (v - hi.astype(F32)).astype(BF16)
    return (jnp.dot(mat_bf16, hi, preferred_element_type=F32)
            + jnp.dot(mat_bf16, lo, preferred_element_type=F32))


def _split_dot_r(v, mat_bf16):
    hi = v.astype(BF16)
    lo = (v - hi.astype(F32)).astype(BF16)
    return (jnp.dot(hi, mat_bf16, preferred_element_type=F32)
            + jnp.dot(lo, mat_bf16, preferred_element_type=F32))


def _mod_kernel(c_ref, w_ref, b_ref, o_ref):
    s = _silu(c_ref[...]).astype(BF16)
    o_ref[...] = jnp.dot(s, w_ref[...].astype(BF16), preferred_element_type=F32) + b_ref[...]


def _mod_call(cc, mod_w, mod_b):
    depth, d, n = mod_w.shape
    cw = d
    return pl.pallas_call(
        _mod_kernel,
        grid=(depth, n // cw),
        in_specs=[pl.BlockSpec((8, d), lambda l, j: (0, 0)),
                  pl.BlockSpec((None, d, cw), lambda l, j: (l, 0, j)),
                  pl.BlockSpec((None, 1, cw), lambda l, j: (l, 0, j))],
        out_specs=pl.BlockSpec((None, 8, cw), lambda l, j: (l, 0, j)),
        out_shape=jax.ShapeDtypeStruct((depth, 8, n), F32),
        compiler_params=_cparams(("parallel", "parallel")),
        name="mod_vectors",
    )(cc, mod_w, mod_b.reshape(depth, 1, n))


class _Geom:
    def __init__(self, batch, seq, ctx_len, d_model):
        self.b, self.t, self.c, self.d = batch, seq, ctx_len, d_model
        self.rows = batch * (seq + ctx_len)
        self.tm = min(ROW_TILE, batch * ctx_len)
        assert (batch * ctx_len) % self.tm == 0 and seq % self.tm == 0
        self.ctx_tiles = batch * ctx_len // self.tm
        self.tiles_per_batch = seq // self.tm
        self.n_tiles = self.rows // self.tm
        self.blk = MIX_BLOCK
        assert ctx_len == self.blk and seq % self.blk == 0
        self.ctx_blocks = batch
        self.nb = seq // self.blk

    def mod_row(self, tile):
        return jnp.where(tile < self.ctx_tiles, self.b, (tile - self.ctx_tiles) // self.tiles_per_batch)

    def fwd_block(self, b, s):
        return jnp.where(s == 0, b, self.ctx_blocks + self.nb * b + s - 1)

    def bwd_block(self, b, s):
        return jnp.where(s == 0, b, self.ctx_blocks + self.nb * b + self.nb - s)


def _proj_even_kernel(x_ref, mod_ref, w_ref, rope_ref, ymain_ref, k2_ref, qt_ref, vt_ref):
    x = x_ref[...]
    h = (x * (1.0 + mod_ref[1:2, :]) + mod_ref[0:1, :]).astype(BF16)
    y = jnp.dot(h, w_ref[...], preferred_element_type=F32)
    ymain_ref[...] = y[:, :EV_MAIN]
    cos = rope_ref[:, 0:128]
    sin_up = rope_ref[:, 128:256]
    sin_dn = rope_ref[:, 256:384]

    def rope(t):
        return t * cos + pltpu.roll(t, 112, 1) * sin_up + pltpu.roll(t, 16, 1) * sin_dn

    for j in range(DIFF_HEADS):
        lo, hi = 128 * j, 128 * (j + 1)
        q = rope(y[:, EV_MAIN + lo:EV_MAIN + hi]) * (DIFF_DQK ** -0.5)
        qt_ref[lo:hi, :] = q.T.astype(BF16)
        k = rope(y[:, EV_MAIN + EV_DIFF + lo:EV_MAIN + EV_DIFF + hi])
        k2_ref[:, lo:hi] = k.astype(BF16)
        v = y[:, EV_MAIN + 2 * EV_DIFF + lo:EV_MAIN + 2 * EV_DIFF + hi]
        vt_ref[lo:hi, :] = v.T.astype(BF16)


def _proj_even_call(g, s_rows, mod_l, w, rope_tab):
    tm, d = g.tm, g.d
    n = w.shape[1]
    rope_tiles_ctx = 1

    def rope_idx(i):
        return jnp.where(i < g.ctx_tiles, 0, rope_tiles_ctx + (i - g.ctx_tiles) % g.tiles_per_batch)

    return pl.pallas_call(
        _proj_even_kernel,
        grid=(g.n_tiles,),
        in_specs=[pl.BlockSpec((tm, d), lambda i: (i, 0)),
                  pl.BlockSpec((None, 6, d), lambda i: (g.mod_row(i), 0, 0)),
                  _resident((d, n)),
                  pl.BlockSpec((tm, 384), lambda i: (rope_idx(i), 0))],
        out_specs=[pl.BlockSpec((tm, EV_MAIN), lambda i: (i, 0)),
                   pl.BlockSpec((tm, EV_DIFF), lambda i: (i, 0)),
                   pl.BlockSpec((EV_DIFF, tm), lambda i: (0, i)),
                   pl.BlockSpec((EV_DIFF, tm), lambda i: (0, i))],
        out_shape=[jax.ShapeDtypeStruct((g.rows, EV_MAIN), F32),
                   jax.ShapeDtypeStruct((g.rows, EV_DIFF), BF16),
                   jax.ShapeDtypeStruct((EV_DIFF, g.rows), BF16),
                   jax.ShapeDtypeStruct((EV_DIFF, g.rows), BF16)],
        compiler_params=_cparams(("parallel",)),
        name="proj_even",
    )(s_rows, mod_l, w, rope_tab)


def _rope_table(g):
    rows = g.t // GRID_W
    row = jnp.repeat(jnp.arange(rows, dtype=F32), GRID_W)
    col = jnp.tile(jnp.arange(GRID_W, dtype=F32), rows)
    n_freq = DIFF_DQK // 4
    inv = ROPE_BASE ** (-jnp.arange(n_freq, dtype=F32) / n_freq)
    ar, ac = row[:, None] * inv, col[:, None] * inv
    z = jnp.zeros_like(ar)
    cos64 = jnp.concatenate([jnp.cos(ar), jnp.cos(ar), jnp.cos(ac), jnp.cos(ac)], axis=-1)
    up64 = jnp.concatenate([-jnp.sin(ar), z, -jnp.sin(ac), z], axis=-1)
    dn64 = jnp.concatenate([z, jnp.sin(ar), z, jnp.sin(ac)], axis=-1)
    tab = jnp.concatenate([cos64, cos64, up64, up64, dn64, dn64], axis=-1)
    ident = jnp.concatenate([jnp.ones((g.tm, 128), F32), jnp.zeros((g.tm, 256), F32)], axis=-1)
    return jnp.concatenate([ident, tab], axis=0)


def _gla_kernel(qf, kf, vf, rf, qb, kb, vb, rb, wg_ref, bg_ref, of_ref, ob_ref, st_ref):
    s = pl.program_id(1)

    @pl.when(s == 0)
    def _():
        st_ref[...] = jnp.zeros_like(st_ref)

    n = MIX_BLOCK
    nch = n // GLA_CHUNK
    ri = lax.broadcasted_iota(jnp.int32, (n, n), 0)
    ci = lax.broadcasted_iota(jnp.int32, (n, n), 1)
    same_chunk = (ri // GLA_CHUNK) == (ci // GLA_CHUNK)
    col_chunk = ci // GLA_CHUNK
    lane_head = lax.broadcasted_iota(jnp.int32, (1, EV_GLA_W), 1) // GLA_DK
    diag_blocks = ((lax.broadcasted_iota(jnp.int32, (EV_GLA_W, EV_GLA_V), 0) // GLA_DK)
                   == (lax.broadcasted_iota(jnp.int32, (EV_GLA_W, EV_GLA_V), 1) // GLA_DV))

    for d, (q_ref, k_ref, v_ref, r_ref, o_ref) in enumerate(
            ((qf, kf, vf, rf, of_ref), (qb, kb, vb, rb, ob_ref))):
        rev = d == 1
        causal = jnp.logical_and(same_chunk, (ci >= ri) if rev else (ci <= ri))
        z = jnp.dot(r_ref[...].astype(BF16), wg_ref[d], preferred_element_type=F32) + bg_ref[d]
        la = (jnp.minimum(z, 0.0) - jnp.log1p(jnp.exp(-jnp.abs(z)))) * GLA_INV_TAU
        cum = _split_dot(jnp.where(causal, 1.0, 0.0).astype(BF16), la)
        tot = jnp.concatenate(
            [jnp.broadcast_to(jnp.sum(la[c * GLA_CHUNK:(c + 1) * GLA_CHUNK], axis=0, keepdims=True),
                              (GLA_CHUNK, EV_GLA_W)) for c in range(nch)], axis=0)
        q = q_ref[...]
        k = k_ref[...]
        v16 = v_ref[...].astype(BF16)
        qd = q * jnp.exp(cum) * (GLA_DK ** -0.5)
        ki = (k * jnp.exp(-cum)).astype(BF16)
        kd = k * jnp.exp(tot - cum)

        parts = []
        for h in range(GLA_HEADS):
            qh = jnp.where(lane_head == h, qd, 0.0).astype(BF16)
            att = lax.dot_general(qh, ki, (((1,), (1,)), ((), ())), preferred_element_type=F32)
            att = jnp.where(causal, att, 0.0).astype(BF16)
            parts.append(jnp.dot(att, v16[:, h * GLA_DV:(h + 1) * GLA_DV], preferred_element_type=F32))
        o_intra = jnp.concatenate(parts, axis=1)

        la_t = la.T
        kd_t = kd.T
        qd16 = qd.astype(BF16)
        state = st_ref[d]
        o_inter = [None] * nch
        for c in (range(nch - 1, -1, -1) if rev else range(nch)):
            in_chunk = col_chunk == c
            o_inter[c] = jnp.dot(qd16[c * GLA_CHUNK:(c + 1) * GLA_CHUNK], state.astype(BF16),
                                 preferred_element_type=F32)
            kv = jnp.dot(jnp.where(in_chunk, kd_t, 0.0).astype(BF16), v16, preferred_element_type=F32)
            decay = jnp.exp(jnp.sum(jnp.where(in_chunk, la_t, 0.0), axis=1, keepdims=True))
            state = decay * state + jnp.where(diag_blocks, kv, 0.0)
        st_ref[d] = state
        o_ref[...] = o_intra + jnp.concatenate(o_inter, axis=0)


def _gla_call(g, ymain, wg, bg):
    blk = g.blk
    qw, vw = EV_GLA_W, EV_GLA_V

    def specs(block_fn):
        return [pl.BlockSpec((blk, qw), lambda b, s: (block_fn(b, s), 0)),
                pl.BlockSpec((blk, qw), lambda b, s: (block_fn(b, s), 1)),
                pl.BlockSpec((blk, vw), lambda b, s: (block_fn(b, s), 1)),
                pl.BlockSpec((blk, 128), lambda b, s: (block_fn(b, s), (EV_MAIN - 128) // 128))]

    return pl.pallas_call(
        _gla_kernel,
        grid=(g.b, g.nb + 1),
        in_specs=specs(g.fwd_block) + specs(g.bwd_block) + [
            pl.BlockSpec((2, 128, qw), lambda b, s: (0, 0, 0)),
            pl.BlockSpec((2, 1, qw), lambda b, s: (0, 0, 0))],
        out_specs=[pl.BlockSpec((blk, vw), lambda b, s: (g.fwd_block(b, s), 0)),
                   pl.BlockSpec((blk, vw), lambda b, s: (g.bwd_block(b, s), 0))],
        out_shape=[jax.ShapeDtypeStruct((g.rows, vw), F32)] * 2,
        scratch_shapes=[pltpu.VMEM((2, qw, vw), F32)],
        compiler_params=_cparams(("parallel", "arbitrary")),
        name="gla_scan",
    )(ymain, ymain, ymain, ymain, ymain, ymain, ymain, ymain, wg, bg)


def _attn_kernel(lam_ref, nw_ref, qt_ref, kc_ref, vct_ref, *rest, lam_init, with_latent):
    if with_latent:
        kx_ref, vxt_ref, o_ref, m_sc, l_sc, acc_sc = rest
        ki = pl.program_id(3)
        last = pl.num_programs(3) - 1
    else:
        _, o_ref, m_sc, l_sc, acc_sc = rest
    qt = qt_ref[...]
    row = lax.broadcasted_iota(jnp.int32, qt.shape, 0)
    zero = jnp.zeros_like(qt)
    qt_maps = (jnp.where(row < DIFF_DQK, qt, zero), jnp.where(row >= DIFF_DQK, qt, zero))

    def update(k, vt):
        for m in range(2):
            st = jnp.dot(k, qt_maps[m], preferred_element_type=F32)
            m_old = m_sc[m]
            m_new = jnp.maximum(m_old, jnp.max(st, axis=0, keepdims=True))
            alpha = jnp.exp(m_old - m_new)
            p = jnp.exp(st - m_new)
            l_sc[m] = alpha * l_sc[m] + jnp.sum(p, axis=0, keepdims=True)
            acc_sc[m] = alpha * acc_sc[m] + jnp.dot(vt, p.astype(BF16), preferred_element_type=F32)
            m_sc[m] = m_new

    def init_and_ctx():
        m_sc[...] = jnp.full(m_sc.shape, -jnp.inf, F32)
        l_sc[...] = jnp.zeros(l_sc.shape, F32)
        acc_sc[...] = jnp.zeros(acc_sc.shape, F32)
        update(kc_ref[...], vct_ref[...])

    def finish():
        lp = lam_ref[...]
        lam = (jnp.exp(jnp.sum(lp[0:1] * lp[1:2], keepdims=True))
               - jnp.exp(jnp.sum(lp[2:3] * lp[3:4], keepdims=True)) + lam_init)
        ot = acc_sc[0] / l_sc[0] - lam * (acc_sc[1] / l_sc[1])
        ms = jnp.mean(ot * ot, axis=0, keepdims=True)
        on = ot * lax.rsqrt(ms + RMS_EPS) * nw_ref[...] * (1.0 - lam_init)
        o_ref[...] = on.T.astype(o_ref.dtype)

    if with_latent:
        pl.when(ki == 0)(init_and_ctx)
        update(kx_ref[...], vxt_ref[...])
        pl.when(ki == last)(finish)
    else:
        init_and_ctx()
        finish()


def _attn_call(g, k2, qt, vt, lam_p, norm_w_col, lam_init, od_prev=None):
    hd = 2 * DIFF_DQK
    tq = min(ATTN_TQ, g.t)
    tk = min(ATTN_TK, g.t)
    assert (g.b * g.c) % tq == 0 and g.t % tq == 0 and g.t % tk == 0 and (g.b * g.c) % tk == 0
    q_off, k_off = g.b * g.c // tq, g.b * g.c // tk
    nq, nk = g.t // tq, g.t // tk
    scratch = lambda nqry: [pltpu.VMEM((2, 1, nqry), F32), pltpu.VMEM((2, 1, nqry), F32),
                            pltpu.VMEM((2, DIFF_DV, nqry), F32)]
    common = [pl.BlockSpec((4, DIFF_DQK), lambda *_: (0, 0)),
              pl.BlockSpec((DIFF_DV, 1), lambda *_: (0, 0))]
    od = pl.pallas_call(
        functools.partial(_attn_kernel, lam_init=lam_init, with_latent=True),
        grid=(g.b, DIFF_HEADS, nq, nk),
        in_specs=common + [
            pl.BlockSpec((hd, tq), lambda b, h, qi, ki: (h, q_off + b * nq + qi)),
            pl.BlockSpec((g.c, hd), lambda b, h, qi, ki: (b, h)),
            pl.BlockSpec((DIFF_DV, g.c), lambda b, h, qi, ki: (h, b)),
            pl.BlockSpec((tk, hd), lambda b, h, qi, ki: (k_off + b * nk + ki, h)),
            pl.BlockSpec((DIFF_DV, tk), lambda b, h, qi, ki: (h, k_off + b * nk + ki))],
        out_specs=pl.BlockSpec((tq, DIFF_DV), lambda b, h, qi, ki: (q_off + b * nq + qi, h)),
        out_shape=jax.ShapeDtypeStruct((g.rows, DIFF_HEADS * DIFF_DV), BF16),
        scratch_shapes=scratch(tq),
        compiler_params=_cparams(("parallel", "parallel", "parallel", "arbitrary")),
        name="diff_attn_latent",
    )(lam_p, norm_w_col, qt, k2, vt, k2, vt)
    return od


def _attn_ctx_call(g, k2, qt, vt, lam_p, norm_w_col, lam_init, od):
    hd = 2 * DIFF_DQK
    return pl.pallas_call(
        functools.partial(_attn_kernel, lam_init=lam_init, with_latent=False),
        grid=(g.b, DIFF_HEADS),
        in_specs=[pl.BlockSpec((4, DIFF_DQK), lambda *_: (0, 0)),
                  pl.BlockSpec((DIFF_DV, 1), lambda *_: (0, 0)),
                  pl.BlockSpec((hd, g.c), lambda b, h: (h, b)),
                  pl.BlockSpec((g.c, hd), lambda b, h: (b, h)),
                  pl.BlockSpec((DIFF_DV, g.c), lambda b, h: (h, b)),
                  pl.BlockSpec(memory_space=pl.ANY)],
        out_specs=pl.BlockSpec((g.c, DIFF_DV), lambda b, h: (b, h)),
        out_shape=jax.ShapeDtypeStruct(od.shape, od.dtype),
        scratch_shapes=[pltpu.VMEM((2, 1, g.c), F32), pltpu.VMEM((2, 1, g.c), F32),
                        pltpu.VMEM((2, DIFF_DV, g.c), F32)],
        input_output_aliases={5: 0},
        compiler_params=_cparams(("parallel", "parallel")),
        name="diff_attn_context",
    )(lam_p, norm_w_col, qt, k2, vt, od)


def _proj_odd_kernel(x_ref, mod_ref, w_ref, y_ref):
    h = (x_ref[...] * (1.0 + mod_ref[1:2, :]) + mod_ref[0:1, :]).astype(BF16)
    y_ref[...] = jnp.dot(h, w_ref[...], preferred_element_type=F32)


def _proj_odd_call(g, s_rows, mod_l, w):
    tm = g.tm // 2
    n = w.shape[1]
    return pl.pallas_call(
        _proj_odd_kernel,
        grid=(g.rows // tm,),
        in_specs=[pl.BlockSpec((tm, g.d), lambda i: (i, 0)),
                  pl.BlockSpec((None, 6, g.d), lambda i: (g.mod_row(i // 2), 0, 0)),
                  _resident((g.d, n))],
        out_specs=pl.BlockSpec((tm, n), lambda i: (i, 0)),
        out_shape=jax.ShapeDtypeStruct((g.rows, n), F32),
        compiler_params=_cparams(("parallel",)),
        name="proj_odd",
    )(s_rows, mod_l, w)


def _conv_kernel(x_ref, prev_ref, next_ref, w_ref, b_ref, o_ref, *, ctx_blocks, nb):
    i = pl.program_id(0)
    j = (i - ctx_blocks) % nb
    is_lat = i >= ctx_blocks
    has_prev = jnp.logical_and(is_lat, j > 0).astype(F32)
    has_next = jnp.logical_and(is_lat, j < nb - 1).astype(F32)
    x = x_ref[...]
    n = x.shape[0]
    pv = prev_ref[...] * has_prev
    nx = next_ref[...] * has_next
    w = w_ref[...]
    r8 = lax.broadcasted_iota(jnp.int32, pv.shape, 0)
    acc = x * w[2:3]
    for tap, sh in ((0, 2), (1, 1)):
        rolled = pltpu.roll(x, sh, 0)
        head = jnp.where(r8 < sh, pltpu.roll(pv, sh, 0), rolled[0:8])
        acc = acc + jnp.concatenate([head, rolled[8:]], axis=0) * w[tap:tap + 1]
    for tap, sh in ((3, 1), (4, 2)):
        rolled = pltpu.roll(x, n - sh, 0)
        tail = jnp.where(r8 >= 8 - sh, pltpu.roll(nx, 8 - sh, 0), rolled[n - 8:n])
        acc = acc + jnp.concatenate([rolled[:n - 8], tail], axis=0) * w[tap:tap + 1]
    o_ref[...] = _silu(acc + b_ref[...])


def _conv_call(g, y_odd, conv_w8, conv_b, d_inner, conv_ch):
    blk, cw = g.blk, 1024
    col0 = d_inner // cw
    nrow8 = g.rows // 8
    per = blk // 8
    return pl.pallas_call(
        functools.partial(_conv_kernel, ctx_blocks=g.ctx_blocks, nb=g.nb),
        grid=(g.rows // blk, conv_ch // cw),
        in_specs=[pl.BlockSpec((blk, cw), lambda i, j: (i, col0 + j)),
                  pl.BlockSpec((8, cw), lambda i, j: (jnp.maximum(i * per - 1, 0), col0 + j)),
                  pl.BlockSpec((8, cw), lambda i, j: (jnp.minimum((i + 1) * per, nrow8 - 1), col0 + j)),
                  pl.BlockSpec((8, cw), lambda i, j: (0, j)),
                  pl.BlockSpec((1, cw), lambda i, j: (0, j))],
        out_specs=pl.BlockSpec((blk, cw), lambda i, j: (i, j)),
        out_shape=jax.ShapeDtypeStruct((g.rows, conv_ch), F32),
        compiler_params=_cparams(("parallel", "parallel")),
        name="ssd_conv",
    )(y_odd, y_odd, y_odd, conv_w8, conv_b)


def _ssd_kernel(xf, bmf, cmf, dtf, xb, bmb, cmb, dtb, bias_ref, alog_ref, yf_ref, yb_ref, st_ref):
    s = pl.program_id(2)

    @pl.when(s == 0)
    def _():
        st_ref[...] = jnp.zeros_like(st_ref)

    n, ch = MIX_BLOCK, SSD_CHUNK
    nch = n // ch
    gw = SSD_HEADS_PER_GROUP * SSD_HEAD_DIM
    ri = lax.broadcasted_iota(jnp.int32, (n, n), 0)
    ci = lax.broadcasted_iota(jnp.int32, (n, n), 1)
    same_chunk = (ri // ch) == (ci // ch)
    ri_c = lax.broadcasted_iota(jnp.int32, (ch, ch), 0)
    ci_c = lax.broadcasted_iota(jnp.int32, (ch, ch), 1)
    lane_c = lax.broadcasted_iota(jnp.int32, (ch, 128), 1)
    a_neg = -jnp.exp(alog_ref[...])
    exp_row = lax.broadcasted_iota(jnp.int32, (128, gw), 0)
    exp_col = lax.broadcasted_iota(jnp.int32, (128, gw), 1) // SSD_HEAD_DIM

    for d, (x_ref, bm_ref, cm_ref, dt_ref, y_ref) in enumerate(
            ((xf, bmf, cmf, dtf, yf_ref), (xb, bmb, cmb, dtb, yb_ref))):
        rev = d == 1
        lane0 = SSD_HEADS_PER_GROUP * d
        expand = jnp.where(exp_row == exp_col + lane0, 1.0, 0.0).astype(BF16)
        z = dt_ref[...] + bias_ref[...]
        dt = jnp.maximum(z, 0.0) + jnp.log1p(jnp.exp(-jnp.abs(z)))
        causal_blk = jnp.logical_and(same_chunk, (ci >= ri) if rev else (ci <= ri))
        acs = _split_dot(jnp.where(causal_blk, 1.0, 0.0).astype(BF16), dt * a_neg)
        acs_t = acs.T
        dt_t = dt.T
        tri = (ci_c >= ri_c) if rev else (ci_c <= ri_c)
        state = st_ref[d]
        for c in (range(nch - 1, -1, -1) if rev else range(nch)):
            r0, r1 = c * ch, (c + 1) * ch
            xc = x_ref[r0:r1, :]
            bmc = bm_ref[r0:r1, :]
            cmc = cm_ref[r0:r1, :]
            acs_c = acs[r0:r1, :]
            acs_tc = acs_t[:, r0:r1]
            dt_tc = dt_t[:, r0:r1]
            dt_c = dt[r0:r1, :]
            cm16 = cmc.astype(BF16)
            cb = lax.dot_general(cm16, bmc.astype(BF16), (((1,), (1,)), ((), ())),
                                 preferred_element_type=F32)
            pairs = []
            for pp in range(SSD_HEADS_PER_GROUP // 2):
                mats = []
                for e in (2 * pp, 2 * pp + 1):
                    ln = lane0 + e
                    seg = jnp.exp(jnp.where(tri, acs_c[:, ln:ln + 1] - acs_tc[ln:ln + 1, :], -jnp.inf))
                    mats.append((cb * seg * dt_tc[ln:ln + 1, :]).astype(BF16))
                xp = xc[:, 128 * pp:128 * (pp + 1)]
                x2 = jnp.concatenate([jnp.where(lane_c < SSD_HEAD_DIM, xp, 0.0),
                                      jnp.where(lane_c >= SSD_HEAD_DIM, xp, 0.0)], axis=0).astype(BF16)
                pairs.append(jnp.dot(jnp.concatenate(mats, axis=1), x2, preferred_element_type=F32))
            y_diag = jnp.concatenate(pairs, axis=1)
            a_last = acs_c[0:1, :] if rev else acs_c[ch - 1:ch, :]
            fac = jnp.concatenate([jnp.exp(acs_c), dt_c * jnp.exp(a_last - acs_c),
                                   jnp.broadcast_to(jnp.exp(a_last), (8, 128))], axis=0)
            fac = _split_dot_r(fac, expand)
            y_off = jnp.dot(cm16, state.astype(BF16), preferred_element_type=F32) * fac[0:ch]
            y_ref[r0:r1, :] = y_diag + y_off
            upd = jnp.dot(bmc.T.astype(BF16), (xc * fac[ch:2 * ch]).astype(BF16),
                          preferred_element_type=F32)
            state = fac[2 * ch:2 * ch + 1] * state + upd
        st_ref[d] = state


def _ssd_call(g, xbc, y_odd, dt_bias_slab, a_log_slab, d_inner):
    blk = g.blk
    gw = SSD_HEADS_PER_GROUP * SSD_HEAD_DIM
    bm0 = d_inner // 128
    cm0 = bm0 + SSD_GROUPS
    dt0 = (y_odd.shape[1] - SSD_GROUPS * 128) // 128

    def specs(block_fn):
        return [pl.BlockSpec((blk, gw), lambda b, gi, s: (block_fn(b, s), gi)),
                pl.BlockSpec((blk, 128), lambda b, gi, s: (block_fn(b, s), bm0 + gi)),
                pl.BlockSpec((blk, 128), lambda b, gi, s: (block_fn(b, s), cm0 + gi)),
                pl.BlockSpec((blk, 128), lambda b, gi, s: (block_fn(b, s), dt0 + gi))]

    return pl.pallas_call(
        _ssd_kernel,
        grid=(g.b, SSD_GROUPS, g.nb + 1),
        in_specs=specs(g.fwd_block) + specs(g.bwd_block) + [
            pl.BlockSpec((None, 1, 128), lambda b, gi, s: (gi, 0, 0)),
            pl.BlockSpec((None, 1, 128), lambda b, gi, s: (gi, 0, 0))],
        out_specs=[pl.BlockSpec((blk, gw), lambda b, gi, s: (g.fwd_block(b, s), gi)),
                   pl.BlockSpec((blk, gw), lambda b, gi, s: (g.bwd_block(b, s), gi))],
        out_shape=[jax.ShapeDtypeStruct((g.rows, d_inner), F32)] * 2,
        scratch_shapes=[pltpu.VMEM((2, SSD_STATE, gw), F32)],
        compiler_params=_cparams(("parallel", "parallel", "arbitrary")),
        name="ssd_scan",
    )(xbc, xbc, xbc, y_odd, xbc, xbc, xbc, y_odd, dt_bias_slab, a_log_slab)


def _mix_even_kernel(x_ref, mod_ref, of_ref, ob_ref, g_ref, od_ref, nw_ref, w_ref, lng_ref, lnb_ref,
                     o_ref, *, alpha):
    o = of_ref[...] + ob_ref[...]
    gate = g_ref[...]
    nw = nw_ref[...]
    parts = []
    for h in range(GLA_HEADS):
        oh = o[:, h * GLA_DV:(h + 1) * GLA_DV]
        ms = jnp.mean(oh * oh, axis=-1, keepdims=True)
        parts.append(oh * lax.rsqrt(ms + RMS_EPS) * nw)
    gla = (jnp.concatenate(parts, axis=1) * _silu(gate)).astype(BF16)
    mixin = jnp.concatenate([gla, od_ref[...]], axis=1)
    mix = jnp.dot(mixin, w_ref[...], preferred_element_type=F32)
    v = alpha * x_ref[...] + mod_ref[2:3, :] * mix
    o_ref[...] = _layer_norm(v, lng_ref[...], lnb_ref[...])


def _mix_even_call(g, s_rows, mod_l, o_f, o_b, ymain, od, gla_nw, w_out, ln_g, ln_b, alpha):
    tm, d = g.tm, g.d
    vw = EV_GLA_V
    row = lambda i: (i, 0)
    return pl.pallas_call(
        functools.partial(_mix_even_kernel, alpha=alpha),
        grid=(g.n_tiles,),
        in_specs=[pl.BlockSpec((tm, d), row),
                  pl.BlockSpec((None, 6, d), lambda i: (g.mod_row(i), 0, 0)),
                  pl.BlockSpec((tm, vw), row),
                  pl.BlockSpec((tm, vw), row),
                  pl.BlockSpec((tm, vw), lambda i: (i, 2 * EV_GLA_W // vw + 1)),
                  pl.BlockSpec((tm, vw), row),
                  pl.BlockSpec((1, GLA_DV), lambda i: (0, 0)),
                  _resident(w_out.shape),
                  pl.BlockSpec((1, d), lambda i: (0, 0)),
                  pl.BlockSpec((1, d), lambda i: (0, 0))],
        out_specs=pl.BlockSpec((tm, d), row),
        out_shape=jax.ShapeDtypeStruct((g.rows, d), F32),
        compiler_params=_cparams(("parallel",)),
        name="mix_even",
    )(s_rows, mod_l, o_f, o_b, ymain, od, gla_nw, w_out, ln_g, ln_b)


def _mix_odd_kernel(x_ref, mod_ref, yf_ref, yb_ref, xs_ref, z_ref, dsk_ref, nw_ref, w_ref,
                    lng_ref, lnb_ref, o_ref, *, alpha):
    y = (yf_ref[...] + yb_ref[...] + dsk_ref[...] * xs_ref[...]) * _silu(z_ref[...])
    gw = y.shape[1] // SSD_GROUPS
    parts = []
    for gi in range(SSD_GROUPS):
        yg = y[:, gi * gw:(gi + 1) * gw]
        ms = jnp.mean(yg * yg, axis=-1, keepdims=True)
        parts.append(yg * lax.rsqrt(ms + RMS_EPS))
    yn = (jnp.concatenate(parts, axis=1) * nw_ref[...]).astype(BF16)
    mix = jnp.dot(yn, w_ref[...], preferred_element_type=F32)
    v = alpha * x_ref[...] + mod_ref[2:3, :] * mix
    o_ref[...] = _layer_norm(v, lng_ref[...], lnb_ref[...])


def _mix_odd_call(g, s_rows, mod_l, y_f, y_b, xbc, y_odd, d_skip, norm_w, w_out, ln_g, ln_b, alpha):
    tm = g.tm // 2
    d = g.d
    di = w_out.shape[0]
    row = lambda i: (i, 0)
    vec = lambda n: pl.BlockSpec((1, n), lambda i: (0, 0))
    return pl.pallas_call(
        functools.partial(_mix_odd_kernel, alpha=alpha),
        grid=(g.rows // tm,),
        in_specs=[pl.BlockSpec((tm, d), row),
                  pl.BlockSpec((None, 6, d), lambda i: (g.mod_row(i // 2), 0, 0)),
                  pl.BlockSpec((tm, di), row),
                  pl.BlockSpec((tm, di), row),
                  pl.BlockSpec((tm, di), row),
                  pl.BlockSpec((tm, di), row),
                  vec(di), vec(di),
                  _resident(w_out.shape),
                  vec(d), vec(d)],
        out_specs=pl.BlockSpec((tm, d), row),
        out_shape=jax.ShapeDtypeStruct((g.rows, d), F32),
        compiler_params=_cparams(("parallel",)),
        name="mix_odd",
    )(s_rows, mod_l, y_f, y_b, xbc, y_odd, d_skip, norm_w, w_out, ln_g, ln_b)


def _ffn_kernel(x_ref, mod_ref, win_ref, wout_ref, lng_ref, lnb_ref, o_ref, *, alpha, hidden, n_chunks):
    x = x_ref[...]
    h = (x * (1.0 + mod_ref[4:5, :]) + mod_ref[3:4, :]).astype(BF16)
    hc = hidden // n_chunks
    acc = jnp.zeros(x.shape, F32)
    for j in range(n_chunks):
        gate = jnp.dot(h, win_ref[:, j * hc:(j + 1) * hc], preferred_element_type=F32)
        up = jnp.dot(h, win_ref[:, hidden + j * hc:hidden + (j + 1) * hc], preferred_element_type=F32)
        act = (_silu(gate) * up).astype(BF16)
        acc = acc + jnp.dot(act, wout_ref[j * hc:(j + 1) * hc, :], preferred_element_type=F32)
    v = alpha * x + mod_ref[5:6, :] * acc
    o_ref[...] = _layer_norm(v, lng_ref[...], lnb_ref[...])


def _ffn_call(g, s_rows, mod_l, w_in, w_out, ln_g, ln_b, alpha, latent_only):
    tm, d = g.tm, g.d
    hidden = w_out.shape[0]
    n_chunks = 2 if hidden % 256 == 0 else 1
    off = g.ctx_tiles if latent_only else 0
    n_tiles = g.n_tiles - off
    return pl.pallas_call(
        functools.partial(_ffn_kernel, alpha=alpha, hidden=hidden, n_chunks=n_chunks),
        grid=(n_tiles,),
        in_specs=[pl.BlockSpec((tm, d), lambda i: (i + off, 0)),
                  pl.BlockSpec((None, 6, d), lambda i: (g.mod_row(i + off), 0, 0)),
                  _resident(w_in.shape),
                  _resident(w_out.shape),
                  pl.BlockSpec((1, d), lambda i: (0, 0)),
                  pl.BlockSpec((1, d), lambda i: (0, 0))],
        out_specs=pl.BlockSpec((tm, d), lambda i: (i, 0)),
        out_shape=jax.ShapeDtypeStruct((n_tiles * tm, d), F32),
        compiler_params=_cparams(("parallel",)),
        name="ffn",
    )(s_rows, mod_l, w_in, w_out, ln_g, ln_b)


def kernel(x, c, ctx, c_ctx, mod_w, mod_b, ln_g, ln_b, ffn_w_in, ffn_w_out, ev_w_in, ev_w_out,
           gla_w_gate2, gla_b_gate, gla_norm_w, diff_lambda, diff_norm_w, ssd_w_in, ssd_conv_w,
           ssd_conv_b, ssd_dt_bias, ssd_a_log, ssd_d, ssd_norm_w, ssd_w_out):
    batch, seq, d = x.shape
    ctx_len = ctx.shape[1]
    depth = mod_w.shape[0]
    g = _Geom(batch, seq, ctx_len, d)
    alpha = (2 * depth) ** 0.25
    d_inner = ssd_w_out.shape[1]
    conv_ch = ssd_conv_w.shape[2]
    n_heads = ssd_d.shape[1]

    s_rows = jnp.concatenate([ctx.reshape(batch * ctx_len, d), x.reshape(batch * seq, d)], axis=0)
    cc = jnp.zeros((8, d), F32).at[:batch].set(c).at[batch].set(c_ctx)
    mod_all = _mod_call(cc, mod_w, mod_b).reshape(depth, 8, 6, d)
    rope_tab = _rope_table(g)

    for layer in range(depth):
        need_ctx = layer < depth - 1
        mod_l = mod_all[layer]
        if layer % 2 == 0:
            e = layer // 2
            w = ev_w_in[e]
            w_my = jnp.concatenate([w[:, :EV_REAL_MAIN], jnp.zeros((d, 128 - 2 * GLA_RANK), w.dtype),
                                    w[:, EV_REAL_MAIN:]], axis=1).astype(BF16)
            ymain, k2, qt, vt = _proj_even_call(g, s_rows, mod_l, w_my, rope_tab)
            wg = jnp.zeros((2, 128, EV_GLA_W), F32)
            wg = wg.at[0, :GLA_RANK].set(gla_w_gate2[e, 0]).at[1, GLA_RANK:2 * GLA_RANK].set(gla_w_gate2[e, 1])
            o_f, o_b = _gla_call(g, ymain, wg.astype(BF16), gla_b_gate[e].reshape(2, 1, EV_GLA_W))
            lam_init = 0.8 - 0.6 * math.exp(-0.3 * layer)
            nw_col = diff_norm_w[e].reshape(DIFF_DV, 1)
            od = _attn_call(g, k2, qt, vt, diff_lambda[e], nw_col, lam_init)
            if need_ctx:
                od = _attn_ctx_call(g, k2, qt, vt, diff_lambda[e], nw_col, lam_init, od)
            s_rows = _mix_even_call(g, s_rows, mod_l, o_f, o_b, ymain, od, gla_norm_w[e].reshape(1, GLA_DV),
                                    ev_w_out[e].astype(BF16), ln_g[layer, 0].reshape(1, d),
                                    ln_b[layer, 0].reshape(1, d), alpha)
        else:
            o = layer // 2
            w = ssd_w_in[o]
            dt_col0 = d_inner + conv_ch
            hpg = SSD_HEADS_PER_GROUP
            slabs = []
            for gi in range(SSD_GROUPS):
                slabs += [w[:, dt_col0 + hpg * gi:dt_col0 + hpg * (gi + 1)],
                          w[:, dt_col0 + n_heads + hpg * gi:dt_col0 + n_heads + hpg * (gi + 1)],
                          jnp.zeros((d, 128 - 2 * hpg), w.dtype)]
            w_my = jnp.concatenate([w[:, :dt_col0]] + slabs, axis=1).astype(BF16)

            def slab_vec(v2):
                rows = [jnp.concatenate([v2[0, hpg * gi:hpg * (gi + 1)], v2[1, hpg * gi:hpg * (gi + 1)],
                                         jnp.zeros((128 - 2 * hpg,), F32)]) for gi in range(SSD_GROUPS)]
                return jnp.stack(rows).reshape(SSD_GROUPS, 1, 128)

            y_odd = _proj_odd_call(g, s_rows, mod_l, w_my)
            conv_w8 = jnp.concatenate([ssd_conv_w[o], jnp.zeros((8 - SSD_CONV, conv_ch), F32)], axis=0)
            xbc = _conv_call(g, y_odd, conv_w8, ssd_conv_b[o].reshape(1, conv_ch), d_inner, conv_ch)
            y_f, y_b = _ssd_call(g, xbc, y_odd, slab_vec(ssd_dt_bias[o]), slab_vec(ssd_a_log[o]), d_inner)
            d_skip = jnp.repeat(ssd_d[o], SSD_HEAD_DIM).reshape(1, d_inner)
            s_rows = _mix_odd_call(g, s_rows, mod_l, y_f, y_b, xbc, y_odd, d_skip,
                                   ssd_norm_w[o].reshape(1, d_inner), ssd_w_out[o].astype(BF16),
                                   ln_g[layer, 0].reshape(1, d), ln_b[layer, 0].reshape(1, d), alpha)
        s_rows = _ffn_call(g, s_rows, mod_l, ffn_w_in[layer].astype(BF16), ffn_w_out[layer].astype(BF16),
                           ln_g[layer, 1].reshape(1, d), ln_b[layer, 1].reshape(1, d), alpha,
                           latent_only=not need_ctx)
    return s_rows.reshape(batch, seq, d)
```

```python
import functools
import math

import jax
import jax.numpy as jnp
from jax import lax
from jax.experimental import pallas as pl
from jax.experimental.pallas import tpu as pltpu

F32 = jnp.float32
BF16 = jnp.bfloat16

GRID_W = 64
GLA_HEADS, GLA_DK, GLA_DV, GLA_RANK, GLA_CHUNK = 4, 64, 128, 16, 64
GLA_INV_TAU = 1.0 / 16.0
DIFF_HEADS, DIFF_DQK, DIFF_DV = 4, 64, 128
ROPE_BASE = 10000.0
SSD_HEAD_DIM, SSD_GROUPS, SSD_STATE, SSD_CONV, SSD_CHUNK = 64, 4, 128, 5, 128
SSD_HEADS_PER_GROUP = 8
LN_EPS = 1e-6
RMS_EPS = 1e-6

EV_GLA_W = GLA_HEADS * GLA_DK
EV_GLA_V = GLA_HEADS * GLA_DV
EV_MAIN = 2 * EV_GLA_W + 2 * EV_GLA_V + 128
EV_DIFF = DIFF_HEADS * 2 * DIFF_DQK
EV_REAL_MAIN = 2 * EV_GLA_W + 2 * EV_GLA_V + 2 * GLA_RANK

ROW_TILE = 512
MIX_BLOCK = 256
ATTN_TQ = 1024
ATTN_TK = 1024
ATTN_QG = 256
ATTN_KC = 128
ATTN_LAG_LIMIT = 50.0
ATTN_ONES = 16
LOG2E = 1.4426950408889634
PROJ_CHUNK = 512
VMEM_LIMIT = 56 * 1024 * 1024


def _cparams(sem):
    return pltpu.CompilerParams(dimension_semantics=sem, vmem_limit_bytes=VMEM_LIMIT)


def _resident(shape):
    nd = len(shape)
    return pl.BlockSpec(shape, lambda *_: (0,) * nd, pipeline_mode=pl.Buffered(1))


def _silu(v):
    return v / (1.0 + jnp.exp(-v))


def _layer_norm(v, g, b):
    mu = jnp.mean(v, axis=-1, keepdims=True)
    d = v - mu
    var = jnp.mean(d * d, axis=-1, keepdims=True)
    return d * lax.rsqrt(var + LN_EPS) * g + b


def _split_dot(mat_bf16, v):
    hi = v.astype(BF16)
    lo = (v - hi.astype(F32)).astype(BF16)
    return (jnp.dot(mat_bf16, hi, preferred_element_type=F32)
            + jnp.dot(mat_bf16, lo, preferred_element_type=F32))


def _split_dot_r(v, mat_bf16):
    hi = v.astype(BF16)
    lo = (v - hi.astype(F32)).astype(BF16)
    return (jnp.dot(hi, mat_bf16, preferred_element_type=F32)
            + jnp.dot(lo, mat_bf16, preferred_element_type=F32))


def _mod_kernel(c_ref, w_ref, b_ref, o_ref):
    s = _silu(c_ref[...]).astype(BF16)
    o_ref[...] = jnp.dot(s, w_ref[...].astype(BF16), preferred_element_type=F32) + b_ref[...]


def _mod_call(cc, mod_w, mod_b):
    depth, d, n = mod_w.shape
    cw = d
    return pl.pallas_call(
        _mod_kernel,
        grid=(depth, n // cw),
        in_specs=[pl.BlockSpec((8, d), lambda l, j: (0, 0)),
                  pl.BlockSpec((None, d, cw), lambda l, j: (l, 0, j)),
                  pl.BlockSpec((None, 1, cw), lambda l, j: (l, 0, j))],
        out_specs=pl.BlockSpec((None, 8, cw), lambda l, j: (l, 0, j)),
        out_shape=jax.ShapeDtypeStruct((depth, 8, n), F32),
        compiler_params=_cparams(("parallel", "parallel")),
        name="mod_vectors",
    )(cc, mod_w, mod_b.reshape(depth, 1, n))


class _Geom:
    def __init__(self, batch, seq, ctx_len, d_model):
        self.b, self.t, self.c, self.d = batch, seq, ctx_len, d_model
        self.rows = batch * (seq + ctx_len)
        self.tm = min(ROW_TILE, batch * ctx_len)
        assert (batch * ctx_len) % self.tm == 0 and seq % self.tm == 0
        self.ctx_tiles = batch * ctx_len // self.tm
        self.tiles_per_batch = seq // self.tm
        self.n_tiles = self.rows // self.tm
        self.blk = MIX_BLOCK
        assert ctx_len == self.blk and seq % self.blk == 0
        self.ctx_blocks = batch
        self.nb = seq // self.blk

    def mod_row(self, tile):
        return jnp.where(tile < self.ctx_tiles, self.b, (tile - self.ctx_tiles) // self.tiles_per_batch)

    def fwd_block(self, b, s):
        return jnp.where(s == 0, b, self.ctx_blocks + self.nb * b + s - 1)

    def bwd_block(self, b, s):
        return jnp.where(s == 0, b, self.ctx_blocks + self.nb * b + self.nb - s)


def _proj_even_kernel(x_ref, mod_ref, w_ref, rope_ref, ymain_ref, k2_ref, qt_ref, vt_ref):
    x = x_ref[...]
    h = (x * (1.0 + mod_ref[1:2, :]) + mod_ref[0:1, :]).astype(BF16)
    y = jnp.dot(h, w_ref[...], preferred_element_type=F32)
    ymain_ref[...] = y[:, :EV_MAIN]
    cos = rope_ref[:, 0:128]
    sin_up = rope_ref[:, 128:256]
    sin_dn = rope_ref[:, 256:384]

    def rope(t):
        return t * cos + pltpu.roll(t, 112, 1) * sin_up + pltpu.roll(t, 16, 1) * sin_dn

    for j in range(DIFF_HEADS):
        lo, hi = 128 * j, 128 * (j + 1)
        q = rope(y[:, EV_MAIN + lo:EV_MAIN + hi]) * (DIFF_DQK ** -0.5 * LOG2E)
        qt_ref[lo:hi, :] = q.T.astype(BF16)
        k = rope(y[:, EV_MAIN + EV_DIFF + lo:EV_MAIN + EV_DIFF + hi])
        k2_ref[:, lo:hi] = k.astype(BF16)
        v = y[:, EV_MAIN + 2 * EV_DIFF + lo:EV_MAIN + 2 * EV_DIFF + hi]
        vt_ref[lo:hi, :] = v.T.astype(BF16)


def _proj_even_call(g, s_rows, mod_l, w, rope_tab):
    tm, d = g.tm, g.d
    n = w.shape[1]
    rope_tiles_ctx = 1

    def rope_idx(i):
        return jnp.where(i < g.ctx_tiles, 0, rope_tiles_ctx + (i - g.ctx_tiles) % g.tiles_per_batch)

    return pl.pallas_call(
        _proj_even_kernel,
        grid=(g.n_tiles,),
        in_specs=[pl.BlockSpec((tm, d), lambda i: (i, 0)),
                  pl.BlockSpec((None, 6, d), lambda i: (g.mod_row(i), 0, 0)),
                  _resident((d, n)),
                  pl.BlockSpec((tm, 384), lambda i: (rope_idx(i), 0))],
        out_specs=[pl.BlockSpec((tm, EV_MAIN), lambda i: (i, 0)),
                   pl.BlockSpec((tm, EV_DIFF), lambda i: (i, 0)),
                   pl.BlockSpec((EV_DIFF, tm), lambda i: (0, i)),
                   pl.BlockSpec((EV_DIFF, tm), lambda i: (0, i))],
        out_shape=[jax.ShapeDtypeStruct((g.rows, EV_MAIN), F32),
                   jax.ShapeDtypeStruct((g.rows, EV_DIFF), BF16),
                   jax.ShapeDtypeStruct((EV_DIFF, g.rows), BF16),
                   jax.ShapeDtypeStruct((EV_DIFF, g.rows), BF16)],
        compiler_params=_cparams(("parallel",)),
        name="proj_even",
    )(s_rows, mod_l, w, rope_tab)


def _rope_table(g):
    rows = g.t // GRID_W
    row = jnp.repeat(jnp.arange(rows, dtype=F32), GRID_W)
    col = jnp.tile(jnp.arange(GRID_W, dtype=F32), rows)
    n_freq = DIFF_DQK // 4
    inv = ROPE_BASE ** (-jnp.arange(n_freq, dtype=F32) / n_freq)
    ar, ac = row[:, None] * inv, col[:, None] * inv
    z = jnp.zeros_like(ar)
    cos64 = jnp.concatenate([jnp.cos(ar), jnp.cos(ar), jnp.cos(ac), jnp.cos(ac)], axis=-1)
    up64 = jnp.concatenate([-jnp.sin(ar), z, -jnp.sin(ac), z], axis=-1)
    dn64 = jnp.concatenate([z, jnp.sin(ar), z, jnp.sin(ac)], axis=-1)
    tab = jnp.concatenate([cos64, cos64, up64, up64, dn64, dn64], axis=-1)
    ident = jnp.concatenate([jnp.ones((g.tm, 128), F32), jnp.zeros((g.tm, 256), F32)], axis=-1)
    return jnp.concatenate([ident, tab], axis=0)


def _gla_kernel(qf, kf, vf, rf, qb, kb, vb, rb, wg_ref, bg_ref, of_ref, ob_ref, st_ref):
    s = pl.program_id(1)

    @pl.when(s == 0)
    def _():
        st_ref[...] = jnp.zeros_like(st_ref)

    n = MIX_BLOCK
    nch = n // GLA_CHUNK
    ri = lax.broadcasted_iota(jnp.int32, (n, n), 0)
    ci = lax.broadcasted_iota(jnp.int32, (n, n), 1)
    same_chunk = (ri // GLA_CHUNK) == (ci // GLA_CHUNK)
    col_chunk = ci // GLA_CHUNK
    lane_head = lax.broadcasted_iota(jnp.int32, (1, EV_GLA_W), 1) // GLA_DK
    diag_blocks = ((lax.broadcasted_iota(jnp.int32, (EV_GLA_W, EV_GLA_V), 0) // GLA_DK)
                   == (lax.broadcasted_iota(jnp.int32, (EV_GLA_W, EV_GLA_V), 1) // GLA_DV))

    dirs = ((qf, kf, vf, rf, of_ref), (qb, kb, vb, rb, ob_ref))
    causal, la, qd, ki, kd, v16 = [], [], [], [], [], []
    for d, (q_ref, k_ref, v_ref, r_ref, _) in enumerate(dirs):
        causal.append(jnp.logical_and(same_chunk, (ci >= ri) if d == 1 else (ci <= ri)))
        z = jnp.dot(r_ref[...].astype(BF16), wg_ref[d], preferred_element_type=F32) + bg_ref[d]
        la.append((jnp.minimum(z, 0.0) - jnp.log1p(jnp.exp(-jnp.abs(z)))) * GLA_INV_TAU)
    for d, (q_ref, k_ref, v_ref, r_ref, _) in enumerate(dirs):
        cum = _split_dot(jnp.where(causal[d], 1.0, 0.0).astype(BF16), la[d])
        tot = jnp.concatenate(
            [jnp.broadcast_to(jnp.sum(la[d][c * GLA_CHUNK:(c + 1) * GLA_CHUNK], axis=0, keepdims=True),
                              (GLA_CHUNK, EV_GLA_W)) for c in range(nch)], axis=0)
        k = k_ref[...]
        v16.append(v_ref[...].astype(BF16))
        qd.append(q_ref[...] * jnp.exp(cum) * (GLA_DK ** -0.5))
        ki.append((k * jnp.exp(-cum)).astype(BF16))
        kd.append(k * jnp.exp(tot - cum))

    parts = ([], [])
    for h in range(GLA_HEADS):
        for d in range(2):
            qh = jnp.where(lane_head == h, qd[d], 0.0).astype(BF16)
            att = lax.dot_general(qh, ki[d], (((1,), (1,)), ((), ())), preferred_element_type=F32)
            att = jnp.where(causal[d], att, 0.0).astype(BF16)
            parts[d].append(jnp.dot(att, v16[d][:, h * GLA_DV:(h + 1) * GLA_DV],
                                    preferred_element_type=F32))

    la_t = [la[d].T for d in range(2)]
    kd_t = [kd[d].T for d in range(2)]
    qd16 = [qd[d].astype(BF16) for d in range(2)]
    state = [st_ref[0], st_ref[1]]
    o_inter = ([None] * nch, [None] * nch)
    for i in range(nch):
        for d in range(2):
            c = nch - 1 - i if d == 1 else i
            in_chunk = col_chunk == c
            o_inter[d][c] = jnp.dot(qd16[d][c * GLA_CHUNK:(c + 1) * GLA_CHUNK], state[d].astype(BF16),
                                    preferred_element_type=F32)
            kv = jnp.dot(jnp.where(in_chunk, kd_t[d], 0.0).astype(BF16), v16[d],
                         preferred_element_type=F32)
            decay = jnp.exp(jnp.sum(jnp.where(in_chunk, la_t[d], 0.0), axis=1, keepdims=True))
            state[d] = decay * state[d] + jnp.where(diag_blocks, kv, 0.0)
    for d in range(2):
        st_ref[d] = state[d]
        dirs[d][4][...] = jnp.concatenate(parts[d], axis=1) + jnp.concatenate(o_inter[d], axis=0)


def _gla_call(g, ymain, wg, bg):
    blk = g.blk
    qw, vw = EV_GLA_W, EV_GLA_V

    def specs(block_fn):
        return [pl.BlockSpec((blk, qw), lambda b, s: (block_fn(b, s), 0)),
                pl.BlockSpec((blk, qw), lambda b, s: (block_fn(b, s), 1)),
                pl.BlockSpec((blk, vw), lambda b, s: (block_fn(b, s), 1)),
                pl.BlockSpec((blk, 128), lambda b, s: (block_fn(b, s), (EV_MAIN - 128) // 128))]

    return pl.pallas_call(
        _gla_kernel,
        grid=(g.b, g.nb + 1),
        in_specs=specs(g.fwd_block) + specs(g.bwd_block) + [
            pl.BlockSpec((2, 128, qw), lambda b, s: (0, 0, 0)),
            pl.BlockSpec((2, 1, qw), lambda b, s: (0, 0, 0))],
        out_specs=[pl.BlockSpec((blk, vw), lambda b, s: (g.fwd_block(b, s), 0)),
                   pl.BlockSpec((blk, vw), lambda b, s: (g.bwd_block(b, s), 0))],
        out_shape=[jax.ShapeDtypeStruct((g.rows, vw), F32)] * 2,
        scratch_shapes=[pltpu.VMEM((2, qw, vw), F32)],
        compiler_params=_cparams(("parallel", "arbitrary")),
        name="gla_scan",
    )(ymain, ymain, ymain, ymain, ymain, ymain, ymain, ymain, wg, bg)


def _part8(x, op):
    rows = [x[i * 8:(i + 1) * 8] for i in range(x.shape[0] // 8)]
    while len(rows) > 1:
        rows = [op(rows[i], rows[i + 1]) for i in range(0, len(rows), 2)]
    return rows[0]


def _attn_kernel(lam_ref, nw_ref, qt_ref, kc_ref, vct_ref, *rest, lam_init, with_latent):
    if with_latent:
        kx_ref, vxt_ref, o_ref, m_sc, acc_sc, p_sc = rest
        ki = pl.program_id(3)
        last = pl.num_programs(3) - 1
    else:
        _, o_ref, m_sc, acc_sc = rest
    qt = qt_ref[...]
    tq = qt.shape[1]
    row = lax.broadcasted_iota(jnp.int32, qt.shape, 0)
    zero = jnp.zeros_like(qt)
    qt_maps = (jnp.where(row < DIFF_DQK, qt, zero), jnp.where(row >= DIFF_DQK, qt, zero))

    def with_ones(vt):
        return jnp.concatenate([vt, jnp.ones((ATTN_ONES, vt.shape[1]), vt.dtype)], axis=0)

    def exact_update(k, vt1):
        for m in range(2):
            st = jnp.dot(k, qt_maps[m], preferred_element_type=F32)
            m_old = m_sc[m]
            m_new = jnp.maximum(m_old, jnp.max(st, axis=0, keepdims=True))
            alpha = jnp.exp2(m_old - m_new)
            p = jnp.exp2(st - m_new)
            acc_sc[m] = alpha * acc_sc[m] + jnp.dot(vt1, p.astype(BF16), preferred_element_type=F32)
            m_sc[m] = m_new

    def lagged_update(k, vt1):
        tk = k.shape[0]
        refs, bmax = [], []
        for m in range(2):
            r = m_sc[m]
            mx_g = []
            for gq in range(tq // ATTN_QG):
                cols = slice(gq * ATTN_QG, (gq + 1) * ATTN_QG)
                st = jnp.dot(k, qt_maps[m][:, cols], preferred_element_type=F32)
                rg = r[:, cols]
                mx = None
                for c in range(tk // ATTN_KC):
                    tile = st[c * ATTN_KC:(c + 1) * ATTN_KC]
                    p_sc[m, c * ATTN_KC:(c + 1) * ATTN_KC, cols] = jnp.exp2(tile - rg).astype(BF16)
                    tmx = _part8(tile, jnp.maximum)
                    mx = tmx if mx is None else jnp.maximum(mx, tmx)
                mx_g.append(mx)
            refs.append(r)
            bmax.append(jnp.max(jnp.concatenate(mx_g, axis=1), axis=0, keepdims=True))
        lag = jnp.maximum(jnp.max(bmax[0] - refs[0]), jnp.max(bmax[1] - refs[1]))
        safe = lag <= ATTN_LAG_LIMIT

        @pl.when(safe)
        def _():
            for m in range(2):
                r_new = jnp.maximum(refs[m], bmax[m])
                pv = jnp.dot(vt1, p_sc[m], preferred_element_type=F32)
                acc_sc[m] = (acc_sc[m] + pv) * jnp.exp2(refs[m] - r_new)
                m_sc[m] = r_new

        @pl.when(jnp.logical_not(safe))
        def _():
            exact_update(k, vt1)

    def init_and_ctx():
        m_sc[...] = jnp.full(m_sc.shape, -jnp.inf, F32)
        acc_sc[...] = jnp.zeros(acc_sc.shape, F32)
        exact_update(kc_ref[...], with_ones(vct_ref[...]))

    def finish():
        lp = lam_ref[...]
        lam = (jnp.exp(jnp.sum(lp[0:1] * lp[1:2], keepdims=True))
               - jnp.exp(jnp.sum(lp[2:3] * lp[3:4], keepdims=True)) + lam_init)
        a0, a1 = acc_sc[0], acc_sc[1]
        ot = (a0[:DIFF_DV] / a0[DIFF_DV:DIFF_DV + 1]
              - lam * (a1[:DIFF_DV] / a1[DIFF_DV:DIFF_DV + 1]))
        ms = jnp.mean(ot * ot, axis=0, keepdims=True)
        on = ot * lax.rsqrt(ms + RMS_EPS) * nw_ref[...] * (1.0 - lam_init)
        o_ref[...] = on.T.astype(o_ref.dtype)

    if with_latent:
        pl.when(ki == 0)(init_and_ctx)
        lagged_update(kx_ref[...], with_ones(vxt_ref[...]))
        pl.when(ki == last)(finish)
    else:
        init_and_ctx()
        finish()


def _attn_call(g, k2, qt, vt, lam_p, norm_w_col, lam_init, od_prev=None):
    hd = 2 * DIFF_DQK
    tq = min(ATTN_TQ, g.t)
    tk = min(ATTN_TK, g.t)
    assert (g.b * g.c) % tq == 0 and g.t % tq == 0 and g.t % tk == 0 and (g.b * g.c) % tk == 0
    q_off, k_off = g.b * g.c // tq, g.b * g.c // tk
    nq, nk = g.t // tq, g.t // tk
    scratch = lambda nqry: [pltpu.VMEM((2, 1, nqry), F32),
                            pltpu.VMEM((2, DIFF_DV + ATTN_ONES, nqry), F32)]
    common = [pl.BlockSpec((4, DIFF_DQK), lambda *_: (0, 0)),
              pl.BlockSpec((DIFF_DV, 1), lambda *_: (0, 0))]
    od = pl.pallas_call(
        functools.partial(_attn_kernel, lam_init=lam_init, with_latent=True),
        grid=(g.b, DIFF_HEADS, nq, nk),
        in_specs=common + [
            pl.BlockSpec((hd, tq), lambda b, h, qi, ki: (h, q_off + b * nq + qi)),
            pl.BlockSpec((g.c, hd), lambda b, h, qi, ki: (b, h)),
            pl.BlockSpec((DIFF_DV, g.c), lambda b, h, qi, ki: (h, b)),
            pl.BlockSpec((tk, hd), lambda b, h, qi, ki: (k_off + b * nk + ki, h)),
            pl.BlockSpec((DIFF_DV, tk), lambda b, h, qi, ki: (h, k_off + b * nk + ki))],
        out_specs=pl.BlockSpec((tq, DIFF_DV), lambda b, h, qi, ki: (q_off + b * nq + qi, h)),
        out_shape=jax.ShapeDtypeStruct((g.rows, DIFF_HEADS * DIFF_DV), BF16),
        scratch_shapes=scratch(tq) + [pltpu.VMEM((2, tk, tq), BF16)],
        compiler_params=_cparams(("parallel", "parallel", "parallel", "arbitrary")),
        name="diff_attn_latent",
    )(lam_p, norm_w_col, qt, k2, vt, k2, vt)
    return od


def _attn_ctx_call(g, k2, qt, vt, lam_p, norm_w_col, lam_init, od):
    hd = 2 * DIFF_DQK
    return pl.pallas_call(
        functools.partial(_attn_kernel, lam_init=lam_init, with_latent=False),
        grid=(g.b, DIFF_HEADS),
        in_specs=[pl.BlockSpec((4, DIFF_DQK), lambda *_: (0, 0)),
                  pl.BlockSpec((DIFF_DV, 1), lambda *_: (0, 0)),
                  pl.BlockSpec((hd, g.c), lambda b, h: (h, b)),
                  pl.BlockSpec((g.c, hd), lambda b, h: (b, h)),
                  pl.BlockSpec((DIFF_DV, g.c), lambda b, h: (h, b)),
                  pl.BlockSpec(memory_space=pl.ANY)],
        out_specs=pl.BlockSpec((g.c, DIFF_DV), lambda b, h: (b, h)),
        out_shape=jax.ShapeDtypeStruct(od.shape, od.dtype),
        scratch_shapes=[pltpu.VMEM((2, 1, g.c), F32),
                        pltpu.VMEM((2, DIFF_DV + ATTN_ONES, g.c), F32)],
        input_output_aliases={5: 0},
        compiler_params=_cparams(("parallel", "parallel")),
        name="diff_attn_context",
    )(lam_p, norm_w_col, qt, k2, vt, od)


def _proj_odd_kernel(x_ref, xp_ref, xn_ref, mod_ref, w_ref, cw_ref, cb_ref, z_ref, xbc_ref, dt_ref,
                     seq_ref, *, ctx_blocks, nb, d_inner, conv_ch):
    i = pl.program_id(0)
    j = (i - ctx_blocks) % nb
    is_lat = i >= ctx_blocks
    has_prev = jnp.logical_and(is_lat, j > 0).astype(F32)
    has_next = jnp.logical_and(is_lat, j < nb - 1).astype(F32)
    scale = 1.0 + mod_ref[1:2, :]
    shift = mod_ref[0:1, :]
    rows = jnp.concatenate([x_ref[...], xp_ref[...], xn_ref[...]], axis=0)
    h = (rows * scale + shift).astype(BF16)
    n = x_ref.shape[0]
    cw = PROJ_CHUNK
    other = [(z_ref, c0, c0) for c0 in range(0, d_inner, cw)]
    other += [(dt_ref, c0, d_inner + conv_ch + c0) for c0 in range(0, dt_ref.shape[1], cw)]
    pad = SSD_CONV // 2
    for jc, c0 in enumerate(range(0, conv_ch, cw)):
        cols = slice(c0, c0 + cw)
        y = jnp.dot(h, w_ref[:, d_inner + c0:d_inner + c0 + cw], preferred_element_type=F32)
        seq_ref[0:8, cols] = y[n:n + 8] * has_prev
        seq_ref[8:8 + n, cols] = y[:n]
        seq_ref[8 + n:16 + n, cols] = y[n + 8:n + 16] * has_next
        if jc < len(other):
            o_ref, oc, wc = other[jc]
            o_ref[:, oc:oc + cw] = jnp.dot(h[:n], w_ref[:, wc:wc + cw], preferred_element_type=F32)
        w = cw_ref[:, cols]
        acc = cb_ref[:, cols] + seq_ref[8:8 + n, cols] * w[pad:pad + 1]
        for tap in range(SSD_CONV):
            if tap != pad:
                acc = acc + seq_ref[pl.ds(8 - pad + tap, n), cols] * w[tap:tap + 1]
        xbc_ref[:, cols] = _silu(acc)
    assert len(other) <= conv_ch // cw


def _proj_odd_call(g, s_rows, mod_l, w, conv_w8, conv_b, d_inner, conv_ch):
    blk = g.blk
    assert g.tm % blk == 0
    per_tile = g.tm // blk
    n = w.shape[1]
    dt_w = n - d_inner - conv_ch
    nrow8 = g.rows // 8
    per = blk // 8
    row = lambda i: (i, 0)
    return pl.pallas_call(
        functools.partial(_proj_odd_kernel, ctx_blocks=g.ctx_blocks, nb=g.nb, d_inner=d_inner,
                          conv_ch=conv_ch),
        grid=(g.rows // blk,),
        in_specs=[pl.BlockSpec((blk, g.d), row),
                  pl.BlockSpec((8, g.d), lambda i: (jnp.maximum(i * per - 1, 0), 0)),
                  pl.BlockSpec((8, g.d), lambda i: (jnp.minimum((i + 1) * per, nrow8 - 1), 0)),
                  pl.BlockSpec((None, 6, g.d), lambda i: (g.mod_row(i // per_tile), 0, 0)),
                  _resident((g.d, n)),
                  pl.BlockSpec((8, conv_ch), lambda i: (0, 0)),
                  pl.BlockSpec((1, conv_ch), lambda i: (0, 0))],
        out_specs=[pl.BlockSpec((blk, d_inner), row),
                   pl.BlockSpec((blk, conv_ch), row),
                   pl.BlockSpec((blk, dt_w), row)],
        out_shape=[jax.ShapeDtypeStruct((g.rows, d_inner), F32),
                   jax.ShapeDtypeStruct((g.rows, conv_ch), F32),
                   jax.ShapeDtypeStruct((g.rows, dt_w), F32)],
        scratch_shapes=[pltpu.VMEM((blk + 16, conv_ch), F32)],
        compiler_params=_cparams(("parallel",)),
        name="proj_odd_conv",
    )(s_rows, s_rows, s_rows, mod_l, w, conv_w8, conv_b)


def _ssd_kernel(xf, bmf, cmf, dtf, xb, bmb, cmb, dtb, bias_ref, alog_ref, yf_ref, yb_ref, st_ref):
    s = pl.program_id(2)

    @pl.when(s == 0)
    def _():
        st_ref[...] = jnp.zeros_like(st_ref)

    n, ch = MIX_BLOCK, SSD_CHUNK
    nch = n // ch
    gw = SSD_HEADS_PER_GROUP * SSD_HEAD_DIM
    ri = lax.broadcasted_iota(jnp.int32, (n, n), 0)
    ci = lax.broadcasted_iota(jnp.int32, (n, n), 1)
    same_chunk = (ri // ch) == (ci // ch)
    ri_c = lax.broadcasted_iota(jnp.int32, (ch, ch), 0)
    ci_c = lax.broadcasted_iota(jnp.int32, (ch, ch), 1)
    lane_c = lax.broadcasted_iota(jnp.int32, (ch, 128), 1)
    a_neg = -jnp.exp(alog_ref[...])
    exp_row = lax.broadcasted_iota(jnp.int32, (128, gw), 0)
    exp_col = lax.broadcasted_iota(jnp.int32, (128, gw), 1) // SSD_HEAD_DIM

    dirs = ((xf, bmf, cmf, dtf, yf_ref), (xb, bmb, cmb, dtb, yb_ref))
    dt, acs, acs_t, dt_t, expand, tri = [], [], [], [], [], []
    for d in range(2):
        lane0 = SSD_HEADS_PER_GROUP * d
        expand.append(jnp.where(exp_row == exp_col + lane0, 1.0, 0.0).astype(BF16))
        z = dirs[d][3][...] + bias_ref[...]
        dt.append(jnp.maximum(z, 0.0) + jnp.log1p(jnp.exp(-jnp.abs(z))))
        causal_blk = jnp.logical_and(same_chunk, (ci >= ri) if d == 1 else (ci <= ri))
        acs.append(_split_dot(jnp.where(causal_blk, 1.0, 0.0).astype(BF16), dt[d] * a_neg))
        tri.append((ci_c >= ri_c) if d == 1 else (ci_c <= ri_c))
    for d in range(2):
        acs_t.append(acs[d].T)
        dt_t.append(dt[d].T)
    state = [st_ref[0], st_ref[1]]

    for i in range(nch):
        for d in range(2):
            x_ref, bm_ref, cm_ref, _, y_ref = dirs[d]
            rev = d == 1
            lane0 = SSD_HEADS_PER_GROUP * d
            c = nch - 1 - i if rev else i
            r0, r1 = c * ch, (c + 1) * ch
            xc = x_ref[r0:r1, :]
            bmc = bm_ref[r0:r1, :]
            acs_c = acs[d][r0:r1, :]
            acs_tc = acs_t[d][:, r0:r1]
            dt_tc = dt_t[d][:, r0:r1]
            cm16 = cm_ref[r0:r1, :].astype(BF16)
            cb = lax.dot_general(cm16, bmc.astype(BF16), (((1,), (1,)), ((), ())),
                                 preferred_element_type=F32)
            a_last = acs_c[0:1, :] if rev else acs_c[ch - 1:ch, :]
            fac = jnp.concatenate([jnp.exp(acs_c), dt[d][r0:r1, :] * jnp.exp(a_last - acs_c)], axis=0)
            fac = jnp.dot(fac.astype(BF16), expand[d], preferred_element_type=F32)
            dec = _split_dot_r(jnp.broadcast_to(jnp.exp(a_last), (8, 128)), expand[d])[0:1]
            pairs = []
            for pp in range(SSD_HEADS_PER_GROUP // 2):
                mats = []
                for e in (2 * pp, 2 * pp + 1):
                    ln = lane0 + e
                    seg = jnp.exp(jnp.where(tri[d], acs_c[:, ln:ln + 1] - acs_tc[ln:ln + 1, :], -jnp.inf))
                    mats.append((cb * seg * dt_tc[ln:ln + 1, :]).astype(BF16))
                xp = xc[:, 128 * pp:128 * (pp + 1)]
                x2 = jnp.concatenate([jnp.where(lane_c < SSD_HEAD_DIM, xp, 0.0),
                                      jnp.where(lane_c >= SSD_HEAD_DIM, xp, 0.0)], axis=0).astype(BF16)
                pairs.append(jnp.dot(jnp.concatenate(mats, axis=1), x2, preferred_element_type=F32))
            y_off = jnp.dot(cm16, state[d].astype(BF16), preferred_element_type=F32) * fac[0:ch]
            y_ref[r0:r1, :] = jnp.concatenate(pairs, axis=1) + y_off
            upd = jnp.dot(bmc.T.astype(BF16), (xc * fac[ch:2 * ch]).astype(BF16),
                          preferred_element_type=F32)
            state[d] = dec * state[d] + upd
    st_ref[0] = state[0]
    st_ref[1] = state[1]


def _ssd_call(g, xbc, y_odd, dt_bias_slab, a_log_slab, d_inner):
    blk = g.blk
    gw = SSD_HEADS_PER_GROUP * SSD_HEAD_DIM
    bm0 = d_inner // 128
    cm0 = bm0 + SSD_GROUPS
    dt0 = 0

    def specs(block_fn):
        return [pl.BlockSpec((blk, gw), lambda b, gi, s: (block_fn(b, s), gi)),
                pl.BlockSpec((blk, 128), lambda b, gi, s: (block_fn(b, s), bm0 + gi)),
                pl.BlockSpec((blk, 128), lambda b, gi, s: (block_fn(b, s), cm0 + gi)),
                pl.BlockSpec((blk, 128), lambda b, gi, s: (block_fn(b, s), dt0 + gi))]

    return pl.pallas_call(
        _ssd_kernel,
        grid=(g.b, SSD_GROUPS, g.nb + 1),
        in_specs=specs(g.fwd_block) + specs(g.bwd_block) + [
            pl.BlockSpec((None, 1, 128), lambda b, gi, s: (gi, 0, 0)),
            pl.BlockSpec((None, 1, 128), lambda b, gi, s: (gi, 0, 0))],
        out_specs=[pl.BlockSpec((blk, gw), lambda b, gi, s: (g.fwd_block(b, s), gi)),
                   pl.BlockSpec((blk, gw), lambda b, gi, s: (g.bwd_block(b, s), gi))],
        out_shape=[jax.ShapeDtypeStruct((g.rows, d_inner), F32)] * 2,
        scratch_shapes=[pltpu.VMEM((2, SSD_STATE, gw), F32)],
        compiler_params=_cparams(("parallel", "parallel", "arbitrary")),
        name="ssd_scan",
    )(xbc, xbc, xbc, y_odd, xbc, xbc, xbc, y_odd, dt_bias_slab, a_log_slab)


def _mix_even_kernel(x_ref, mod_ref, of_ref, ob_ref, g_ref, od_ref, nw_ref, w_ref, lng_ref, lnb_ref,
                     o_ref, *, alpha):
    o = of_ref[...] + ob_ref[...]
    gate = g_ref[...]
    nw = nw_ref[...]
    parts = []
    for h in range(GLA_HEADS):
        oh = o[:, h * GLA_DV:(h + 1) * GLA_DV]
        ms = jnp.mean(oh * oh, axis=-1, keepdims=True)
        parts.append(oh * lax.rsqrt(ms + RMS_EPS) * nw)
    gla = (jnp.concatenate(parts, axis=1) * _silu(gate)).astype(BF16)
    mixin = jnp.concatenate([gla, od_ref[...]], axis=1)
    mix = jnp.dot(mixin, w_ref[...], preferred_element_type=F32)
    v = alpha * x_ref[...] + mod_ref[2:3, :] * mix
    o_ref[...] = _layer_norm(v, lng_ref[...], lnb_ref[...])


def _mix_even_call(g, s_rows, mod_l, o_f, o_b, ymain, od, gla_nw, w_out, ln_g, ln_b, alpha):
    tm, d = g.tm, g.d
    vw = EV_GLA_V
    row = lambda i: (i, 0)
    return pl.pallas_call(
        functools.partial(_mix_even_kernel, alpha=alpha),
        grid=(g.n_tiles,),
        in_specs=[pl.BlockSpec((tm, d), row),
                  pl.BlockSpec((None, 6, d), lambda i: (g.mod_row(i), 0, 0)),
                  pl.BlockSpec((tm, vw), row),
                  pl.BlockSpec((tm, vw), row),
                  pl.BlockSpec((tm, vw), lambda i: (i, 2 * EV_GLA_W // vw + 1)),
                  pl.BlockSpec((tm, vw), row),
                  pl.BlockSpec((1, GLA_DV), lambda i: (0, 0)),
                  _resident(w_out.shape),
                  pl.BlockSpec((1, d), lambda i: (0, 0)),
                  pl.BlockSpec((1, d), lambda i: (0, 0))],
        out_specs=pl.BlockSpec((tm, d), row),
        out_shape=jax.ShapeDtypeStruct((g.rows, d), F32),
        compiler_params=_cparams(("parallel",)),
        name="mix_even",
    )(s_rows, mod_l, o_f, o_b, ymain, od, gla_nw, w_out, ln_g, ln_b)


def _mix_odd_kernel(x_ref, mod_ref, yf_ref, yb_ref, xs_ref, z_ref, dsk_ref, nw_ref, w_ref,
                    lng_ref, lnb_ref, o_ref, *, alpha):
    y = (yf_ref[...] + yb_ref[...] + dsk_ref[...] * xs_ref[...]) * _silu(z_ref[...])
    gw = y.shape[1] // SSD_GROUPS
    parts = []
    for gi in range(SSD_GROUPS):
        yg = y[:, gi * gw:(gi + 1) * gw]
        ms = jnp.mean(yg * yg, axis=-1, keepdims=True)
        parts.append(yg * lax.rsqrt(ms + RMS_EPS))
    yn = (jnp.concatenate(parts, axis=1) * nw_ref[...]).astype(BF16)
    mix = jnp.dot(yn, w_ref[...], preferred_element_type=F32)
    v = alpha * x_ref[...] + mod_ref[2:3, :] * mix
    o_ref[...] = _layer_norm(v, lng_ref[...], lnb_ref[...])


def _mix_odd_call(g, s_rows, mod_l, y_f, y_b, xbc, y_odd, d_skip, norm_w, w_out, ln_g, ln_b, alpha):
    tm = g.tm // 2
    d = g.d
    di = w_out.shape[0]
    row = lambda i: (i, 0)
    vec = lambda n: pl.BlockSpec((1, n), lambda i: (0, 0))
    return pl.pallas_call(
        functools.partial(_mix_odd_kernel, alpha=alpha),
        grid=(g.rows // tm,),
        in_specs=[pl.BlockSpec((tm, d), row),
                  pl.BlockSpec((None, 6, d), lambda i: (g.mod_row(i // 2), 0, 0)),
                  pl.BlockSpec((tm, di), row),
                  pl.BlockSpec((tm, di), row),
                  pl.BlockSpec((tm, di), row),
                  pl.BlockSpec((tm, di), row),
                  vec(di), vec(di),
                  _resident(w_out.shape),
                  vec(d), vec(d)],
        out_specs=pl.BlockSpec((tm, d), row),
        out_shape=jax.ShapeDtypeStruct((g.rows, d), F32),
        compiler_params=_cparams(("parallel",)),
        name="mix_odd",
    )(s_rows, mod_l, y_f, y_b, xbc, y_odd, d_skip, norm_w, w_out, ln_g, ln_b)


def _ffn_kernel(x_ref, mod_ref, win_ref, wout_ref, lng_ref, lnb_ref, o_ref, *, alpha, hidden, n_chunks):
    x = x_ref[...]
    h = (x * (1.0 + mod_ref[4:5, :]) + mod_ref[3:4, :]).astype(BF16)
    hc = hidden // n_chunks
    acc = jnp.zeros(x.shape, F32)
    for j in range(n_chunks):
        gate = jnp.dot(h, win_ref[:, j * hc:(j + 1) * hc], preferred_element_type=F32)
        up = jnp.dot(h, win_ref[:, hidden + j * hc:hidden + (j + 1) * hc], preferred_element_type=F32)
        act = (_silu(gate) * up).astype(BF16)
        acc = acc + jnp.dot(act, wout_ref[j * hc:(j + 1) * hc, :], preferred_element_type=F32)
    v = alpha * x + mod_ref[5:6, :] * acc
    o_ref[...] = _layer_norm(v, lng_ref[...], lnb_ref[...])


def _ffn_call(g, s_rows, mod_l, w_in, w_out, ln_g, ln_b, alpha, latent_only):
    tm, d = g.tm, g.d
    hidden = w_out.shape[0]
    n_chunks = 2 if hidden % 256 == 0 else 1
    off = g.ctx_tiles if latent_only else 0
    n_tiles = g.n_tiles - off
    return pl.pallas_call(
        functools.partial(_ffn_kernel, alpha=alpha, hidden=hidden, n_chunks=n_chunks),
        grid=(n_tiles,),
        in_specs=[pl.BlockSpec((tm, d), lambda i: (i + off, 0)),
                  pl.BlockSpec((None, 6, d), lambda i: (g.mod_row(i + off), 0, 0)),
                  _resident(w_in.shape),
                  _resident(w_out.shape),
                  pl.BlockSpec((1, d), lambda i: (0, 0)),
                  pl.BlockSpec((1, d), lambda i: (0, 0))],
        out_specs=pl.BlockSpec((tm, d), lambda i: (i, 0)),
        out_shape=jax.ShapeDtypeStruct((n_tiles * tm, d), F32),
        compiler_params=_cparams(("parallel",)),
        name="ffn",
    )(s_rows, mod_l, w_in, w_out, ln_g, ln_b)


def kernel(x, c, ctx, c_ctx, mod_w, mod_b, ln_g, ln_b, ffn_w_in, ffn_w_out, ev_w_in, ev_w_out,
           gla_w_gate2, gla_b_gate, gla_norm_w, diff_lambda, diff_norm_w, ssd_w_in, ssd_conv_w,
           ssd_conv_b, ssd_dt_bias, ssd_a_log, ssd_d, ssd_norm_w, ssd_w_out):
    batch, seq, d = x.shape
    ctx_len = ctx.shape[1]
    depth = mod_w.shape[0]
    g = _Geom(batch, seq, ctx_len, d)
    alpha = (2 * depth) ** 0.25
    d_inner = ssd_w_out.shape[1]
    conv_ch = ssd_conv_w.shape[2]
    n_heads = ssd_d.shape[1]

    s_rows = jnp.concatenate([ctx.reshape(batch * ctx_len, d), x.reshape(batch * seq, d)], axis=0)
    cc = jnp.zeros((8, d), F32).at[:batch].set(c).at[batch].set(c_ctx)
    mod_all = _mod_call(cc, mod_w, mod_b).reshape(depth, 8, 6, d)
    rope_tab = _rope_table(g)

    for layer in range(depth):
        need_ctx = layer < depth - 1
        mod_l = mod_all[layer]
        if layer % 2 == 0:
            e = layer // 2
            w = ev_w_in[e]
            w_my = jnp.concatenate([w[:, :EV_REAL_MAIN], jnp.zeros((d, 128 - 2 * GLA_RANK), w.dtype),
                                    w[:, EV_REAL_MAIN:]], axis=1).astype(BF16)
            ymain, k2, qt, vt = _proj_even_call(g, s_rows, mod_l, w_my, rope_tab)
            wg = jnp.zeros((2, 128, EV_GLA_W), F32)
            wg = wg.at[0, :GLA_RANK].set(gla_w_gate2[e, 0]).at[1, GLA_RANK:2 * GLA_RANK].set(gla_w_gate2[e, 1])
            o_f, o_b = _gla_call(g, ymain, wg.astype(BF16), gla_b_gate[e].reshape(2, 1, EV_GLA_W))
            lam_init = 0.8 - 0.6 * math.exp(-0.3 * layer)
            nw_col = diff_norm_w[e].reshape(DIFF_DV, 1)
            od = _attn_call(g, k2, qt, vt, diff_lambda[e], nw_col, lam_init)
            if need_ctx:
                od = _attn_ctx_call(g, k2, qt, vt, diff_lambda[e], nw_col, lam_init, od)
            s_rows = _mix_even_call(g, s_rows, mod_l, o_f, o_b, ymain, od, gla_norm_w[e].reshape(1, GLA_DV),
                                    ev_w_out[e].astype(BF16), ln_g[layer, 0].reshape(1, d),
                                    ln_b[layer, 0].reshape(1, d), alpha)
        else:
            o = layer // 2
            w = ssd_w_in[o]
            dt_col0 = d_inner + conv_ch
            hpg = SSD_HEADS_PER_GROUP
            slabs = []
            for gi in range(SSD_GROUPS):
                slabs += [w[:, dt_col0 + hpg * gi:dt_col0 + hpg * (gi + 1)],
                          w[:, dt_col0 + n_heads + hpg * gi:dt_col0 + n_heads + hpg * (gi + 1)],
                          jnp.zeros((d, 128 - 2 * hpg), w.dtype)]
            w_my = jnp.concatenate([w[:, :dt_col0]] + slabs, axis=1).astype(BF16)

            def slab_vec(v2):
                rows = [jnp.concatenate([v2[0, hpg * gi:hpg * (gi + 1)], v2[1, hpg * gi:hpg * (gi + 1)],
                                         jnp.zeros((128 - 2 * hpg,), F32)]) for gi in range(SSD_GROUPS)]
                return jnp.stack(rows).reshape(SSD_GROUPS, 1, 128)

            conv_w8 = jnp.concatenate([ssd_conv_w[o], jnp.zeros((8 - SSD_CONV, conv_ch), F32)], axis=0)
            z_gate, xbc, dt_raw = _proj_odd_call(g, s_rows, mod_l, w_my, conv_w8,
                                                 ssd_conv_b[o].reshape(1, conv_ch), d_inner, conv_ch)
            y_f, y_b = _ssd_call(g, xbc, dt_raw, slab_vec(ssd_dt_bias[o]), slab_vec(ssd_a_log[o]), d_inner)
            d_skip = jnp.repeat(ssd_d[o], SSD_HEAD_DIM).reshape(1, d_inner)
            s_rows = _mix_odd_call(g, s_rows, mod_l, y_f, y_b, xbc, z_gate, d_skip,
                                   ssd_norm_w[o].reshape(1, d_inner), ssd_w_out[o].astype(BF16),
                                   ln_g[layer, 0].reshape(1, d), ln_b[layer, 0].reshape(1, d), alpha)
        s_rows = _ffn_call(g, s_rows, mod_l, ffn_w_in[layer].astype(BF16), ffn_w_out[layer].astype(BF16),
                           ln_g[layer, 1].reshape(1, d), ln_b[layer, 1].reshape(1, d), alpha,
                           latent_only=not need_ctx)
    return s_rows.reshape(batch, seq, d)
```

```python
import functools
import math

import jax
import jax.numpy as jnp
from jax import lax
from jax.experimental import pallas as pl
from jax.experimental.pallas import tpu as pltpu

F32 = jnp.float32
BF16 = jnp.bfloat16

GRID_W = 64
GLA_HEADS, GLA_DK, GLA_DV, GLA_RANK, GLA_CHUNK = 4, 64, 128, 16, 64
GLA_INV_TAU = 1.0 / 16.0
DIFF_HEADS, DIFF_DQK, DIFF_DV = 4, 64, 128
ROPE_BASE = 10000.0
SSD_HEAD_DIM, SSD_GROUPS, SSD_STATE, SSD_CONV, SSD_CHUNK = 64, 4, 128, 5, 128
SSD_HEADS_PER_GROUP = 8
LN_EPS = 1e-6
RMS_EPS = 1e-6

EV_GLA_W = GLA_HEADS * GLA_DK
EV_GLA_V = GLA_HEADS * GLA_DV
EV_MAIN = 2 * EV_GLA_W + 2 * EV_GLA_V + 128
EV_DIFF = DIFF_HEADS * 2 * DIFF_DQK
EV_REAL_MAIN = 2 * EV_GLA_W + 2 * EV_GLA_V + 2 * GLA_RANK

ROW_TILE = 512
MIX_BLOCK = 256
ATTN_TQ = 1024
ATTN_TK = 1024
ATTN_QG = 256
ATTN_KC = 128
ATTN_LAG_LIMIT = 50.0
ATTN_ONES = 16
LOG2E = 1.4426950408889634
PROJ_CHUNK = 512
VMEM_LIMIT = 56 * 1024 * 1024


def _cparams(sem):
    return pltpu.CompilerParams(dimension_semantics=sem, vmem_limit_bytes=VMEM_LIMIT)


def _resident(shape):
    nd = len(shape)
    return pl.BlockSpec(shape, lambda *_: (0,) * nd, pipeline_mode=pl.Buffered(1))


def _silu(v):
    return v / (1.0 + jnp.exp(-v))


def _layer_norm(v, g, b):
    mu = jnp.mean(v, axis=-1, keepdims=True)
    d = v - mu
    var = jnp.mean(d * d, axis=-1, keepdims=True)
    return d * lax.rsqrt(var + LN_EPS) * g + b


def _split_dot(mat_bf16, v):
    hi = v.astype(BF16)
    lo = (v - hi.astype(F32)).astype(BF16)
    return (jnp.dot(mat_bf16, hi, preferred_element_type=F32)
            + jnp.dot(mat_bf16, lo, preferred_element_type=F32))


def _split_dot_r(v, mat_bf16):
    hi = v.astype(BF16)
    lo = (v - hi.astype(F32)).astype(BF16)
    return (jnp.dot(hi, mat_bf16, preferred_element_type=F32)
            + jnp.dot(lo, mat_bf16, preferred_element_type=F32))


def _mod_kernel(c_ref, w_ref, b_ref, o_ref):
    s = _silu(c_ref[...]).astype(BF16)
    o_ref[...] = jnp.dot(s, w_ref[...].astype(BF16), preferred_element_type=F32) + b_ref[...]


def _mod_call(cc, mod_w, mod_b):
    depth, d, n = mod_w.shape
    cw = d
    return pl.pallas_call(
        _mod_kernel,
        grid=(depth, n // cw),
        in_specs=[pl.BlockSpec((8, d), lambda l, j: (0, 0)),
                  pl.BlockSpec((None, d, cw), lambda l, j: (l, 0, j)),
                  pl.BlockSpec((None, 1, cw), lambda l, j: (l, 0, j))],
        out_specs=pl.BlockSpec((None, 8, cw), lambda l, j: (l, 0, j)),
        out_shape=jax.ShapeDtypeStruct((depth, 8, n), F32),
        compiler_params=_cparams(("parallel", "parallel")),
        name="mod_vectors",
    )(cc, mod_w, mod_b.reshape(depth, 1, n))


class _Geom:
    def __init__(self, batch, seq, ctx_len, d_model):
        self.b, self.t, self.c, self.d = batch, seq, ctx_len, d_model
        self.rows = batch * (seq + ctx_len)
        self.tm = min(ROW_TILE, batch * ctx_len)
        assert (batch * ctx_len) % self.tm == 0 and seq % self.tm == 0
        self.ctx_tiles = batch * ctx_len // self.tm
        self.tiles_per_batch = seq // self.tm
        self.n_tiles = self.rows // self.tm
        self.blk = MIX_BLOCK
        assert ctx_len == self.blk and seq % self.blk == 0
        self.ctx_blocks = batch
        self.nb = seq // self.blk

    def mod_row(self, tile):
        return jnp.where(tile < self.ctx_tiles, self.b, (tile - self.ctx_tiles) // self.tiles_per_batch)

    def fwd_block(self, b, s):
        return jnp.where(s == 0, b, self.ctx_blocks + self.nb * b + s - 1)

    def bwd_block(self, b, s):
        return jnp.where(s == 0, b, self.ctx_blocks + self.nb * b + self.nb - s)


def _proj_even_kernel(x_ref, mod_ref, w_ref, rope_ref, ymain_ref, k2_ref, qt_ref, vt_ref):
    x = x_ref[...]
    h = (x * (1.0 + mod_ref[1:2, :]) + mod_ref[0:1, :]).astype(BF16)
    y = jnp.dot(h, w_ref[...], preferred_element_type=F32)
    ymain_ref[...] = y[:, :EV_MAIN]
    cos = rope_ref[:, 0:128]
    sin_up = rope_ref[:, 128:256]
    sin_dn = rope_ref[:, 256:384]

    def rope(t):
        return t * cos + pltpu.roll(t, 112, 1) * sin_up + pltpu.roll(t, 16, 1) * sin_dn

    for j in range(DIFF_HEADS):
        lo, hi = 128 * j, 128 * (j + 1)
        q = rope(y[:, EV_MAIN + lo:EV_MAIN + hi]) * (DIFF_DQK ** -0.5 * LOG2E)
        qt_ref[lo:hi, :] = q.T.astype(BF16)
        k = rope(y[:, EV_MAIN + EV_DIFF + lo:EV_MAIN + EV_DIFF + hi])
        k2_ref[:, lo:hi] = k.astype(BF16)
        v = y[:, EV_MAIN + 2 * EV_DIFF + lo:EV_MAIN + 2 * EV_DIFF + hi]
        vt_ref[lo:hi, :] = v.T.astype(BF16)


def _proj_even_call(g, s_rows, mod_l, w, rope_tab):
    tm, d = g.tm, g.d
    n = w.shape[1]
    rope_tiles_ctx = 1

    def rope_idx(i):
        return jnp.where(i < g.ctx_tiles, 0, rope_tiles_ctx + (i - g.ctx_tiles) % g.tiles_per_batch)

    return pl.pallas_call(
        _proj_even_kernel,
        grid=(g.n_tiles,),
        in_specs=[pl.BlockSpec((tm, d), lambda i: (i, 0)),
                  pl.BlockSpec((None, 6, d), lambda i: (g.mod_row(i), 0, 0)),
                  _resident((d, n)),
                  pl.BlockSpec((tm, 384), lambda i: (rope_idx(i), 0))],
        out_specs=[pl.BlockSpec((tm, EV_MAIN), lambda i: (i, 0)),
                   pl.BlockSpec((tm, EV_DIFF), lambda i: (i, 0)),
                   pl.BlockSpec((EV_DIFF, tm), lambda i: (0, i)),
                   pl.BlockSpec((EV_DIFF, tm), lambda i: (0, i))],
        out_shape=[jax.ShapeDtypeStruct((g.rows, EV_MAIN), F32),
                   jax.ShapeDtypeStruct((g.rows, EV_DIFF), BF16),
                   jax.ShapeDtypeStruct((EV_DIFF, g.rows), BF16),
                   jax.ShapeDtypeStruct((EV_DIFF, g.rows), BF16)],
        compiler_params=_cparams(("parallel",)),
        name="proj_even",
    )(s_rows, mod_l, w, rope_tab)


def _rope_table(g):
    rows = g.t // GRID_W
    n_freq = DIFF_DQK // 4
    inv = ROPE_BASE ** (-jnp.arange(n_freq, dtype=F32) / n_freq)
    ang_r = jnp.arange(rows, dtype=F32)[:, None] * inv
    ang_c = jnp.arange(GRID_W, dtype=F32)[:, None] * inv
    per_row = lambda tab: jnp.repeat(tab, GRID_W, axis=0)
    per_col = lambda tab: jnp.tile(tab, (rows, 1))
    cr, sr = per_row(jnp.cos(ang_r)), per_row(jnp.sin(ang_r))
    cc, sc = per_col(jnp.cos(ang_c)), per_col(jnp.sin(ang_c))
    z = jnp.zeros_like(cr)
    cos64 = jnp.concatenate([cr, cr, cc, cc], axis=-1)
    up64 = jnp.concatenate([-sr, z, -sc, z], axis=-1)
    dn64 = jnp.concatenate([z, sr, z, sc], axis=-1)
    tab = jnp.concatenate([cos64, cos64, up64, up64, dn64, dn64], axis=-1)
    ident = jnp.concatenate([jnp.ones((g.tm, 128), F32), jnp.zeros((g.tm, 256), F32)], axis=-1)
    return jnp.concatenate([ident, tab], axis=0)


def _gla_kernel(qf, kf, vf, rf, qb, kb, vb, rb, wg_ref, bg_ref, of_ref, ob_ref, st_ref):
    s = pl.program_id(1)

    @pl.when(s == 0)
    def _():
        st_ref[...] = jnp.zeros_like(st_ref)

    n = MIX_BLOCK
    nch = n // GLA_CHUNK
    ri = lax.broadcasted_iota(jnp.int32, (n, n), 0)
    ci = lax.broadcasted_iota(jnp.int32, (n, n), 1)
    same_chunk = (ri // GLA_CHUNK) == (ci // GLA_CHUNK)
    col_chunk = ci // GLA_CHUNK
    lane_head = lax.broadcasted_iota(jnp.int32, (1, EV_GLA_W), 1) // GLA_DK
    diag_blocks = ((lax.broadcasted_iota(jnp.int32, (EV_GLA_W, EV_GLA_V), 0) // GLA_DK)
                   == (lax.broadcasted_iota(jnp.int32, (EV_GLA_W, EV_GLA_V), 1) // GLA_DV))

    dirs = ((qf, kf, vf, rf, of_ref), (qb, kb, vb, rb, ob_ref))
    causal, la, qd, ki, kd, v16 = [], [], [], [], [], []
    for d, (q_ref, k_ref, v_ref, r_ref, _) in enumerate(dirs):
        causal.append(jnp.logical_and(same_chunk, (ci >= ri) if d == 1 else (ci <= ri)))
        z = jnp.dot(r_ref[...].astype(BF16), wg_ref[d], preferred_element_type=F32) + bg_ref[d]
        la.append((jnp.minimum(z, 0.0) - jnp.log1p(jnp.exp(-jnp.abs(z)))) * GLA_INV_TAU)
    for d, (q_ref, k_ref, v_ref, r_ref, _) in enumerate(dirs):
        cum = _split_dot(jnp.where(causal[d], 1.0, 0.0).astype(BF16), la[d])
        tot = jnp.concatenate(
            [jnp.broadcast_to(jnp.sum(la[d][c * GLA_CHUNK:(c + 1) * GLA_CHUNK], axis=0, keepdims=True),
                              (GLA_CHUNK, EV_GLA_W)) for c in range(nch)], axis=0)
        k = k_ref[...]
        v16.append(v_ref[...].astype(BF16))
        qd.append(q_ref[...] * jnp.exp(cum) * (GLA_DK ** -0.5))
        ki.append((k * jnp.exp(-cum)).astype(BF16))
        kd.append(k * jnp.exp(tot - cum))

    parts = ([], [])
    for h in range(GLA_HEADS):
        for d in range(2):
            qh = jnp.where(lane_head == h, qd[d], 0.0).astype(BF16)
            att = lax.dot_general(qh, ki[d], (((1,), (1,)), ((), ())), preferred_element_type=F32)
            att = jnp.where(causal[d], att, 0.0).astype(BF16)
            parts[d].append(jnp.dot(att, v16[d][:, h * GLA_DV:(h + 1) * GLA_DV],
                                    preferred_element_type=F32))

    la_t = [la[d].T for d in range(2)]
    kd_t = [kd[d].T for d in range(2)]
    qd16 = [qd[d].astype(BF16) for d in range(2)]
    state = [st_ref[0], st_ref[1]]
    o_inter = ([None] * nch, [None] * nch)
    for i in range(nch):
        for d in range(2):
            c = nch - 1 - i if d == 1 else i
            in_chunk = col_chunk == c
            o_inter[d][c] = jnp.dot(qd16[d][c * GLA_CHUNK:(c + 1) * GLA_CHUNK], state[d].astype(BF16),
                                    preferred_element_type=F32)
            kv = jnp.dot(jnp.where(in_chunk, kd_t[d], 0.0).astype(BF16), v16[d],
                         preferred_element_type=F32)
            decay = jnp.exp(jnp.sum(jnp.where(in_chunk, la_t[d], 0.0), axis=1, keepdims=True))
            state[d] = decay * state[d] + jnp.where(diag_blocks, kv, 0.0)
    for d in range(2):
        st_ref[d] = state[d]
        dirs[d][4][...] = jnp.concatenate(parts[d], axis=1) + jnp.concatenate(o_inter[d], axis=0)


def _gla_call(g, ymain, wg, bg):
    blk = g.blk
    qw, vw = EV_GLA_W, EV_GLA_V

    def specs(block_fn):
        return [pl.BlockSpec((blk, qw), lambda b, s: (block_fn(b, s), 0)),
                pl.BlockSpec((blk, qw), lambda b, s: (block_fn(b, s), 1)),
                pl.BlockSpec((blk, vw), lambda b, s: (block_fn(b, s), 1)),
                pl.BlockSpec((blk, 128), lambda b, s: (block_fn(b, s), (EV_MAIN - 128) // 128))]

    return pl.pallas_call(
        _gla_kernel,
        grid=(g.b, g.nb + 1),
        in_specs=specs(g.fwd_block) + specs(g.bwd_block) + [
            pl.BlockSpec((2, 128, qw), lambda b, s: (0, 0, 0)),
            pl.BlockSpec((2, 1, qw), lambda b, s: (0, 0, 0))],
        out_specs=[pl.BlockSpec((blk, vw), lambda b, s: (g.fwd_block(b, s), 0)),
                   pl.BlockSpec((blk, vw), lambda b, s: (g.bwd_block(b, s), 0))],
        out_shape=[jax.ShapeDtypeStruct((g.rows, vw), F32)] * 2,
        scratch_shapes=[pltpu.VMEM((2, qw, vw), F32)],
        compiler_params=_cparams(("parallel", "arbitrary")),
        name="gla_scan",
    )(ymain, ymain, ymain, ymain, ymain, ymain, ymain, ymain, wg, bg)


def _part8(x, op):
    rows = [x[i * 8:(i + 1) * 8] for i in range(x.shape[0] // 8)]
    while len(rows) > 1:
        rows = [op(rows[i], rows[i + 1]) for i in range(0, len(rows), 2)]
    return rows[0]


def _attn_kernel(lam_ref, nw_ref, qt_ref, kc_ref, vct_ref, *rest, lam_init, with_latent):
    if with_latent:
        kx_ref, vxt_ref, o_ref, m_sc, acc_sc, nxt_sc = rest
        ki = pl.program_id(3)
        last = pl.num_programs(3) - 1
    else:
        o_ref, m_sc, acc_sc = rest
    qt = qt_ref[...]
    tq = qt.shape[1]
    row = lax.broadcasted_iota(jnp.int32, qt.shape, 0)
    zero = jnp.zeros_like(qt)
    qt_maps = (jnp.where(row < DIFF_DQK, qt, zero), jnp.where(row >= DIFF_DQK, qt, zero))

    def with_ones(vt):
        return jnp.concatenate([vt, jnp.ones((ATTN_ONES, vt.shape[1]), vt.dtype)], axis=0)

    def exact_update(k, vt1):
        for m in range(2):
            st = jnp.dot(k, qt_maps[m], preferred_element_type=F32)
            m_old = m_sc[m]
            m_new = jnp.maximum(m_old, jnp.max(st, axis=0, keepdims=True))
            alpha = jnp.exp2(m_old - m_new)
            p = jnp.exp2(st - m_new)
            acc_sc[m] = alpha * acc_sc[m] + jnp.dot(vt1, p.astype(BF16), preferred_element_type=F32)
            m_sc[m] = m_new

    def lagged_update(k, vt1):
        tk = k.shape[0]
        refs = [m_sc[0], m_sc[1]]
        groups = [(m, gq) for m in range(2) for gq in range(tq // ATTN_QG)]
        cols_of = lambda gq: slice(gq * ATTN_QG, (gq + 1) * ATTN_QG)
        scores = lambda m, gq: jnp.dot(k, qt_maps[m][:, cols_of(gq)], preferred_element_type=F32)
        mx_g = ([], [])
        st = scores(*groups[0])
        for gi, (m, gq) in enumerate(groups):
            cols = cols_of(gq)
            rg = refs[m][:, cols]
            mx = None
            tiles = []
            for c in range(tk // ATTN_KC):
                tile = st[c * ATTN_KC:(c + 1) * ATTN_KC]
                tiles.append(jnp.exp2(tile - rg).astype(BF16))
                tmx = _part8(tile, jnp.maximum)
                mx = tmx if mx is None else jnp.maximum(mx, tmx)
            mx_g[m].append(mx)
            if gi + 1 < len(groups):
                st = scores(*groups[gi + 1])
            pv = jnp.dot(vt1, jnp.concatenate(tiles, axis=0), preferred_element_type=F32)
            nxt_sc[m, :, cols] = acc_sc[m, :, cols] + pv
        bmax = [jnp.max(jnp.concatenate(mx_g[m], axis=1), axis=0, keepdims=True) for m in range(2)]
        lag = jnp.maximum(jnp.max(bmax[0] - refs[0]), jnp.max(bmax[1] - refs[1]))
        safe = lag <= ATTN_LAG_LIMIT

        @pl.when(safe)
        def _():
            for m in range(2):
                r_new = jnp.maximum(refs[m], bmax[m])
                acc_sc[m] = nxt_sc[m] * jnp.exp2(refs[m] - r_new)
                m_sc[m] = r_new

        @pl.when(jnp.logical_not(safe))
        def _():
            exact_update(k, vt1)

    def init_and_ctx():
        m_sc[...] = jnp.full(m_sc.shape, -jnp.inf, F32)
        acc_sc[...] = jnp.zeros(acc_sc.shape, F32)
        exact_update(kc_ref[...], with_ones(vct_ref[...]))

    def finish():
        lp = lam_ref[...]
        lam = (jnp.exp(jnp.sum(lp[0:1] * lp[1:2], keepdims=True))
               - jnp.exp(jnp.sum(lp[2:3] * lp[3:4], keepdims=True)) + lam_init)
        a0, a1 = acc_sc[0], acc_sc[1]
        ot = (a0[:DIFF_DV] / a0[DIFF_DV:DIFF_DV + 1]
              - lam * (a1[:DIFF_DV] / a1[DIFF_DV:DIFF_DV + 1]))
        ms = jnp.mean(ot * ot, axis=0, keepdims=True)
        on = ot * lax.rsqrt(ms + RMS_EPS) * nw_ref[...] * (1.0 - lam_init)
        o_ref[...] = on.T.astype(o_ref.dtype)

    if with_latent:
        pl.when(ki == 0)(init_and_ctx)
        lagged_update(kx_ref[...], with_ones(vxt_ref[...]))
        pl.when(ki == last)(finish)
    else:
        init_and_ctx()
        finish()


def _attn_call(g, k2, qt, vt, lam_p, norm_w_col, lam_init):
    hd = 2 * DIFF_DQK
    tq = min(ATTN_TQ, g.t)
    tk = min(ATTN_TK, g.t)
    assert (g.b * g.c) % tq == 0 and g.t % tq == 0 and g.t % tk == 0 and (g.b * g.c) % tk == 0
    q_off, k_off = g.b * g.c // tq, g.b * g.c // tk
    nq, nk = g.t // tq, g.t // tk
    scratch = lambda nqry: [pltpu.VMEM((2, 1, nqry), F32),
                            pltpu.VMEM((2, DIFF_DV + ATTN_ONES, nqry), F32)]
    common = [pl.BlockSpec((4, DIFF_DQK), lambda *_: (0, 0)),
              pl.BlockSpec((DIFF_DV, 1), lambda *_: (0, 0))]
    od = pl.pallas_call(
        functools.partial(_attn_kernel, lam_init=lam_init, with_latent=True),
        grid=(g.b, DIFF_HEADS, nq, nk),
        in_specs=common + [
            pl.BlockSpec((hd, tq), lambda b, h, qi, ki: (h, q_off + b * nq + qi)),
            pl.BlockSpec((g.c, hd), lambda b, h, qi, ki: (b, h)),
            pl.BlockSpec((DIFF_DV, g.c), lambda b, h, qi, ki: (h, b)),
            pl.BlockSpec((tk, hd), lambda b, h, qi, ki: (k_off + b * nk + ki, h)),
            pl.BlockSpec((DIFF_DV, tk), lambda b, h, qi, ki: (h, k_off + b * nk + ki))],
        out_specs=pl.BlockSpec((tq, DIFF_DV), lambda b, h, qi, ki: (b * nq + qi, h)),
        out_shape=jax.ShapeDtypeStruct((g.b * g.t, DIFF_HEADS * DIFF_DV), BF16),
        scratch_shapes=scratch(tq) + [pltpu.VMEM((2, DIFF_DV + ATTN_ONES, tq), F32)],
        compiler_params=_cparams(("parallel", "parallel", "parallel", "arbitrary")),
        name="diff_attn_latent",
    )(lam_p, norm_w_col, qt, k2, vt, k2, vt)
    return od


def _attn_ctx_call(g, k2, qt, vt, lam_p, norm_w_col, lam_init):
    hd = 2 * DIFF_DQK
    return pl.pallas_call(
        functools.partial(_attn_kernel, lam_init=lam_init, with_latent=False),
        grid=(g.b, DIFF_HEADS),
        in_specs=[pl.BlockSpec((4, DIFF_DQK), lambda *_: (0, 0)),
                  pl.BlockSpec((DIFF_DV, 1), lambda *_: (0, 0)),
                  pl.BlockSpec((hd, g.c), lambda b, h: (h, b)),
                  pl.BlockSpec((g.c, hd), lambda b, h: (b, h)),
                  pl.BlockSpec((DIFF_DV, g.c), lambda b, h: (h, b))],
        out_specs=pl.BlockSpec((g.c, DIFF_DV), lambda b, h: (b, h)),
        out_shape=jax.ShapeDtypeStruct((g.b * g.c, DIFF_HEADS * DIFF_DV), BF16),
        scratch_shapes=[pltpu.VMEM((2, 1, g.c), F32),
                        pltpu.VMEM((2, DIFF_DV + ATTN_ONES, g.c), F32)],
        compiler_params=_cparams(("parallel", "parallel")),
        name="diff_attn_context",
    )(lam_p, norm_w_col, qt, k2, vt)


def _proj_odd_kernel(x_ref, xp_ref, xn_ref, mod_ref, w_ref, cw_ref, cb_ref, z_ref, xbc_ref, dt_ref,
                     seq_ref, *, ctx_blocks, nb, d_inner, conv_ch):
    i = pl.program_id(0)
    j = (i - ctx_blocks) % nb
    is_lat = i >= ctx_blocks
    has_prev = jnp.logical_and(is_lat, j > 0).astype(F32)
    has_next = jnp.logical_and(is_lat, j < nb - 1).astype(F32)
    scale = 1.0 + mod_ref[1:2, :]
    shift = mod_ref[0:1, :]
    rows = jnp.concatenate([x_ref[...], xp_ref[...], xn_ref[...]], axis=0)
    h = (rows * scale + shift).astype(BF16)
    n = x_ref.shape[0]
    cw = PROJ_CHUNK
    other = [(z_ref, c0, c0) for c0 in range(0, d_inner, cw)]
    other += [(dt_ref, c0, d_inner + conv_ch + c0) for c0 in range(0, dt_ref.shape[1], cw)]
    pad = SSD_CONV // 2
    for jc, c0 in enumerate(range(0, conv_ch, cw)):
        cols = slice(c0, c0 + cw)
        y = jnp.dot(h, w_ref[:, d_inner + c0:d_inner + c0 + cw], preferred_element_type=F32)
        seq_ref[0:8, cols] = y[n:n + 8] * has_prev
        seq_ref[8:8 + n, cols] = y[:n]
        seq_ref[8 + n:16 + n, cols] = y[n + 8:n + 16] * has_next
        if jc < len(other):
            o_ref, oc, wc = other[jc]
            o_ref[:, oc:oc + cw] = jnp.dot(h[:n], w_ref[:, wc:wc + cw], preferred_element_type=F32)
        w = cw_ref[:, cols]
        acc = cb_ref[:, cols] + seq_ref[8:8 + n, cols] * w[pad:pad + 1]
        for tap in range(SSD_CONV):
            if tap != pad:
                acc = acc + seq_ref[pl.ds(8 - pad + tap, n), cols] * w[tap:tap + 1]
        xbc_ref[:, cols] = _silu(acc)
    assert len(other) <= conv_ch // cw


def _proj_odd_call(g, s_rows, mod_l, w, conv_w8, conv_b, d_inner, conv_ch):
    blk = g.blk
    assert g.tm % blk == 0
    per_tile = g.tm // blk
    n = w.shape[1]
    dt_w = n - d_inner - conv_ch
    nrow8 = g.rows // 8
    per = blk // 8
    row = lambda i: (i, 0)
    return pl.pallas_call(
        functools.partial(_proj_odd_kernel, ctx_blocks=g.ctx_blocks, nb=g.nb, d_inner=d_inner,
                          conv_ch=conv_ch),
        grid=(g.rows // blk,),
        in_specs=[pl.BlockSpec((blk, g.d), row),
                  pl.BlockSpec((8, g.d), lambda i: (jnp.maximum(i * per - 1, 0), 0)),
                  pl.BlockSpec((8, g.d), lambda i: (jnp.minimum((i + 1) * per, nrow8 - 1), 0)),
                  pl.BlockSpec((None, 6, g.d), lambda i: (g.mod_row(i // per_tile), 0, 0)),
                  _resident((g.d, n)),
                  pl.BlockSpec((8, conv_ch), lambda i: (0, 0)),
                  pl.BlockSpec((1, conv_ch), lambda i: (0, 0))],
        out_specs=[pl.BlockSpec((blk, d_inner), row),
                   pl.BlockSpec((blk, conv_ch), row),
                   pl.BlockSpec((blk, dt_w), row)],
        out_shape=[jax.ShapeDtypeStruct((g.rows, d_inner), F32),
                   jax.ShapeDtypeStruct((g.rows, conv_ch), F32),
                   jax.ShapeDtypeStruct((g.rows, dt_w), F32)],
        scratch_shapes=[pltpu.VMEM((blk + 16, conv_ch), F32)],
        compiler_params=_cparams(("parallel",)),
        name="proj_odd_conv",
    )(s_rows, s_rows, s_rows, mod_l, w, conv_w8, conv_b)


def _ssd_kernel(xf, bmf, cmf, dtf, xb, bmb, cmb, dtb, bias_ref, alog_ref, dsk_ref, yf_ref, yb_ref,
                st_ref):
    s = pl.program_id(2)

    @pl.when(s == 0)
    def _():
        st_ref[...] = jnp.zeros_like(st_ref)

    n, ch = MIX_BLOCK, SSD_CHUNK
    nch = n // ch
    gw = SSD_HEADS_PER_GROUP * SSD_HEAD_DIM
    ri = lax.broadcasted_iota(jnp.int32, (n, n), 0)
    ci = lax.broadcasted_iota(jnp.int32, (n, n), 1)
    same_chunk = (ri // ch) == (ci // ch)
    ri_c = lax.broadcasted_iota(jnp.int32, (ch, ch), 0)
    ci_c = lax.broadcasted_iota(jnp.int32, (ch, ch), 1)
    lane_c = lax.broadcasted_iota(jnp.int32, (ch, 128), 1)
    a_neg = -jnp.exp(alog_ref[...])
    exp_row = lax.broadcasted_iota(jnp.int32, (128, gw), 0)
    exp_col = lax.broadcasted_iota(jnp.int32, (128, gw), 1) // SSD_HEAD_DIM

    dirs = ((xf, bmf, cmf, dtf, yf_ref), (xb, bmb, cmb, dtb, yb_ref))
    dt, acs, acs_t, dt_t, expand, tri = [], [], [], [], [], []
    for d in range(2):
        lane0 = SSD_HEADS_PER_GROUP * d
        expand.append(jnp.where(exp_row == exp_col + lane0, 1.0, 0.0).astype(BF16))
        z = dirs[d][3][...] + bias_ref[...]
        dt.append(jnp.maximum(z, 0.0) + jnp.log1p(jnp.exp(-jnp.abs(z))))
        causal_blk = jnp.logical_and(same_chunk, (ci >= ri) if d == 1 else (ci <= ri))
        acs.append(_split_dot(jnp.where(causal_blk, 1.0, 0.0).astype(BF16), dt[d] * a_neg))
        tri.append((ci_c >= ri_c) if d == 1 else (ci_c <= ri_c))
    for d in range(2):
        acs_t.append(acs[d].T)
        dt_t.append(dt[d].T)
    state = [st_ref[0], st_ref[1]]

    for i in range(nch):
        for d in range(2):
            x_ref, bm_ref, cm_ref, _, y_ref = dirs[d]
            rev = d == 1
            lane0 = SSD_HEADS_PER_GROUP * d
            c = nch - 1 - i if rev else i
            r0, r1 = c * ch, (c + 1) * ch
            xc = x_ref[r0:r1, :]
            bmc = bm_ref[r0:r1, :]
            acs_c = acs[d][r0:r1, :]
            acs_tc = acs_t[d][:, r0:r1]
            dt_tc = dt_t[d][:, r0:r1]
            cm16 = cm_ref[r0:r1, :].astype(BF16)
            cb = lax.dot_general(cm16, bmc.astype(BF16), (((1,), (1,)), ((), ())),
                                 preferred_element_type=F32)
            a_last = acs_c[0:1, :] if rev else acs_c[ch - 1:ch, :]
            fac = jnp.concatenate([jnp.exp(acs_c), dt[d][r0:r1, :] * jnp.exp(a_last - acs_c)], axis=0)
            fac = jnp.dot(fac.astype(BF16), expand[d], preferred_element_type=F32)
            dec = _split_dot_r(jnp.broadcast_to(jnp.exp(a_last), (8, 128)), expand[d])[0:1]
            pairs = []
            for pp in range(SSD_HEADS_PER_GROUP // 2):
                mats = []
                for e in (2 * pp, 2 * pp + 1):
                    ln = lane0 + e
                    seg = jnp.exp(jnp.where(tri[d], acs_c[:, ln:ln + 1] - acs_tc[ln:ln + 1, :], -jnp.inf))
                    mats.append((cb * seg * dt_tc[ln:ln + 1, :]).astype(BF16))
                xp = xc[:, 128 * pp:128 * (pp + 1)]
                x2 = jnp.concatenate([jnp.where(lane_c < SSD_HEAD_DIM, xp, 0.0),
                                      jnp.where(lane_c >= SSD_HEAD_DIM, xp, 0.0)], axis=0).astype(BF16)
                pairs.append(jnp.dot(jnp.concatenate(mats, axis=1), x2, preferred_element_type=F32))
            y_off = jnp.dot(cm16, state[d].astype(BF16), preferred_element_type=F32) * fac[0:ch]
            y = jnp.concatenate(pairs, axis=1) + y_off
            if not rev:
                y = y + dsk_ref[...] * xc
            y_ref[r0:r1, :] = y
            upd = jnp.dot(bmc.T.astype(BF16), (xc * fac[ch:2 * ch]).astype(BF16),
                          preferred_element_type=F32)
            state[d] = dec * state[d] + upd
    st_ref[0] = state[0]
    st_ref[1] = state[1]


def _ssd_call(g, xbc, y_odd, dt_bias_slab, a_log_slab, d_skip, d_inner):
    blk = g.blk
    gw = SSD_HEADS_PER_GROUP * SSD_HEAD_DIM
    bm0 = d_inner // 128
    cm0 = bm0 + SSD_GROUPS
    dt0 = 0

    def specs(block_fn):
        return [pl.BlockSpec((blk, gw), lambda b, gi, s: (block_fn(b, s), gi)),
                pl.BlockSpec((blk, 128), lambda b, gi, s: (block_fn(b, s), bm0 + gi)),
                pl.BlockSpec((blk, 128), lambda b, gi, s: (block_fn(b, s), cm0 + gi)),
                pl.BlockSpec((blk, 128), lambda b, gi, s: (block_fn(b, s), dt0 + gi))]

    return pl.pallas_call(
        _ssd_kernel,
        grid=(g.b, SSD_GROUPS, g.nb + 1),
        in_specs=specs(g.fwd_block) + specs(g.bwd_block) + [
            pl.BlockSpec((None, 1, 128), lambda b, gi, s: (gi, 0, 0)),
            pl.BlockSpec((None, 1, 128), lambda b, gi, s: (gi, 0, 0)),
            pl.BlockSpec((None, 1, gw), lambda b, gi, s: (gi, 0, 0))],
        out_specs=[pl.BlockSpec((blk, gw), lambda b, gi, s: (g.fwd_block(b, s), gi)),
                   pl.BlockSpec((blk, gw), lambda b, gi, s: (g.bwd_block(b, s), gi))],
        out_shape=[jax.ShapeDtypeStruct((g.rows, d_inner), F32)] * 2,
        scratch_shapes=[pltpu.VMEM((2, SSD_STATE, gw), F32)],
        compiler_params=_cparams(("parallel", "parallel", "arbitrary")),
        name="ssd_scan",
    )(xbc, xbc, xbc, y_odd, xbc, xbc, xbc, y_odd, dt_bias_slab, a_log_slab,
      d_skip.reshape(SSD_GROUPS, 1, gw))


def _mix_even_kernel(x_ref, mod_ref, of_ref, ob_ref, g_ref, odc_ref, odl_ref, nw_ref, w_ref, lng_ref,
                     lnb_ref, o_ref, *, alpha, ctx_tiles):
    o = of_ref[...] + ob_ref[...]
    gate = g_ref[...]
    nw = nw_ref[...]
    parts = []
    for h in range(GLA_HEADS):
        oh = o[:, h * GLA_DV:(h + 1) * GLA_DV]
        ms = jnp.mean(oh * oh, axis=-1, keepdims=True)
        parts.append(oh * lax.rsqrt(ms + RMS_EPS) * nw)
    gla = (jnp.concatenate(parts, axis=1) * _silu(gate)).astype(BF16)
    od = jnp.where(pl.program_id(0) < ctx_tiles, odc_ref[...], odl_ref[...])
    mixin = jnp.concatenate([gla, od], axis=1)
    mix = jnp.dot(mixin, w_ref[...], preferred_element_type=F32)
    v = alpha * x_ref[...] + mod_ref[2:3, :] * mix
    o_ref[...] = _layer_norm(v, lng_ref[...], lnb_ref[...])


def _mix_even_call(g, s_rows, mod_l, o_f, o_b, ymain, od_ctx, od_lat, gla_nw, w_out, ln_g, ln_b, alpha):
    tm, d = g.tm, g.d
    vw = EV_GLA_V
    row = lambda i: (i, 0)
    return pl.pallas_call(
        functools.partial(_mix_even_kernel, alpha=alpha, ctx_tiles=g.ctx_tiles),
        grid=(g.n_tiles,),
        in_specs=[pl.BlockSpec((tm, d), row),
                  pl.BlockSpec((None, 6, d), lambda i: (g.mod_row(i), 0, 0)),
                  pl.BlockSpec((tm, vw), row),
                  pl.BlockSpec((tm, vw), row),
                  pl.BlockSpec((tm, vw), lambda i: (i, 2 * EV_GLA_W // vw + 1)),
                  pl.BlockSpec((tm, vw), lambda i: (jnp.minimum(i, g.ctx_tiles - 1), 0)),
                  pl.BlockSpec((tm, vw), lambda i: (jnp.maximum(i - g.ctx_tiles, 0), 0)),
                  pl.BlockSpec((1, GLA_DV), lambda i: (0, 0)),
                  _resident(w_out.shape),
                  pl.BlockSpec((1, d), lambda i: (0, 0)),
                  pl.BlockSpec((1, d), lambda i: (0, 0))],
        out_specs=pl.BlockSpec((tm, d), row),
        out_shape=jax.ShapeDtypeStruct((g.rows, d), F32),
        compiler_params=_cparams(("parallel",)),
        name="mix_even",
    )(s_rows, mod_l, o_f, o_b, ymain, od_ctx, od_lat, gla_nw, w_out, ln_g, ln_b)


def _mix_odd_kernel(x_ref, mod_ref, yf_ref, yb_ref, z_ref, nw_ref, w_ref, lng_ref, lnb_ref, o_ref,
                    *, alpha):
    y = (yf_ref[...] + yb_ref[...]) * _silu(z_ref[...])
    gw = y.shape[1] // SSD_GROUPS
    parts = []
    for gi in range(SSD_GROUPS):
        yg = y[:, gi * gw:(gi + 1) * gw]
        ms = jnp.mean(yg * yg, axis=-1, keepdims=True)
        parts.append(yg * lax.rsqrt(ms + RMS_EPS))
    yn = (jnp.concatenate(parts, axis=1) * nw_ref[...]).astype(BF16)
    mix = jnp.dot(yn, w_ref[...], preferred_element_type=F32)
    v = alpha * x_ref[...] + mod_ref[2:3, :] * mix
    o_ref[...] = _layer_norm(v, lng_ref[...], lnb_ref[...])


def _mix_odd_call(g, s_rows, mod_l, y_f, y_b, z_gate, norm_w, w_out, ln_g, ln_b, alpha):
    tm = g.tm // 2
    d = g.d
    di = w_out.shape[0]
    row = lambda i: (i, 0)
    vec = lambda n: pl.BlockSpec((1, n), lambda i: (0, 0))
    return pl.pallas_call(
        functools.partial(_mix_odd_kernel, alpha=alpha),
        grid=(g.rows // tm,),
        in_specs=[pl.BlockSpec((tm, d), row),
                  pl.BlockSpec((None, 6, d), lambda i: (g.mod_row(i // 2), 0, 0)),
                  pl.BlockSpec((tm, di), row),
                  pl.BlockSpec((tm, di), row),
                  pl.BlockSpec((tm, di), row),
                  vec(di),
                  _resident(w_out.shape),
                  vec(d), vec(d)],
        out_specs=pl.BlockSpec((tm, d), row),
        out_shape=jax.ShapeDtypeStruct((g.rows, d), F32),
        compiler_params=_cparams(("parallel",)),
        name="mix_odd",
    )(s_rows, mod_l, y_f, y_b, z_gate, norm_w, w_out, ln_g, ln_b)


def _ffn_kernel(x_ref, mod_ref, win_ref, wout_ref, lng_ref, lnb_ref, o_ref, *, alpha, hidden, n_chunks):
    x = x_ref[...]
    h = (x * (1.0 + mod_ref[4:5, :]) + mod_ref[3:4, :]).astype(BF16)
    hc = hidden // n_chunks
    acc = jnp.zeros(x.shape, F32)
    for j in range(n_chunks):
        gate = jnp.dot(h, win_ref[:, j * hc:(j + 1) * hc], preferred_element_type=F32)
        up = jnp.dot(h, win_ref[:, hidden + j * hc:hidden + (j + 1) * hc], preferred_element_type=F32)
        act = (_silu(gate) * up).astype(BF16)
        acc = acc + jnp.dot(act, wout_ref[j * hc:(j + 1) * hc, :], preferred_element_type=F32)
    v = alpha * x + mod_ref[5:6, :] * acc
    o_ref[...] = _layer_norm(v, lng_ref[...], lnb_ref[...])


def _ffn_call(g, s_rows, mod_l, w_in, w_out, ln_g, ln_b, alpha, latent_only):
    tm, d = g.tm, g.d
    hidden = w_out.shape[0]
    n_chunks = 2 if hidden % 256 == 0 else 1
    off = g.ctx_tiles if latent_only else 0
    n_tiles = g.n_tiles - off
    return pl.pallas_call(
        functools.partial(_ffn_kernel, alpha=alpha, hidden=hidden, n_chunks=n_chunks),
        grid=(n_tiles,),
        in_specs=[pl.BlockSpec((tm, d), lambda i: (i + off, 0)),
                  pl.BlockSpec((None, 6, d), lambda i: (g.mod_row(i + off), 0, 0)),
                  _resident(w_in.shape),
                  _resident(w_out.shape),
                  pl.BlockSpec((1, d), lambda i: (0, 0)),
                  pl.BlockSpec((1, d), lambda i: (0, 0))],
        out_specs=pl.BlockSpec((tm, d), lambda i: (i, 0)),
        out_shape=jax.ShapeDtypeStruct((n_tiles * tm, d), F32),
        compiler_params=_cparams(("parallel",)),
        name="ffn",
    )(s_rows, mod_l, w_in, w_out, ln_g, ln_b)


def kernel(x, c, ctx, c_ctx, mod_w, mod_b, ln_g, ln_b, ffn_w_in, ffn_w_out, ev_w_in, ev_w_out,
           gla_w_gate2, gla_b_gate, gla_norm_w, diff_lambda, diff_norm_w, ssd_w_in, ssd_conv_w,
           ssd_conv_b, ssd_dt_bias, ssd_a_log, ssd_d, ssd_norm_w, ssd_w_out):
    batch, seq, d = x.shape
    ctx_len = ctx.shape[1]
    depth = mod_w.shape[0]
    g = _Geom(batch, seq, ctx_len, d)
    alpha = (2 * depth) ** 0.25
    d_inner = ssd_w_out.shape[1]
    conv_ch = ssd_conv_w.shape[2]
    n_heads = ssd_d.shape[1]

    s_rows = jnp.concatenate([ctx.reshape(batch * ctx_len, d), x.reshape(batch * seq, d)], axis=0)
    cc = jnp.zeros((8, d), F32).at[:batch].set(c).at[batch].set(c_ctx)
    mod_all = _mod_call(cc, mod_w, mod_b).reshape(depth, 8, 6, d)
    rope_tab = _rope_table(g)

    for layer in range(depth):
        need_ctx = layer < depth - 1
        mod_l = mod_all[layer]
        if layer % 2 == 0:
            e = layer // 2
            w = ev_w_in[e]
            w_my = jnp.concatenate([w[:, :EV_REAL_MAIN], jnp.zeros((d, 128 - 2 * GLA_RANK), w.dtype),
                                    w[:, EV_REAL_MAIN:]], axis=1).astype(BF16)
            ymain, k2, qt, vt = _proj_even_call(g, s_rows, mod_l, w_my, rope_tab)
            wg = jnp.zeros((2, 128, EV_GLA_W), F32)
            wg = wg.at[0, :GLA_RANK].set(gla_w_gate2[e, 0]).at[1, GLA_RANK:2 * GLA_RANK].set(gla_w_gate2[e, 1])
            o_f, o_b = _gla_call(g, ymain, wg.astype(BF16), gla_b_gate[e].reshape(2, 1, EV_GLA_W))
            lam_init = 0.8 - 0.6 * math.exp(-0.3 * layer)
            nw_col = diff_norm_w[e].reshape(DIFF_DV, 1)
            od_lat = _attn_call(g, k2, qt, vt, diff_lambda[e], nw_col, lam_init)
            od_ctx = _attn_ctx_call(g, k2, qt, vt, diff_lambda[e], nw_col, lam_init)
            s_rows = _mix_even_call(g, s_rows, mod_l, o_f, o_b, ymain, od_ctx, od_lat,
                                    gla_norm_w[e].reshape(1, GLA_DV),
                                    ev_w_out[e].astype(BF16), ln_g[layer, 0].reshape(1, d),
                                    ln_b[layer, 0].reshape(1, d), alpha)
        else:
            o = layer // 2
            w = ssd_w_in[o]
            dt_col0 = d_inner + conv_ch
            hpg = SSD_HEADS_PER_GROUP
            slabs = []
            for gi in range(SSD_GROUPS):
                slabs += [w[:, dt_col0 + hpg * gi:dt_col0 + hpg * (gi + 1)],
                          w[:, dt_col0 + n_heads + hpg * gi:dt_col0 + n_heads + hpg * (gi + 1)],
                          jnp.zeros((d, 128 - 2 * hpg), w.dtype)]
            w_my = jnp.concatenate([w[:, :dt_col0]] + slabs, axis=1).astype(BF16)

            def slab_vec(v2):
                rows = [jnp.concatenate([v2[0, hpg * gi:hpg * (gi + 1)], v2[1, hpg * gi:hpg * (gi + 1)],
                                         jnp.zeros((128 - 2 * hpg,), F32)]) for gi in range(SSD_GROUPS)]
                return jnp.stack(rows).reshape(SSD_GROUPS, 1, 128)

            conv_w8 = jnp.concatenate([ssd_conv_w[o], jnp.zeros((8 - SSD_CONV, conv_ch), F32)], axis=0)
            z_gate, xbc, dt_raw = _proj_odd_call(g, s_rows, mod_l, w_my, conv_w8,
                                                 ssd_conv_b[o].reshape(1, conv_ch), d_inner, conv_ch)
            d_skip = jnp.repeat(ssd_d[o], SSD_HEAD_DIM)
            y_f, y_b = _ssd_call(g, xbc, dt_raw, slab_vec(ssd_dt_bias[o]), slab_vec(ssd_a_log[o]),
                                 d_skip, d_inner)
            s_rows = _mix_odd_call(g, s_rows, mod_l, y_f, y_b, z_gate,
                                   ssd_norm_w[o].reshape(1, d_inner), ssd_w_out[o].astype(BF16),
                                   ln_g[layer, 0].reshape(1, d), ln_b[layer, 0].reshape(1, d), alpha)
        s_rows = _ffn_call(g, s_rows, mod_l, ffn_w_in[layer].astype(BF16), ffn_w_out[layer].astype(BF16),
                           ln_g[layer, 1].reshape(1, d), ln_b[layer, 1].reshape(1, d), alpha,
                           latent_only=not need_ctx)
    return s_rows.reshape(batch, seq, d)
```

```python
import functools
import math

import jax
import jax.numpy as jnp
from jax import lax
from jax.experimental import pallas as pl
from jax.experimental.pallas import tpu as pltpu

F32 = jnp.float32
BF16 = jnp.bfloat16

GRID_W = 64
GLA_HEADS, GLA_DK, GLA_DV, GLA_RANK, GLA_CHUNK = 4, 64, 128, 16, 64
GLA_INV_TAU = 1.0 / 16.0
DIFF_HEADS, DIFF_DQK, DIFF_DV = 4, 64, 128
ROPE_BASE = 10000.0
SSD_HEAD_DIM, SSD_GROUPS, SSD_STATE, SSD_CONV, SSD_CHUNK = 64, 4, 128, 5, 128
SSD_HEADS_PER_GROUP = 8
LN_EPS = 1e-6
RMS_EPS = 1e-6

EV_GLA_W = GLA_HEADS * GLA_DK
EV_GLA_V = GLA_HEADS * GLA_DV
EV_MAIN = 2 * EV_GLA_W + 2 * EV_GLA_V + 128
EV_DIFF = DIFF_HEADS * 2 * DIFF_DQK
EV_REAL_MAIN = 2 * EV_GLA_W + 2 * EV_GLA_V + 2 * GLA_RANK

ROW_TILE = 512
MIX_BLOCK = 256
ATTN_TQ = 1024
ATTN_TK = 1024
ATTN_QG = 256
ATTN_KC = 128
ATTN_LAG_LIMIT = 50.0
ATTN_ONES = 16
LOG2E = 1.4426950408889634
PROJ_CHUNK = 512
MXU_TILE = 256
FFN_CHUNKS = 4
VMEM_LIMIT = 56 * 1024 * 1024


def _cparams(sem):
    return pltpu.CompilerParams(dimension_semantics=sem, vmem_limit_bytes=VMEM_LIMIT)


def _resident(shape):
    nd = len(shape)
    return pl.BlockSpec(shape, lambda *_: (0,) * nd, pipeline_mode=pl.Buffered(1))


def _silu(v):
    return v / (1.0 + jnp.exp(-v))


def _layer_norm(v, g, b):
    mu = jnp.mean(v, axis=-1, keepdims=True)
    d = v - mu
    var = jnp.mean(d * d, axis=-1, keepdims=True)
    return d * lax.rsqrt(var + LN_EPS) * g + b


def _split_dot(mat_bf16, v):
    hi = v.astype(BF16)
    lo = (v - hi.astype(F32)).astype(BF16)
    return (jnp.dot(mat_bf16, hi, preferred_element_type=F32)
            + jnp.dot(mat_bf16, lo, preferred_element_type=F32))


def _split_dot_r(v, mat_bf16):
    hi = v.astype(BF16)
    lo = (v - hi.astype(F32)).astype(BF16)
    return (jnp.dot(hi, mat_bf16, preferred_element_type=F32)
            + jnp.dot(lo, mat_bf16, preferred_element_type=F32))


def _mod_kernel(c_ref, w_ref, b_ref, o_ref):
    s = _silu(c_ref[...]).astype(BF16)
    o_ref[...] = jnp.dot(s, w_ref[...].astype(BF16), preferred_element_type=F32) + b_ref[...]


def _mod_call(cc, mod_w, mod_b):
    depth, d, n = mod_w.shape
    cw = d
    return pl.pallas_call(
        _mod_kernel,
        grid=(depth, n // cw),
        in_specs=[pl.BlockSpec((8, d), lambda l, j: (0, 0)),
                  pl.BlockSpec((None, d, cw), lambda l, j: (l, 0, j)),
                  pl.BlockSpec((None, 1, cw), lambda l, j: (l, 0, j))],
        out_specs=pl.BlockSpec((None, 8, cw), lambda l, j: (l, 0, j)),
        out_shape=jax.ShapeDtypeStruct((depth, 8, n), F32),
        compiler_params=_cparams(("parallel", "parallel")),
        name="mod_vectors",
    )(cc, mod_w, mod_b.reshape(depth, 1, n))


class _Geom:
    def __init__(self, batch, seq, ctx_len, d_model):
        self.b, self.t, self.c, self.d = batch, seq, ctx_len, d_model
        self.rows = batch * (seq + ctx_len)
        self.tm = min(ROW_TILE, batch * ctx_len)
        assert (batch * ctx_len) % self.tm == 0 and seq % self.tm == 0
        self.ctx_tiles = batch * ctx_len // self.tm
        self.tiles_per_batch = seq // self.tm
        self.n_tiles = self.rows // self.tm
        self.blk = MIX_BLOCK
        assert ctx_len == self.blk and seq % self.blk == 0
        self.ctx_blocks = batch
        self.nb = seq // self.blk

    def mod_row(self, tile):
        return jnp.where(tile < self.ctx_tiles, self.b, (tile - self.ctx_tiles) // self.tiles_per_batch)

    def fwd_block(self, b, s):
        return jnp.where(s == 0, b, self.ctx_blocks + self.nb * b + s - 1)

    def bwd_block(self, b, s):
        return jnp.where(s == 0, b, self.ctx_blocks + self.nb * b + self.nb - s)


def _proj_even_kernel(x_ref, mod_ref, w_ref, rope_ref, ymain_ref, k2_ref, qt_ref, vt_ref):
    x = x_ref[...]
    h = (x * (1.0 + mod_ref[1:2, :]) + mod_ref[0:1, :]).astype(BF16)
    y = jnp.dot(h, w_ref[...], preferred_element_type=F32)
    ymain_ref[...] = y[:, :EV_MAIN]
    cos = rope_ref[:, 0:128]
    sin_up = rope_ref[:, 128:256]
    sin_dn = rope_ref[:, 256:384]

    def rope(t):
        return t * cos + pltpu.roll(t, 112, 1) * sin_up + pltpu.roll(t, 16, 1) * sin_dn

    for j in range(DIFF_HEADS):
        lo, hi = 128 * j, 128 * (j + 1)
        q = rope(y[:, EV_MAIN + lo:EV_MAIN + hi]) * (DIFF_DQK ** -0.5 * LOG2E)
        qt_ref[lo:hi, :] = q.T.astype(BF16)
        k = rope(y[:, EV_MAIN + EV_DIFF + lo:EV_MAIN + EV_DIFF + hi])
        k2_ref[:, lo:hi] = k.astype(BF16)
        v = y[:, EV_MAIN + 2 * EV_DIFF + lo:EV_MAIN + 2 * EV_DIFF + hi]
        vt_ref[lo:hi, :] = v.T.astype(BF16)


def _proj_even_call(g, s_rows, mod_l, w, rope_tab):
    tm, d = g.tm, g.d
    n = w.shape[1]
    rope_tiles_ctx = 1

    def rope_idx(i):
        return jnp.where(i < g.ctx_tiles, 0, rope_tiles_ctx + (i - g.ctx_tiles) % g.tiles_per_batch)

    return pl.pallas_call(
        _proj_even_kernel,
        grid=(g.n_tiles,),
        in_specs=[pl.BlockSpec((tm, d), lambda i: (i, 0)),
                  pl.BlockSpec((None, 6, d), lambda i: (g.mod_row(i), 0, 0)),
                  _resident((d, n)),
                  pl.BlockSpec((tm, 384), lambda i: (rope_idx(i), 0))],
        out_specs=[pl.BlockSpec((tm, EV_MAIN), lambda i: (i, 0)),
                   pl.BlockSpec((tm, EV_DIFF), lambda i: (i, 0)),
                   pl.BlockSpec((EV_DIFF, tm), lambda i: (0, i)),
                   pl.BlockSpec((EV_DIFF, tm), lambda i: (0, i))],
        out_shape=[jax.ShapeDtypeStruct((g.rows, EV_MAIN), F32),
                   jax.ShapeDtypeStruct((g.rows, EV_DIFF), BF16),
                   jax.ShapeDtypeStruct((EV_DIFF, g.rows), BF16),
                   jax.ShapeDtypeStruct((EV_DIFF, g.rows), BF16)],
        compiler_params=_cparams(("parallel",)),
        name="proj_even",
    )(s_rows, mod_l, w, rope_tab)


def _rope_table(g):
    rows = g.t // GRID_W
    n_freq = DIFF_DQK // 4
    inv = ROPE_BASE ** (-jnp.arange(n_freq, dtype=F32) / n_freq)
    ang_r = jnp.arange(rows, dtype=F32)[:, None] * inv
    ang_c = jnp.arange(GRID_W, dtype=F32)[:, None] * inv
    per_row = lambda tab: jnp.repeat(tab, GRID_W, axis=0)
    per_col = lambda tab: jnp.tile(tab, (rows, 1))
    cr, sr = per_row(jnp.cos(ang_r)), per_row(jnp.sin(ang_r))
    cc, sc = per_col(jnp.cos(ang_c)), per_col(jnp.sin(ang_c))
    z = jnp.zeros_like(cr)
    cos64 = jnp.concatenate([cr, cr, cc, cc], axis=-1)
    up64 = jnp.concatenate([-sr, z, -sc, z], axis=-1)
    dn64 = jnp.concatenate([z, sr, z, sc], axis=-1)
    tab = jnp.concatenate([cos64, cos64, up64, up64, dn64, dn64], axis=-1)
    ident = jnp.concatenate([jnp.ones((g.tm, 128), F32), jnp.zeros((g.tm, 256), F32)], axis=-1)
    return jnp.concatenate([ident, tab], axis=0)


def _gla_kernel(qf, kf, vf, rf, qb, kb, vb, rb, wg_ref, bg_ref, of_ref, ob_ref, st_ref):
    s = pl.program_id(1)

    @pl.when(s == 0)
    def _():
        st_ref[...] = jnp.zeros_like(st_ref)

    n = MIX_BLOCK
    nch = n // GLA_CHUNK
    ri = lax.broadcasted_iota(jnp.int32, (n, n), 0)
    ci = lax.broadcasted_iota(jnp.int32, (n, n), 1)
    same_chunk = (ri // GLA_CHUNK) == (ci // GLA_CHUNK)
    col_chunk = ci // GLA_CHUNK
    lane_head = lax.broadcasted_iota(jnp.int32, (1, EV_GLA_W), 1) // GLA_DK
    diag_blocks = ((lax.broadcasted_iota(jnp.int32, (EV_GLA_W, EV_GLA_V), 0) // GLA_DK)
                   == (lax.broadcasted_iota(jnp.int32, (EV_GLA_W, EV_GLA_V), 1) // GLA_DV))

    dirs = ((qf, kf, vf, rf, of_ref), (qb, kb, vb, rb, ob_ref))
    causal, la, qd, ki, kd, v16 = [], [], [], [], [], []
    for d, (q_ref, k_ref, v_ref, r_ref, _) in enumerate(dirs):
        causal.append(jnp.logical_and(same_chunk, (ci >= ri) if d == 1 else (ci <= ri)))
        z = jnp.dot(r_ref[...].astype(BF16), wg_ref[d], preferred_element_type=F32) + bg_ref[d]
        la.append((jnp.minimum(z, 0.0) - jnp.log1p(jnp.exp(-jnp.abs(z)))) * GLA_INV_TAU)
    for d, (q_ref, k_ref, v_ref, r_ref, _) in enumerate(dirs):
        cum = _split_dot(jnp.where(causal[d], 1.0, 0.0).astype(BF16), la[d])
        tot = jnp.concatenate(
            [jnp.broadcast_to(jnp.sum(la[d][c * GLA_CHUNK:(c + 1) * GLA_CHUNK], axis=0, keepdims=True),
                              (GLA_CHUNK, EV_GLA_W)) for c in range(nch)], axis=0)
        k = k_ref[...]
        v16.append(v_ref[...].astype(BF16))
        qd.append(q_ref[...] * jnp.exp(cum) * (GLA_DK ** -0.5))
        ki.append((k * jnp.exp(-cum)).astype(BF16))
        kd.append(k * jnp.exp(tot - cum))

    parts = ([], [])
    for h in range(GLA_HEADS):
        for d in range(2):
            qh = jnp.where(lane_head == h, qd[d], 0.0).astype(BF16)
            att = lax.dot_general(qh, ki[d], (((1,), (1,)), ((), ())), preferred_element_type=F32)
            att = jnp.where(causal[d], att, 0.0).astype(BF16)
            parts[d].append(jnp.dot(att, v16[d][:, h * GLA_DV:(h + 1) * GLA_DV],
                                    preferred_element_type=F32))

    la_t = [la[d].T for d in range(2)]
    kd_t = [kd[d].T for d in range(2)]
    qd16 = [qd[d].astype(BF16) for d in range(2)]
    state = [st_ref[0], st_ref[1]]
    o_inter = ([None] * nch, [None] * nch)
    for i in range(nch):
        for d in range(2):
            c = nch - 1 - i if d == 1 else i
            in_chunk = col_chunk == c
            o_inter[d][c] = jnp.dot(qd16[d][c * GLA_CHUNK:(c + 1) * GLA_CHUNK], state[d].astype(BF16),
                                    preferred_element_type=F32)
            kv = jnp.dot(jnp.where(in_chunk, kd_t[d], 0.0).astype(BF16), v16[d],
                         preferred_element_type=F32)
            decay = jnp.exp(jnp.sum(jnp.where(in_chunk, la_t[d], 0.0), axis=1, keepdims=True))
            state[d] = decay * state[d] + jnp.where(diag_blocks, kv, 0.0)
    for d in range(2):
        st_ref[d] = state[d]
        dirs[d][4][...] = jnp.concatenate(parts[d], axis=1) + jnp.concatenate(o_inter[d], axis=0)


def _gla_call(g, ymain, wg, bg):
    blk = g.blk
    qw, vw = EV_GLA_W, EV_GLA_V

    def specs(block_fn):
        return [pl.BlockSpec((blk, qw), lambda b, s: (block_fn(b, s), 0)),
                pl.BlockSpec((blk, qw), lambda b, s: (block_fn(b, s), 1)),
                pl.BlockSpec((blk, vw), lambda b, s: (block_fn(b, s), 1)),
                pl.BlockSpec((blk, 128), lambda b, s: (block_fn(b, s), (EV_MAIN - 128) // 128))]

    return pl.pallas_call(
        _gla_kernel,
        grid=(g.b, g.nb + 1),
        in_specs=specs(g.fwd_block) + specs(g.bwd_block) + [
            pl.BlockSpec((2, 128, qw), lambda b, s: (0, 0, 0)),
            pl.BlockSpec((2, 1, qw), lambda b, s: (0, 0, 0))],
        out_specs=[pl.BlockSpec((blk, vw), lambda b, s: (g.fwd_block(b, s), 0)),
                   pl.BlockSpec((blk, vw), lambda b, s: (g.bwd_block(b, s), 0))],
        out_shape=[jax.ShapeDtypeStruct((g.rows, vw), F32)] * 2,
        scratch_shapes=[pltpu.VMEM((2, qw, vw), F32)],
        compiler_params=_cparams(("parallel", "arbitrary")),
        name="gla_scan",
    )(ymain, ymain, ymain, ymain, ymain, ymain, ymain, ymain, wg, bg)


def _part8(x, op):
    rows = [x[i * 8:(i + 1) * 8] for i in range(x.shape[0] // 8)]
    while len(rows) > 1:
        rows = [op(rows[i], rows[i + 1]) for i in range(0, len(rows), 2)]
    return rows[0]


def _attn_kernel(lam_ref, nw_ref, qt_ref, kc_ref, vct_ref, *rest, lam_init, with_latent):
    if with_latent:
        kx_ref, vxt_ref, o_ref, m_sc, acc_sc, nxt_sc = rest
        ki = pl.program_id(3)
        last = pl.num_programs(3) - 1
    else:
        o_ref, m_sc, acc_sc = rest
    qt = qt_ref[...]
    tq = qt.shape[1]
    row = lax.broadcasted_iota(jnp.int32, qt.shape, 0)
    zero = jnp.zeros_like(qt)
    qt_maps = (jnp.where(row < DIFF_DQK, qt, zero), jnp.where(row >= DIFF_DQK, qt, zero))

    def with_ones(vt):
        return jnp.concatenate([vt, jnp.ones((ATTN_ONES, vt.shape[1]), vt.dtype)], axis=0)

    def exact_update(k, vt1):
        for m in range(2):
            st = jnp.dot(k, qt_maps[m], preferred_element_type=F32)
            m_old = m_sc[m]
            m_new = jnp.maximum(m_old, jnp.max(st, axis=0, keepdims=True))
            alpha = jnp.exp2(m_old - m_new)
            p = jnp.exp2(st - m_new)
            acc_sc[m] = alpha * acc_sc[m] + jnp.dot(vt1, p.astype(BF16), preferred_element_type=F32)
            m_sc[m] = m_new

    def lagged_update(k, vt1):
        tk = k.shape[0]
        refs = [m_sc[0], m_sc[1]]
        groups = [(m, gq) for m in range(2) for gq in range(tq // ATTN_QG)]
        cols_of = lambda gq: slice(gq * ATTN_QG, (gq + 1) * ATTN_QG)
        scores = lambda m, gq: jnp.dot(k, qt_maps[m][:, cols_of(gq)], preferred_element_type=F32)
        mx_g = ([], [])
        st = scores(*groups[0])
        for gi, (m, gq) in enumerate(groups):
            cols = cols_of(gq)
            rg = refs[m][:, cols]
            mx = None
            tiles = []
            for c in range(tk // ATTN_KC):
                tile = st[c * ATTN_KC:(c + 1) * ATTN_KC]
                tiles.append(jnp.exp2(tile - rg).astype(BF16))
                tmx = _part8(tile, jnp.maximum)
                mx = tmx if mx is None else jnp.maximum(mx, tmx)
            mx_g[m].append(mx)
            if gi + 1 < len(groups):
                st = scores(*groups[gi + 1])
            pv = jnp.dot(vt1, jnp.concatenate(tiles, axis=0), preferred_element_type=F32)
            nxt_sc[m, :, cols] = acc_sc[m, :, cols] + pv
        bmax = [jnp.max(jnp.concatenate(mx_g[m], axis=1), axis=0, keepdims=True) for m in range(2)]
        lag = jnp.maximum(jnp.max(bmax[0] - refs[0]), jnp.max(bmax[1] - refs[1]))
        safe = lag <= ATTN_LAG_LIMIT

        @pl.when(safe)
        def _():
            for m in range(2):
                r_new = jnp.maximum(refs[m], bmax[m])
                acc_sc[m] = nxt_sc[m] * jnp.exp2(refs[m] - r_new)
                m_sc[m] = r_new

        @pl.when(jnp.logical_not(safe))
        def _():
            exact_update(k, vt1)

    def init_and_ctx():
        m_sc[...] = jnp.full(m_sc.shape, -jnp.inf, F32)
        acc_sc[...] = jnp.zeros(acc_sc.shape, F32)
        exact_update(kc_ref[...], with_ones(vct_ref[...]))

    def finish():
        lp = lam_ref[...]
        lam = (jnp.exp(jnp.sum(lp[0:1] * lp[1:2], keepdims=True))
               - jnp.exp(jnp.sum(lp[2:3] * lp[3:4], keepdims=True)) + lam_init)
        a0, a1 = acc_sc[0], acc_sc[1]
        ot = (a0[:DIFF_DV] / a0[DIFF_DV:DIFF_DV + 1]
              - lam * (a1[:DIFF_DV] / a1[DIFF_DV:DIFF_DV + 1]))
        ms = jnp.mean(ot * ot, axis=0, keepdims=True)
        on = ot * lax.rsqrt(ms + RMS_EPS) * nw_ref[...] * (1.0 - lam_init)
        o_ref[...] = on.T.astype(o_ref.dtype)

    if with_latent:
        pl.when(ki == 0)(init_and_ctx)
        lagged_update(kx_ref[...], with_ones(vxt_ref[...]))
        pl.when(ki == last)(finish)
    else:
        init_and_ctx()
        finish()


def _attn_call(g, k2, qt, vt, lam_p, norm_w_col, lam_init):
    hd = 2 * DIFF_DQK
    tq = min(ATTN_TQ, g.t)
    tk = min(ATTN_TK, g.t)
    assert (g.b * g.c) % tq == 0 and g.t % tq == 0 and g.t % tk == 0 and (g.b * g.c) % tk == 0
    q_off, k_off = g.b * g.c // tq, g.b * g.c // tk
    nq, nk = g.t // tq, g.t // tk
    scratch = lambda nqry: [pltpu.VMEM((2, 1, nqry), F32),
                            pltpu.VMEM((2, DIFF_DV + ATTN_ONES, nqry), F32)]
    common = [pl.BlockSpec((4, DIFF_DQK), lambda *_: (0, 0)),
              pl.BlockSpec((DIFF_DV, 1), lambda *_: (0, 0))]
    od = pl.pallas_call(
        functools.partial(_attn_kernel, lam_init=lam_init, with_latent=True),
        grid=(g.b, DIFF_HEADS, nq, nk),
        in_specs=common + [
            pl.BlockSpec((hd, tq), lambda b, h, qi, ki: (h, q_off + b * nq + qi)),
            pl.BlockSpec((g.c, hd), lambda b, h, qi, ki: (b, h)),
            pl.BlockSpec((DIFF_DV, g.c), lambda b, h, qi, ki: (h, b)),
            pl.BlockSpec((tk, hd), lambda b, h, qi, ki: (k_off + b * nk + ki, h)),
            pl.BlockSpec((DIFF_DV, tk), lambda b, h, qi, ki: (h, k_off + b * nk + ki))],
        out_specs=pl.BlockSpec((tq, DIFF_DV), lambda b, h, qi, ki: (b * nq + qi, h)),
        out_shape=jax.ShapeDtypeStruct((g.b * g.t, DIFF_HEADS * DIFF_DV), BF16),
        scratch_shapes=scratch(tq) + [pltpu.VMEM((2, DIFF_DV + ATTN_ONES, tq), F32)],
        compiler_params=_cparams(("parallel", "parallel", "parallel", "arbitrary")),
        name="diff_attn_latent",
    )(lam_p, norm_w_col, qt, k2, vt, k2, vt)
    return od


def _attn_ctx_call(g, k2, qt, vt, lam_p, norm_w_col, lam_init):
    hd = 2 * DIFF_DQK
    return pl.pallas_call(
        functools.partial(_attn_kernel, lam_init=lam_init, with_latent=False),
        grid=(g.b, DIFF_HEADS),
        in_specs=[pl.BlockSpec((4, DIFF_DQK), lambda *_: (0, 0)),
                  pl.BlockSpec((DIFF_DV, 1), lambda *_: (0, 0)),
                  pl.BlockSpec((hd, g.c), lambda b, h: (h, b)),
                  pl.BlockSpec((g.c, hd), lambda b, h: (b, h)),
                  pl.BlockSpec((DIFF_DV, g.c), lambda b, h: (h, b))],
        out_specs=pl.BlockSpec((g.c, DIFF_DV), lambda b, h: (b, h)),
        out_shape=jax.ShapeDtypeStruct((g.b * g.c, DIFF_HEADS * DIFF_DV), BF16),
        scratch_shapes=[pltpu.VMEM((2, 1, g.c), F32),
                        pltpu.VMEM((2, DIFF_DV + ATTN_ONES, g.c), F32)],
        compiler_params=_cparams(("parallel", "parallel")),
        name="diff_attn_context",
    )(lam_p, norm_w_col, qt, k2, vt)


def _proj_odd_kernel(x_ref, xp_ref, xn_ref, mod_ref, w_ref, cw_ref, cb_ref, z_ref, xbc_ref, dt_ref,
                     seq_ref, *, ctx_blocks, nb, d_inner, conv_ch):
    i = pl.program_id(0)
    j = (i - ctx_blocks) % nb
    is_lat = i >= ctx_blocks
    has_prev = jnp.logical_and(is_lat, j > 0).astype(F32)
    has_next = jnp.logical_and(is_lat, j < nb - 1).astype(F32)
    scale = 1.0 + mod_ref[1:2, :]
    shift = mod_ref[0:1, :]
    rows = jnp.concatenate([x_ref[...], xp_ref[...], xn_ref[...]], axis=0)
    h = (rows * scale + shift).astype(BF16)
    n = x_ref.shape[0]
    cw = PROJ_CHUNK
    other = [(z_ref, c0, c0) for c0 in range(0, d_inner, cw)]
    other += [(dt_ref, c0, d_inner + conv_ch + c0) for c0 in range(0, dt_ref.shape[1], cw)]
    pad = SSD_CONV // 2
    n_chunks = conv_ch // cw
    assert len(other) <= n_chunks
    conv_cols = lambda jc: jnp.dot(h, w_ref[:, d_inner + jc * cw:d_inner + (jc + 1) * cw],
                                   preferred_element_type=F32)
    y = conv_cols(0)
    for jc in range(n_chunks):
        cols = slice(jc * cw, (jc + 1) * cw)
        seq_ref[jc, 0:8, :] = y[n:n + 8] * has_prev
        seq_ref[jc, 8:8 + n, :] = y[:n]
        seq_ref[jc, 8 + n:16 + n, :] = y[n + 8:n + 16] * has_next
        if jc + 1 < n_chunks:
            y = conv_cols(jc + 1)
        if jc < len(other):
            o_ref, oc, wc = other[jc]
            o_ref[:, oc:oc + cw] = jnp.dot(h[:n], w_ref[:, wc:wc + cw], preferred_element_type=F32)
        w = cw_ref[:, cols]
        acc = cb_ref[:, cols] + seq_ref[jc, 8:8 + n, :] * w[pad:pad + 1]
        for tap in range(SSD_CONV):
            if tap != pad:
                acc = acc + seq_ref[jc, pl.ds(8 - pad + tap, n), :] * w[tap:tap + 1]
        xbc_ref[:, cols] = _silu(acc)


def _proj_odd_call(g, s_rows, mod_l, w, conv_w8, conv_b, d_inner, conv_ch):
    blk = g.blk
    assert g.tm % blk == 0
    per_tile = g.tm // blk
    n = w.shape[1]
    dt_w = n - d_inner - conv_ch
    nrow8 = g.rows // 8
    per = blk // 8
    row = lambda i: (i, 0)
    return pl.pallas_call(
        functools.partial(_proj_odd_kernel, ctx_blocks=g.ctx_blocks, nb=g.nb, d_inner=d_inner,
                          conv_ch=conv_ch),
        grid=(g.rows // blk,),
        in_specs=[pl.BlockSpec((blk, g.d), row),
                  pl.BlockSpec((8, g.d), lambda i: (jnp.maximum(i * per - 1, 0), 0)),
                  pl.BlockSpec((8, g.d), lambda i: (jnp.minimum((i + 1) * per, nrow8 - 1), 0)),
                  pl.BlockSpec((None, 6, g.d), lambda i: (g.mod_row(i // per_tile), 0, 0)),
                  _resident((g.d, n)),
                  pl.BlockSpec((8, conv_ch), lambda i: (0, 0)),
                  pl.BlockSpec((1, conv_ch), lambda i: (0, 0))],
        out_specs=[pl.BlockSpec((blk, d_inner), row),
                   pl.BlockSpec((blk, conv_ch), row),
                   pl.BlockSpec((blk, dt_w), row)],
        out_shape=[jax.ShapeDtypeStruct((g.rows, d_inner), F32),
                   jax.ShapeDtypeStruct((g.rows, conv_ch), F32),
                   jax.ShapeDtypeStruct((g.rows, dt_w), F32)],
        scratch_shapes=[pltpu.VMEM((conv_ch // PROJ_CHUNK, blk + 16, PROJ_CHUNK), F32)],
        compiler_params=_cparams(("parallel",)),
        name="proj_odd_conv",
    )(s_rows, s_rows, s_rows, mod_l, w, conv_w8, conv_b)


def _ssd_kernel(xf, bmf, cmf, dtf, xb, bmb, cmb, dtb, bias_ref, alog_ref, dsk_ref, yf_ref, yb_ref,
                st_ref):
    s = pl.program_id(2)

    @pl.when(s == 0)
    def _():
        st_ref[...] = jnp.zeros_like(st_ref)

    n, ch = MIX_BLOCK, SSD_CHUNK
    nch = n // ch
    gw = SSD_HEADS_PER_GROUP * SSD_HEAD_DIM
    ri = lax.broadcasted_iota(jnp.int32, (n, n), 0)
    ci = lax.broadcasted_iota(jnp.int32, (n, n), 1)
    same_chunk = (ri // ch) == (ci // ch)
    ri_c = lax.broadcasted_iota(jnp.int32, (ch, ch), 0)
    ci_c = lax.broadcasted_iota(jnp.int32, (ch, ch), 1)
    lane_c = lax.broadcasted_iota(jnp.int32, (ch, 128), 1)
    a_neg = -jnp.exp(alog_ref[...])
    exp_row = lax.broadcasted_iota(jnp.int32, (128, gw), 0)
    exp_col = lax.broadcasted_iota(jnp.int32, (128, gw), 1) // SSD_HEAD_DIM

    dirs = ((xf, bmf, cmf, dtf, yf_ref), (xb, bmb, cmb, dtb, yb_ref))
    dt, acs, acs_t, dt_t, expand, tri = [], [], [], [], [], []
    for d in range(2):
        lane0 = SSD_HEADS_PER_GROUP * d
        expand.append(jnp.where(exp_row == exp_col + lane0, 1.0, 0.0).astype(BF16))
        z = dirs[d][3][...] + bias_ref[...]
        dt.append(jnp.maximum(z, 0.0) + jnp.log1p(jnp.exp(-jnp.abs(z))))
        causal_blk = jnp.logical_and(same_chunk, (ci >= ri) if d == 1 else (ci <= ri))
        acs.append(_split_dot(jnp.where(causal_blk, 1.0, 0.0).astype(BF16), dt[d] * a_neg))
        tri.append((ci_c >= ri_c) if d == 1 else (ci_c <= ri_c))
    for d in range(2):
        acs_t.append(acs[d].T)
        dt_t.append(dt[d].T)
    state = [st_ref[0], st_ref[1]]

    for i in range(nch):
        for d in range(2):
            x_ref, bm_ref, cm_ref, _, y_ref = dirs[d]
            rev = d == 1
            lane0 = SSD_HEADS_PER_GROUP * d
            c = nch - 1 - i if rev else i
            r0, r1 = c * ch, (c + 1) * ch
            xc = x_ref[r0:r1, :]
            bmc = bm_ref[r0:r1, :]
            acs_c = acs[d][r0:r1, :]
            acs_tc = acs_t[d][:, r0:r1]
            dt_tc = dt_t[d][:, r0:r1]
            cm16 = cm_ref[r0:r1, :].astype(BF16)
            cb = lax.dot_general(cm16, bmc.astype(BF16), (((1,), (1,)), ((), ())),
                                 preferred_element_type=F32)
            a_last = acs_c[0:1, :] if rev else acs_c[ch - 1:ch, :]
            fac = jnp.concatenate([jnp.exp(acs_c), dt[d][r0:r1, :] * jnp.exp(a_last - acs_c)], axis=0)
            fac = jnp.dot(fac.astype(BF16), expand[d], preferred_element_type=F32)
            dec = _split_dot_r(jnp.broadcast_to(jnp.exp(a_last), (8, 128)), expand[d])[0:1]
            pairs = []
            for pp in range(SSD_HEADS_PER_GROUP // 2):
                mats = []
                for e in (2 * pp, 2 * pp + 1):
                    ln = lane0 + e
                    seg = jnp.exp(jnp.where(tri[d], acs_c[:, ln:ln + 1] - acs_tc[ln:ln + 1, :], -jnp.inf))
                    mats.append((cb * seg * dt_tc[ln:ln + 1, :]).astype(BF16))
                xp = xc[:, 128 * pp:128 * (pp + 1)]
                x2 = jnp.concatenate([jnp.where(lane_c < SSD_HEAD_DIM, xp, 0.0),
                                      jnp.where(lane_c >= SSD_HEAD_DIM, xp, 0.0)], axis=0).astype(BF16)
                pairs.append(jnp.dot(jnp.concatenate(mats, axis=1), x2, preferred_element_type=F32))
            y_off = jnp.dot(cm16, state[d].astype(BF16), preferred_element_type=F32) * fac[0:ch]
            y = jnp.concatenate(pairs, axis=1) + y_off
            if not rev:
                y = y + dsk_ref[...] * xc
            y_ref[r0:r1, :] = y
            upd = jnp.dot(bmc.T.astype(BF16), (xc * fac[ch:2 * ch]).astype(BF16),
                          preferred_element_type=F32)
            state[d] = dec * state[d] + upd
    st_ref[0] = state[0]
    st_ref[1] = state[1]


def _ssd_call(g, xbc, y_odd, dt_bias_slab, a_log_slab, d_skip, d_inner):
    blk = g.blk
    gw = SSD_HEADS_PER_GROUP * SSD_HEAD_DIM
    bm0 = d_inner // 128
    cm0 = bm0 + SSD_GROUPS
    dt0 = 0

    def specs(block_fn):
        return [pl.BlockSpec((blk, gw), lambda b, gi, s: (block_fn(b, s), gi)),
                pl.BlockSpec((blk, 128), lambda b, gi, s: (block_fn(b, s), bm0 + gi)),
                pl.BlockSpec((blk, 128), lambda b, gi, s: (block_fn(b, s), cm0 + gi)),
                pl.BlockSpec((blk, 128), lambda b, gi, s: (block_fn(b, s), dt0 + gi))]

    return pl.pallas_call(
        _ssd_kernel,
        grid=(g.b, SSD_GROUPS, g.nb + 1),
        in_specs=specs(g.fwd_block) + specs(g.bwd_block) + [
            pl.BlockSpec((None, 1, 128), lambda b, gi, s: (gi, 0, 0)),
            pl.BlockSpec((None, 1, 128), lambda b, gi, s: (gi, 0, 0)),
            pl.BlockSpec((None, 1, gw), lambda b, gi, s: (gi, 0, 0))],
        out_specs=[pl.BlockSpec((blk, gw), lambda b, gi, s: (g.fwd_block(b, s), gi)),
                   pl.BlockSpec((blk, gw), lambda b, gi, s: (g.bwd_block(b, s), gi))],
        out_shape=[jax.ShapeDtypeStruct((g.rows, d_inner), F32)] * 2,
        scratch_shapes=[pltpu.VMEM((2, SSD_STATE, gw), F32)],
        compiler_params=_cparams(("parallel", "parallel", "arbitrary")),
        name="ssd_scan",
    )(xbc, xbc, xbc, y_odd, xbc, xbc, xbc, y_odd, dt_bias_slab, a_log_slab,
      d_skip.reshape(SSD_GROUPS, 1, gw))


def _ffn_residual(x, mod_ref, win_ref, wout_ref, ln_ref, alpha, hidden):
    h = (x * (1.0 + mod_ref[4:5, :]) + mod_ref[3:4, :]).astype(BF16)
    step = -(-hidden // (FFN_CHUNKS * MXU_TILE)) * MXU_TILE
    bounds = [(c0, min(c0 + step, hidden)) for c0 in range(0, hidden, step)]

    def gate_up(c0, c1):
        return (jnp.dot(h, win_ref[:, c0:c1], preferred_element_type=F32),
                jnp.dot(h, win_ref[:, hidden + c0:hidden + c1], preferred_element_type=F32))

    acc = None
    gate, up = gate_up(*bounds[0])
    for j, (c0, c1) in enumerate(bounds):
        nxt = gate_up(*bounds[j + 1]) if j + 1 < len(bounds) else None
        act = (_silu(gate) * up).astype(BF16)
        part = jnp.dot(act, wout_ref[c0:c1, :], preferred_element_type=F32)
        acc = part if acc is None else acc + part
        if nxt is not None:
            gate, up = nxt
    v = alpha * x + mod_ref[5:6, :] * acc
    return _layer_norm(v, ln_ref[2:3, :], ln_ref[3:4, :])


def _post_even_kernel(x_ref, mod_ref, of_ref, ob_ref, g_ref, odc_ref, odl_ref, nw_ref, w_ref, ln_ref,
                      win_ref, wout_ref, o_ref, *, alpha, ctx_tiles, off, hidden):
    o = of_ref[...] + ob_ref[...]
    gate = g_ref[...]
    nw = nw_ref[...]
    parts = []
    for h in range(GLA_HEADS):
        oh = o[:, h * GLA_DV:(h + 1) * GLA_DV]
        ms = jnp.mean(oh * oh, axis=-1, keepdims=True)
        parts.append(oh * lax.rsqrt(ms + RMS_EPS) * nw)
    gla = (jnp.concatenate(parts, axis=1) * _silu(gate)).astype(BF16)
    od = jnp.where(pl.program_id(0) + off < ctx_tiles, odc_ref[...], odl_ref[...])
    mixin = jnp.concatenate([gla, od], axis=1)
    mix = jnp.dot(mixin, w_ref[...], preferred_element_type=F32)
    v = alpha * x_ref[...] + mod_ref[2:3, :] * mix
    x1 = _layer_norm(v, ln_ref[0:1, :], ln_ref[1:2, :])
    o_ref[...] = _ffn_residual(x1, mod_ref, win_ref, wout_ref, ln_ref, alpha, hidden)


def _post_even_call(g, s_rows, mod_l, o_f, o_b, ymain, od_ctx, od_lat, gla_nw, w_mix, ln4, w_in, w_out,
                    alpha, latent_only):
    tm, d = g.tm, g.d
    vw = EV_GLA_V
    off = g.ctx_tiles if latent_only else 0
    n_tiles = g.n_tiles - off
    row = lambda i: (i + off, 0)
    return pl.pallas_call(
        functools.partial(_post_even_kernel, alpha=alpha, ctx_tiles=g.ctx_tiles, off=off,
                          hidden=w_out.shape[0]),
        grid=(n_tiles,),
        in_specs=[pl.BlockSpec((tm, d), row),
                  pl.BlockSpec((None, 6, d), lambda i: (g.mod_row(i + off), 0, 0)),
                  pl.BlockSpec((tm, vw), row),
                  pl.BlockSpec((tm, vw), row),
                  pl.BlockSpec((tm, vw), lambda i: (i + off, 2 * EV_GLA_W // vw + 1)),
                  pl.BlockSpec((tm, vw), lambda i: (jnp.minimum(i + off, g.ctx_tiles - 1), 0)),
                  pl.BlockSpec((tm, vw), lambda i: (jnp.maximum(i + off - g.ctx_tiles, 0), 0)),
                  pl.BlockSpec((1, GLA_DV), lambda i: (0, 0)),
                  _resident(w_mix.shape),
                  pl.BlockSpec((4, d), lambda i: (0, 0)),
                  _resident(w_in.shape),
                  _resident(w_out.shape)],
        out_specs=pl.BlockSpec((tm, d), lambda i: (i, 0)),
        out_shape=jax.ShapeDtypeStruct((n_tiles * tm, d), F32),
        compiler_params=_cparams(("parallel",)),
        name="post_even",
    )(s_rows, mod_l, o_f, o_b, ymain, od_ctx, od_lat, gla_nw, w_mix, ln4, w_in, w_out)


def _post_odd_kernel(x_ref, mod_ref, yf_ref, yb_ref, z_ref, nw_ref, w_ref, ln_ref, win_ref, wout_ref,
                     o_ref, *, alpha, hidden):
    y = (yf_ref[...] + yb_ref[...]) * _silu(z_ref[...])
    gw = y.shape[1] // SSD_GROUPS
    parts = []
    for gi in range(SSD_GROUPS):
        yg = y[:, gi * gw:(gi + 1) * gw]
        ms = jnp.mean(yg * yg, axis=-1, keepdims=True)
        parts.append(yg * lax.rsqrt(ms + RMS_EPS))
    yn = (jnp.concatenate(parts, axis=1) * nw_ref[...]).astype(BF16)
    mix = jnp.dot(yn, w_ref[...], preferred_element_type=F32)
    v = alpha * x_ref[...] + mod_ref[2:3, :] * mix
    x1 = _layer_norm(v, ln_ref[0:1, :], ln_ref[1:2, :])
    o_ref[...] = _ffn_residual(x1, mod_ref, win_ref, wout_ref, ln_ref, alpha, hidden)


def _post_odd_call(g, s_rows, mod_l, y_f, y_b, z_gate, norm_w, w_mix, ln4, w_in, w_out, alpha,
                   latent_only):
    tm = g.tm // 2
    per_tile = g.tm // tm
    d = g.d
    di = w_mix.shape[0]
    off = g.ctx_tiles * per_tile if latent_only else 0
    n_tiles = g.rows // tm - off
    row = lambda i: (i + off, 0)
    return pl.pallas_call(
        functools.partial(_post_odd_kernel, alpha=alpha, hidden=w_out.shape[0]),
        grid=(n_tiles,),
        in_specs=[pl.BlockSpec((tm, d), row),
                  pl.BlockSpec((None, 6, d), lambda i: (g.mod_row((i + off) // per_tile), 0, 0)),
                  pl.BlockSpec((tm, di), row),
                  pl.BlockSpec((tm, di), row),
                  pl.BlockSpec((tm, di), row),
                  pl.BlockSpec((1, di), lambda i: (0, 0)),
                  _resident(w_mix.shape),
                  pl.BlockSpec((4, d), lambda i: (0, 0)),
                  _resident(w_in.shape),
                  _resident(w_out.shape)],
        out_specs=pl.BlockSpec((tm, d), lambda i: (i, 0)),
        out_shape=jax.ShapeDtypeStruct((n_tiles * tm, d), F32),
        compiler_params=_cparams(("parallel",)),
        name="post_odd",
    )(s_rows, mod_l, y_f, y_b, z_gate, norm_w, w_mix, ln4, w_in, w_out)


def kernel(x, c, ctx, c_ctx, mod_w, mod_b, ln_g, ln_b, ffn_w_in, ffn_w_out, ev_w_in, ev_w_out,
           gla_w_gate2, gla_b_gate, gla_norm_w, diff_lambda, diff_norm_w, ssd_w_in, ssd_conv_w,
           ssd_conv_b, ssd_dt_bias, ssd_a_log, ssd_d, ssd_norm_w, ssd_w_out):
    batch, seq, d = x.shape
    ctx_len = ctx.shape[1]
    depth = mod_w.shape[0]
    g = _Geom(batch, seq, ctx_len, d)
    alpha = (2 * depth) ** 0.25
    d_inner = ssd_w_out.shape[1]
    conv_ch = ssd_conv_w.shape[2]
    n_heads = ssd_d.shape[1]

    s_rows = jnp.concatenate([ctx.reshape(batch * ctx_len, d), x.reshape(batch * seq, d)], axis=0)
    cc = jnp.zeros((8, d), F32).at[:batch].set(c).at[batch].set(c_ctx)
    mod_all = _mod_call(cc, mod_w, mod_b).reshape(depth, 8, 6, d)
    rope_tab = _rope_table(g)

    for layer in range(depth):
        need_ctx = layer < depth - 1
        mod_l = mod_all[layer]
        ln4 = jnp.stack([ln_g[layer, 0], ln_b[layer, 0], ln_g[layer, 1], ln_b[layer, 1]])
        w_ffn_in = ffn_w_in[layer].astype(BF16)
        w_ffn_out = ffn_w_out[layer].astype(BF16)
        if layer % 2 == 0:
            e = layer // 2
            w = ev_w_in[e]
            w_my = jnp.concatenate([w[:, :EV_REAL_MAIN], jnp.zeros((d, 128 - 2 * GLA_RANK), w.dtype),
                                    w[:, EV_REAL_MAIN:]], axis=1).astype(BF16)
            ymain, k2, qt, vt = _proj_even_call(g, s_rows, mod_l, w_my, rope_tab)
            wg = jnp.zeros((2, 128, EV_GLA_W), F32)
            wg = wg.at[0, :GLA_RANK].set(gla_w_gate2[e, 0]).at[1, GLA_RANK:2 * GLA_RANK].set(gla_w_gate2[e, 1])
            o_f, o_b = _gla_call(g, ymain, wg.astype(BF16), gla_b_gate[e].reshape(2, 1, EV_GLA_W))
            lam_init = 0.8 - 0.6 * math.exp(-0.3 * layer)
            nw_col = diff_norm_w[e].reshape(DIFF_DV, 1)
            od_lat = _attn_call(g, k2, qt, vt, diff_lambda[e], nw_col, lam_init)
            od_ctx = _attn_ctx_call(g, k2, qt, vt, diff_lambda[e], nw_col, lam_init)
            s_rows = _post_even_call(g, s_rows, mod_l, o_f, o_b, ymain, od_ctx, od_lat,
                                     gla_norm_w[e].reshape(1, GLA_DV), ev_w_out[e].astype(BF16), ln4,
                                     w_ffn_in, w_ffn_out, alpha, latent_only=not need_ctx)
        else:
            o = layer // 2
            w = ssd_w_in[o]
            dt_col0 = d_inner + conv_ch
            hpg = SSD_HEADS_PER_GROUP
            slabs = []
            for gi in range(SSD_GROUPS):
                slabs += [w[:, dt_col0 + hpg * gi:dt_col0 + hpg * (gi + 1)],
                          w[:, dt_col0 + n_heads + hpg * gi:dt_col0 + n_heads + hpg * (gi + 1)],
                          jnp.zeros((d, 128 - 2 * hpg), w.dtype)]
            w_my = jnp.concatenate([w[:, :dt_col0]] + slabs, axis=1).astype(BF16)

            def slab_vec(v2):
                rows = [jnp.concatenate([v2[0, hpg * gi:hpg * (gi + 1)], v2[1, hpg * gi:hpg * (gi + 1)],
                                         jnp.zeros((128 - 2 * hpg,), F32)]) for gi in range(SSD_GROUPS)]
                return jnp.stack(rows).reshape(SSD_GROUPS, 1, 128)

            conv_w8 = jnp.concatenate([ssd_conv_w[o], jnp.zeros((8 - SSD_CONV, conv_ch), F32)], axis=0)
            z_gate, xbc, dt_raw = _proj_odd_call(g, s_rows, mod_l, w_my, conv_w8,
                                                 ssd_conv_b[o].reshape(1, conv_ch), d_inner, conv_ch)
            d_skip = jnp.repeat(ssd_d[o], SSD_HEAD_DIM)
            y_f, y_b = _ssd_call(g, xbc, dt_raw, slab_vec(ssd_dt_bias[o]), slab_vec(ssd_a_log[o]),
                                 d_skip, d_inner)
            s_rows = _post_odd_call(g, s_rows, mod_l, y_f, y_b, z_gate, ssd_norm_w[o].reshape(1, d_inner),
                                    ssd_w_out[o].astype(BF16), ln4, w_ffn_in, w_ffn_out, alpha,
                                    latent_only=not need_ctx)
    return s_rows.reshape(batch, seq, d)
```

```python
import functools
import math

import jax
import jax.numpy as jnp
from jax import lax
from jax.experimental import pallas as pl
from jax.experimental.pallas import tpu as pltpu

F32 = jnp.float32
BF16 = jnp.bfloat16

GRID_W = 64
GLA_HEADS, GLA_DK, GLA_DV, GLA_RANK, GLA_CHUNK = 4, 64, 128, 16, 64
GLA_INV_TAU = 1.0 / 16.0
DIFF_HEADS, DIFF_DQK, DIFF_DV = 4, 64, 128
ROPE_BASE = 10000.0
SSD_HEAD_DIM, SSD_GROUPS, SSD_STATE, SSD_CONV, SSD_CHUNK = 64, 4, 128, 5, 128
SSD_HEADS_PER_GROUP = 8
LN_EPS = 1e-6
RMS_EPS = 1e-6

EV_GLA_W = GLA_HEADS * GLA_DK
EV_GLA_V = GLA_HEADS * GLA_DV
EV_MAIN = 2 * EV_GLA_W + 2 * EV_GLA_V + 128
EV_DIFF = DIFF_HEADS * 2 * DIFF_DQK
EV_REAL_MAIN = 2 * EV_GLA_W + 2 * EV_GLA_V + 2 * GLA_RANK

ROW_TILE = 512
MIX_BLOCK = 256
ATTN_TQ = 2048
ATTN_TK = 2048
ATTN_QG = 256
ATTN_KC = 128
ATTN_LAG_LIMIT = 50.0
ATTN_ONES = 16
LOG2E = 1.4426950408889634
PROJ_CHUNK = 512
MXU_TILE = 256
FFN_CHUNKS = 4
VMEM_LIMIT = 56 * 1024 * 1024


def _cparams(sem):
    return pltpu.CompilerParams(dimension_semantics=sem, vmem_limit_bytes=VMEM_LIMIT)


def _resident(shape):
    nd = len(shape)
    return pl.BlockSpec(shape, lambda *_: (0,) * nd, pipeline_mode=pl.Buffered(1))


def _silu(v):
    return v / (1.0 + jnp.exp(-v))


def _layer_norm(v, g, b):
    mu = jnp.mean(v, axis=-1, keepdims=True)
    d = v - mu
    var = jnp.mean(d * d, axis=-1, keepdims=True)
    return d * lax.rsqrt(var + LN_EPS) * g + b


def _split_dot(mat_bf16, v):
    hi = v.astype(BF16)
    lo = (v - hi.astype(F32)).astype(BF16)
    return (jnp.dot(mat_bf16, hi, preferred_element_type=F32)
            + jnp.dot(mat_bf16, lo, preferred_element_type=F32))


def _split_dot_r(v, mat_bf16):
    hi = v.astype(BF16)
    lo = (v - hi.astype(F32)).astype(BF16)
    return (jnp.dot(hi, mat_bf16, preferred_element_type=F32)
            + jnp.dot(lo, mat_bf16, preferred_element_type=F32))


def _mod_kernel(c_ref, w_ref, b_ref, o_ref):
    s = _silu(c_ref[...]).astype(BF16)
    o_ref[...] = jnp.dot(s, w_ref[...].astype(BF16), preferred_element_type=F32) + b_ref[...]


def _mod_call(cc, mod_w, mod_b):
    depth, d, n = mod_w.shape
    cw = d
    return pl.pallas_call(
        _mod_kernel,
        grid=(depth, n // cw),
        in_specs=[pl.BlockSpec((8, d), lambda l, j: (0, 0)),
                  pl.BlockSpec((None, d, cw), lambda l, j: (l, 0, j)),
                  pl.BlockSpec((None, 1, cw), lambda l, j: (l, 0, j))],
        out_specs=pl.BlockSpec((None, 8, cw), lambda l, j: (l, 0, j)),
        out_shape=jax.ShapeDtypeStruct((depth, 8, n), F32),
        compiler_params=_cparams(("parallel", "parallel")),
        name="mod_vectors",
    )(cc, mod_w, mod_b.reshape(depth, 1, n))


class _Geom:
    def __init__(self, batch, seq, ctx_len, d_model):
        self.b, self.t, self.c, self.d = batch, seq, ctx_len, d_model
        self.rows = batch * (seq + ctx_len)
        self.tm = min(ROW_TILE, batch * ctx_len)
        assert (batch * ctx_len) % self.tm == 0 and seq % self.tm == 0
        self.tiles_per_batch = seq // self.tm
        self.n_tiles = self.rows // self.tm
        self.lat_tiles = batch * seq // self.tm
        self.blk = MIX_BLOCK
        assert ctx_len == self.blk and seq % self.blk == 0
        self.nb = seq // self.blk
        self.lat_blocks = batch * self.nb

    def mod_row(self, tile):
        return jnp.where(tile < self.lat_tiles, tile // self.tiles_per_batch, self.b)

    def fwd_block(self, b, s):
        return jnp.where(s == 0, self.lat_blocks + b, self.nb * b + s - 1)

    def bwd_block(self, b, s):
        return jnp.where(s == 0, self.lat_blocks + b, self.nb * b + self.nb - s)


def _proj_even_kernel(x_ref, mod_ref, w_ref, rope_ref, ymain_ref, k2_ref, qt_ref, vt_ref):
    x = x_ref[...]
    h = (x * (1.0 + mod_ref[1:2, :]) + mod_ref[0:1, :]).astype(BF16)
    y = jnp.dot(h, w_ref[...], preferred_element_type=F32)
    ymain_ref[...] = y[:, :EV_MAIN]
    cos = rope_ref[:, 0:128]
    sin_up = rope_ref[:, 128:256]
    sin_dn = rope_ref[:, 256:384]

    def rope(t):
        return t * cos + pltpu.roll(t, 112, 1) * sin_up + pltpu.roll(t, 16, 1) * sin_dn

    for j in range(DIFF_HEADS):
        lo, hi = 128 * j, 128 * (j + 1)
        q = rope(y[:, EV_MAIN + lo:EV_MAIN + hi]) * (DIFF_DQK ** -0.5 * LOG2E)
        qt_ref[lo:hi, :] = q.T.astype(BF16)
        k = rope(y[:, EV_MAIN + EV_DIFF + lo:EV_MAIN + EV_DIFF + hi])
        k2_ref[:, lo:hi] = k.astype(BF16)
        v = y[:, EV_MAIN + 2 * EV_DIFF + lo:EV_MAIN + 2 * EV_DIFF + hi]
        vt_ref[lo:hi, :] = v.T.astype(BF16)


def _proj_even_call(g, s_rows, mod_l, w, rope_tab):
    tm, d = g.tm, g.d
    n = w.shape[1]
    def rope_idx(i):
        return jnp.where(i < g.lat_tiles, 1 + i % g.tiles_per_batch, 0)

    return pl.pallas_call(
        _proj_even_kernel,
        grid=(g.n_tiles,),
        in_specs=[pl.BlockSpec((tm, d), lambda i: (i, 0)),
                  pl.BlockSpec((None, 6, d), lambda i: (g.mod_row(i), 0, 0)),
                  _resident((d, n)),
                  pl.BlockSpec((tm, 384), lambda i: (rope_idx(i), 0))],
        out_specs=[pl.BlockSpec((tm, EV_MAIN), lambda i: (i, 0)),
                   pl.BlockSpec((tm, EV_DIFF), lambda i: (i, 0)),
                   pl.BlockSpec((EV_DIFF, tm), lambda i: (0, i)),
                   pl.BlockSpec((EV_DIFF, tm), lambda i: (0, i))],
        out_shape=[jax.ShapeDtypeStruct((g.rows, EV_MAIN), F32),
                   jax.ShapeDtypeStruct((g.rows, EV_DIFF), BF16),
                   jax.ShapeDtypeStruct((EV_DIFF, g.rows), BF16),
                   jax.ShapeDtypeStruct((EV_DIFF, g.rows), BF16)],
        compiler_params=_cparams(("parallel",)),
        name="proj_even",
    )(s_rows, mod_l, w, rope_tab)


def _rope_table(g):
    rows = g.t // GRID_W
    n_freq = DIFF_DQK // 4
    inv = ROPE_BASE ** (-jnp.arange(n_freq, dtype=F32) / n_freq)
    ang_r = jnp.arange(rows, dtype=F32)[:, None] * inv
    ang_c = jnp.arange(GRID_W, dtype=F32)[:, None] * inv
    per_row = lambda tab: jnp.repeat(tab, GRID_W, axis=0)
    per_col = lambda tab: jnp.tile(tab, (rows, 1))
    cr, sr = per_row(jnp.cos(ang_r)), per_row(jnp.sin(ang_r))
    cc, sc = per_col(jnp.cos(ang_c)), per_col(jnp.sin(ang_c))
    z = jnp.zeros_like(cr)
    cos64 = jnp.concatenate([cr, cr, cc, cc], axis=-1)
    up64 = jnp.concatenate([-sr, z, -sc, z], axis=-1)
    dn64 = jnp.concatenate([z, sr, z, sc], axis=-1)
    tab = jnp.concatenate([cos64, cos64, up64, up64, dn64, dn64], axis=-1)
    ident = jnp.concatenate([jnp.ones((g.tm, 128), F32), jnp.zeros((g.tm, 256), F32)], axis=-1)
    return jnp.concatenate([ident, tab], axis=0)


def _gla_kernel(qf, kf, vf, rf, qb, kb, vb, rb, wg_ref, bg_ref, of_ref, ob_ref, st_ref):
    s = pl.program_id(1)

    @pl.when(s == 0)
    def _():
        st_ref[...] = jnp.zeros_like(st_ref)

    n = MIX_BLOCK
    nch = n // GLA_CHUNK
    ri = lax.broadcasted_iota(jnp.int32, (n, n), 0)
    ci = lax.broadcasted_iota(jnp.int32, (n, n), 1)
    same_chunk = (ri // GLA_CHUNK) == (ci // GLA_CHUNK)
    col_chunk = ci // GLA_CHUNK
    lane_head = lax.broadcasted_iota(jnp.int32, (1, EV_GLA_W), 1) // GLA_DK
    diag_blocks = ((lax.broadcasted_iota(jnp.int32, (EV_GLA_W, EV_GLA_V), 0) // GLA_DK)
                   == (lax.broadcasted_iota(jnp.int32, (EV_GLA_W, EV_GLA_V), 1) // GLA_DV))

    dirs = ((qf, kf, vf, rf, of_ref), (qb, kb, vb, rb, ob_ref))
    causal, la, qd, ki, kd, v16 = [], [], [], [], [], []
    for d, (q_ref, k_ref, v_ref, r_ref, _) in enumerate(dirs):
        causal.append(jnp.logical_and(same_chunk, (ci >= ri) if d == 1 else (ci <= ri)))
        z = jnp.dot(r_ref[...].astype(BF16), wg_ref[d], preferred_element_type=F32) + bg_ref[d]
        la.append((jnp.minimum(z, 0.0) - jnp.log1p(jnp.exp(-jnp.abs(z)))) * GLA_INV_TAU)
    for d, (q_ref, k_ref, v_ref, r_ref, _) in enumerate(dirs):
        cum = _split_dot(jnp.where(causal[d], 1.0, 0.0).astype(BF16), la[d])
        tot = jnp.concatenate(
            [jnp.broadcast_to(jnp.sum(la[d][c * GLA_CHUNK:(c + 1) * GLA_CHUNK], axis=0, keepdims=True),
                              (GLA_CHUNK, EV_GLA_W)) for c in range(nch)], axis=0)
        k = k_ref[...]
        v16.append(v_ref[...].astype(BF16))
        qd.append(q_ref[...] * jnp.exp(cum) * (GLA_DK ** -0.5))
        ki.append((k * jnp.exp(-cum)).astype(BF16))
        kd.append(k * jnp.exp(tot - cum))

    parts = ([], [])
    for h in range(GLA_HEADS):
        for d in range(2):
            qh = jnp.where(lane_head == h, qd[d], 0.0).astype(BF16)
            att = lax.dot_general(qh, ki[d], (((1,), (1,)), ((), ())), preferred_element_type=F32)
            att = jnp.where(causal[d], att, 0.0).astype(BF16)
            parts[d].append(jnp.dot(att, v16[d][:, h * GLA_DV:(h + 1) * GLA_DV],
                                    preferred_element_type=F32))

    la_t = [la[d].T for d in range(2)]
    kd_t = [kd[d].T for d in range(2)]
    qd16 = [qd[d].astype(BF16) for d in range(2)]
    state = [st_ref[0], st_ref[1]]
    o_inter = ([None] * nch, [None] * nch)
    for i in range(nch):
        for d in range(2):
            c = nch - 1 - i if d == 1 else i
            in_chunk = col_chunk == c
            o_inter[d][c] = jnp.dot(qd16[d][c * GLA_CHUNK:(c + 1) * GLA_CHUNK], state[d].astype(BF16),
                                    preferred_element_type=F32)
            kv = jnp.dot(jnp.where(in_chunk, kd_t[d], 0.0).astype(BF16), v16[d],
                         preferred_element_type=F32)
            decay = jnp.exp(jnp.sum(jnp.where(in_chunk, la_t[d], 0.0), axis=1, keepdims=True))
            state[d] = decay * state[d] + jnp.where(diag_blocks, kv, 0.0)
    for d in range(2):
        st_ref[d] = state[d]
        dirs[d][4][...] = jnp.concatenate(parts[d], axis=1) + jnp.concatenate(o_inter[d], axis=0)


def _gla_call(g, ymain, wg, bg):
    blk = g.blk
    qw, vw = EV_GLA_W, EV_GLA_V

    def specs(block_fn):
        return [pl.BlockSpec((blk, qw), lambda b, s: (block_fn(b, s), 0)),
                pl.BlockSpec((blk, qw), lambda b, s: (block_fn(b, s), 1)),
                pl.BlockSpec((blk, vw), lambda b, s: (block_fn(b, s), 1)),
                pl.BlockSpec((blk, 128), lambda b, s: (block_fn(b, s), (EV_MAIN - 128) // 128))]

    return pl.pallas_call(
        _gla_kernel,
        grid=(g.b, g.nb + 1),
        in_specs=specs(g.fwd_block) + specs(g.bwd_block) + [
            pl.BlockSpec((2, 128, qw), lambda b, s: (0, 0, 0)),
            pl.BlockSpec((2, 1, qw), lambda b, s: (0, 0, 0))],
        out_specs=[pl.BlockSpec((blk, vw), lambda b, s: (g.fwd_block(b, s), 0)),
                   pl.BlockSpec((blk, vw), lambda b, s: (g.bwd_block(b, s), 0))],
        out_shape=[jax.ShapeDtypeStruct((g.rows, vw), F32)] * 2,
        scratch_shapes=[pltpu.VMEM((2, qw, vw), F32)],
        compiler_params=_cparams(("parallel", "arbitrary")),
        name="gla_scan",
    )(ymain, ymain, ymain, ymain, ymain, ymain, ymain, ymain, wg, bg)


def _part8(x, op):
    rows = [x[i * 8:(i + 1) * 8] for i in range(x.shape[0] // 8)]
    while len(rows) > 1:
        rows = [op(rows[i], rows[i + 1]) for i in range(0, len(rows), 2)]
    return rows[0]


def _attn_kernel(lam_ref, nw_ref, qt_ref, kc_ref, vct_ref, *rest, lam_init, with_latent):
    if with_latent:
        kx_ref, vxt_ref, o_ref, m_sc, acc_sc, nxt_sc = rest
        ki = pl.program_id(3)
        last = pl.num_programs(3) - 1
    else:
        o_ref, m_sc, acc_sc = rest
    qt = qt_ref[...]
    tq = qt.shape[1]
    row = lax.broadcasted_iota(jnp.int32, qt.shape, 0)
    zero = jnp.zeros_like(qt)
    qt_maps = (jnp.where(row < DIFF_DQK, qt, zero), jnp.where(row >= DIFF_DQK, qt, zero))

    def with_ones(vt):
        return jnp.concatenate([vt, jnp.ones((ATTN_ONES, vt.shape[1]), vt.dtype)], axis=0)

    def exact_update(k, vt1):
        for m in range(2):
            st = jnp.dot(k, qt_maps[m], preferred_element_type=F32)
            m_old = m_sc[m]
            m_new = jnp.maximum(m_old, jnp.max(st, axis=0, keepdims=True))
            alpha = jnp.exp2(m_old - m_new)
            p = jnp.exp2(st - m_new)
            acc_sc[m] = alpha * acc_sc[m] + jnp.dot(vt1, p.astype(BF16), preferred_element_type=F32)
            m_sc[m] = m_new

    def lagged_update(k, vt1):
        tk = k.shape[0]
        refs = [m_sc[0], m_sc[1]]
        groups = [(m, gq) for m in range(2) for gq in range(tq // ATTN_QG)]
        cols_of = lambda gq: slice(gq * ATTN_QG, (gq + 1) * ATTN_QG)
        scores = lambda m, gq: jnp.dot(k, qt_maps[m][:, cols_of(gq)], preferred_element_type=F32)
        mx_g = ([], [])
        st = scores(*groups[0])
        for gi, (m, gq) in enumerate(groups):
            cols = cols_of(gq)
            rg = refs[m][:, cols]
            mx = None
            tiles = []
            for c in range(tk // ATTN_KC):
                tile = st[c * ATTN_KC:(c + 1) * ATTN_KC]
                tiles.append(jnp.exp2(tile - rg).astype(BF16))
                tmx = _part8(tile, jnp.maximum)
                mx = tmx if mx is None else jnp.maximum(mx, tmx)
            mx_g[m].append(mx)
            if gi + 1 < len(groups):
                st = scores(*groups[gi + 1])
            pv = jnp.dot(vt1, jnp.concatenate(tiles, axis=0), preferred_element_type=F32)
            nxt_sc[m, :, cols] = acc_sc[m, :, cols] + pv
        bmax = [jnp.max(jnp.concatenate(mx_g[m], axis=1), axis=0, keepdims=True) for m in range(2)]
        lag = jnp.maximum(jnp.max(bmax[0] - refs[0]), jnp.max(bmax[1] - refs[1]))
        safe = lag <= ATTN_LAG_LIMIT

        @pl.when(safe)
        def _():
            for m in range(2):
                r_new = jnp.maximum(refs[m], bmax[m])
                acc_sc[m] = nxt_sc[m] * jnp.exp2(refs[m] - r_new)
                m_sc[m] = r_new

        @pl.when(jnp.logical_not(safe))
        def _():
            exact_update(k, vt1)

    def init_and_ctx():
        m_sc[...] = jnp.full(m_sc.shape, -jnp.inf, F32)
        acc_sc[...] = jnp.zeros(acc_sc.shape, F32)
        exact_update(kc_ref[...], with_ones(vct_ref[...]))

    def finish():
        lp = lam_ref[...]
        lam = (jnp.exp(jnp.sum(lp[0:1] * lp[1:2], keepdims=True))
               - jnp.exp(jnp.sum(lp[2:3] * lp[3:4], keepdims=True)) + lam_init)
        a0, a1 = acc_sc[0], acc_sc[1]
        ot = (a0[:DIFF_DV] / a0[DIFF_DV:DIFF_DV + 1]
              - lam * (a1[:DIFF_DV] / a1[DIFF_DV:DIFF_DV + 1]))
        ms = jnp.mean(ot * ot, axis=0, keepdims=True)
        on = ot * lax.rsqrt(ms + RMS_EPS) * nw_ref[...] * (1.0 - lam_init)
        o_ref[...] = on.T.astype(o_ref.dtype)

    if with_latent:
        pl.when(ki == 0)(init_and_ctx)
        lagged_update(kx_ref[...], with_ones(vxt_ref[...]))
        pl.when(ki == last)(finish)
    else:
        init_and_ctx()
        finish()


def _attn_call(g, k2, qt, vt, lam_p, norm_w_col, lam_init):
    hd = 2 * DIFF_DQK
    tq = min(ATTN_TQ, g.t)
    tk = min(ATTN_TK, g.t)
    assert g.t % tq == 0 and g.t % tk == 0
    nq, nk = g.t // tq, g.t // tk
    c0 = g.lat_blocks
    scratch = lambda nqry: [pltpu.VMEM((2, 1, nqry), F32),
                            pltpu.VMEM((2, DIFF_DV + ATTN_ONES, nqry), F32)]
    common = [pl.BlockSpec((4, DIFF_DQK), lambda *_: (0, 0)),
              pl.BlockSpec((DIFF_DV, 1), lambda *_: (0, 0))]
    od = pl.pallas_call(
        functools.partial(_attn_kernel, lam_init=lam_init, with_latent=True),
        grid=(g.b, DIFF_HEADS, nq, nk),
        in_specs=common + [
            pl.BlockSpec((hd, tq), lambda b, h, qi, ki: (h, b * nq + qi)),
            pl.BlockSpec((g.c, hd), lambda b, h, qi, ki: (c0 + b, h)),
            pl.BlockSpec((DIFF_DV, g.c), lambda b, h, qi, ki: (h, c0 + b)),
            pl.BlockSpec((tk, hd), lambda b, h, qi, ki: (b * nk + ki, h)),
            pl.BlockSpec((DIFF_DV, tk), lambda b, h, qi, ki: (h, b * nk + ki))],
        out_specs=pl.BlockSpec((tq, DIFF_DV), lambda b, h, qi, ki: (b * nq + qi, h)),
        out_shape=jax.ShapeDtypeStruct((g.b * g.t, DIFF_HEADS * DIFF_DV), BF16),
        scratch_shapes=scratch(tq) + [pltpu.VMEM((2, DIFF_DV + ATTN_ONES, tq), F32)],
        compiler_params=_cparams(("parallel", "parallel", "parallel", "arbitrary")),
        name="diff_attn_latent",
    )(lam_p, norm_w_col, qt, k2, vt, k2, vt)
    return od


def _attn_ctx_call(g, k2, qt, vt, lam_p, norm_w_col, lam_init):
    hd = 2 * DIFF_DQK
    return pl.pallas_call(
        functools.partial(_attn_kernel, lam_init=lam_init, with_latent=False),
        grid=(g.b, DIFF_HEADS),
        in_specs=[pl.BlockSpec((4, DIFF_DQK), lambda *_: (0, 0)),
                  pl.BlockSpec((DIFF_DV, 1), lambda *_: (0, 0)),
                  pl.BlockSpec((hd, g.c), lambda b, h: (h, g.lat_blocks + b)),
                  pl.BlockSpec((g.c, hd), lambda b, h: (g.lat_blocks + b, h)),
                  pl.BlockSpec((DIFF_DV, g.c), lambda b, h: (h, g.lat_blocks + b))],
        out_specs=pl.BlockSpec((g.c, DIFF_DV), lambda b, h: (b, h)),
        out_shape=jax.ShapeDtypeStruct((g.b * g.c, DIFF_HEADS * DIFF_DV), BF16),
        scratch_shapes=[pltpu.VMEM((2, 1, g.c), F32),
                        pltpu.VMEM((2, DIFF_DV + ATTN_ONES, g.c), F32)],
        compiler_params=_cparams(("parallel", "parallel")),
        name="diff_attn_context",
    )(lam_p, norm_w_col, qt, k2, vt)


def _proj_odd_kernel(x_ref, xp_ref, xn_ref, mod_ref, w_ref, cw_ref, cb_ref, z_ref, xbc_ref, dt_ref,
                     seq_ref, *, lat_blocks, nb, d_inner, conv_ch):
    i = pl.program_id(0)
    j = i % nb
    is_lat = i < lat_blocks
    has_prev = jnp.logical_and(is_lat, j > 0).astype(F32)
    has_next = jnp.logical_and(is_lat, j < nb - 1).astype(F32)
    scale = 1.0 + mod_ref[1:2, :]
    shift = mod_ref[0:1, :]
    rows = jnp.concatenate([x_ref[...], xp_ref[...], xn_ref[...]], axis=0)
    h = (rows * scale + shift).astype(BF16)
    n = x_ref.shape[0]
    cw = PROJ_CHUNK
    other = [(z_ref, c0, c0) for c0 in range(0, d_inner, cw)]
    other += [(dt_ref, c0, d_inner + conv_ch + c0) for c0 in range(0, dt_ref.shape[1], cw)]
    pad = SSD_CONV // 2
    n_chunks = conv_ch // cw
    assert len(other) <= n_chunks
    conv_cols = lambda jc: jnp.dot(h, w_ref[:, d_inner + jc * cw:d_inner + (jc + 1) * cw],
                                   preferred_element_type=F32)
    y = conv_cols(0)
    for jc in range(n_chunks):
        cols = slice(jc * cw, (jc + 1) * cw)
        seq_ref[jc, 0:8, :] = y[n:n + 8] * has_prev
        seq_ref[jc, 8:8 + n, :] = y[:n]
        seq_ref[jc, 8 + n:16 + n, :] = y[n + 8:n + 16] * has_next
        if jc + 1 < n_chunks:
            y = conv_cols(jc + 1)
        if jc < len(other):
            o_ref, oc, wc = other[jc]
            o_ref[:, oc:oc + cw] = jnp.dot(h[:n], w_ref[:, wc:wc + cw], preferred_element_type=F32)
        w = cw_ref[:, cols]
        acc = cb_ref[:, cols] + seq_ref[jc, 8:8 + n, :] * w[pad:pad + 1]
        for tap in range(SSD_CONV):
            if tap != pad:
                acc = acc + seq_ref[jc, pl.ds(8 - pad + tap, n), :] * w[tap:tap + 1]
        xbc_ref[:, cols] = _silu(acc)


def _proj_odd_call(g, s_rows, mod_l, w, conv_w8, conv_b, d_inner, conv_ch):
    blk = g.blk
    assert g.tm % blk == 0
    per_tile = g.tm // blk
    n = w.shape[1]
    dt_w = n - d_inner - conv_ch
    nrow8 = g.rows // 8
    per = blk // 8
    row = lambda i: (i, 0)
    return pl.pallas_call(
        functools.partial(_proj_odd_kernel, lat_blocks=g.lat_blocks, nb=g.nb, d_inner=d_inner,
                          conv_ch=conv_ch),
        grid=(g.rows // blk,),
        in_specs=[pl.BlockSpec((blk, g.d), row),
                  pl.BlockSpec((8, g.d), lambda i: (jnp.maximum(i * per - 1, 0), 0)),
                  pl.BlockSpec((8, g.d), lambda i: (jnp.minimum((i + 1) * per, nrow8 - 1), 0)),
                  pl.BlockSpec((None, 6, g.d), lambda i: (g.mod_row(i // per_tile), 0, 0)),
                  _resident((g.d, n)),
                  pl.BlockSpec((8, conv_ch), lambda i: (0, 0)),
                  pl.BlockSpec((1, conv_ch), lambda i: (0, 0))],
        out_specs=[pl.BlockSpec((blk, d_inner), row),
                   pl.BlockSpec((blk, conv_ch), row),
                   pl.BlockSpec((blk, dt_w), row)],
        out_shape=[jax.ShapeDtypeStruct((g.rows, d_inner), F32),
                   jax.ShapeDtypeStruct((g.rows, conv_ch), F32),
                   jax.ShapeDtypeStruct((g.rows, dt_w), F32)],
        scratch_shapes=[pltpu.VMEM((conv_ch // PROJ_CHUNK, blk + 16, PROJ_CHUNK), F32)],
        compiler_params=_cparams(("parallel",)),
        name="proj_odd_conv",
    )(s_rows, s_rows, s_rows, mod_l, w, conv_w8, conv_b)


def _ssd_kernel(xf, bmf, cmf, dtf, xb, bmb, cmb, dtb, bias_ref, alog_ref, dsk_ref, yf_ref, yb_ref,
                st_ref):
    s = pl.program_id(2)

    @pl.when(s == 0)
    def _():
        st_ref[...] = jnp.zeros_like(st_ref)

    n, ch = MIX_BLOCK, SSD_CHUNK
    nch = n // ch
    gw = SSD_HEADS_PER_GROUP * SSD_HEAD_DIM
    ri = lax.broadcasted_iota(jnp.int32, (n, n), 0)
    ci = lax.broadcasted_iota(jnp.int32, (n, n), 1)
    same_chunk = (ri // ch) == (ci // ch)
    ri_c = lax.broadcasted_iota(jnp.int32, (ch, ch), 0)
    ci_c = lax.broadcasted_iota(jnp.int32, (ch, ch), 1)
    lane_c = lax.broadcasted_iota(jnp.int32, (ch, 128), 1)
    a_neg = -jnp.exp(alog_ref[...])
    exp_row = lax.broadcasted_iota(jnp.int32, (128, gw), 0)
    exp_col = lax.broadcasted_iota(jnp.int32, (128, gw), 1) // SSD_HEAD_DIM

    dirs = ((xf, bmf, cmf, dtf, yf_ref), (xb, bmb, cmb, dtb, yb_ref))
    dt, acs, acs_t, dt_t, expand, tri = [], [], [], [], [], []
    for d in range(2):
        lane0 = SSD_HEADS_PER_GROUP * d
        expand.append(jnp.where(exp_row == exp_col + lane0, 1.0, 0.0).astype(BF16))
        z = dirs[d][3][...] + bias_ref[...]
        dt.append(jnp.maximum(z, 0.0) + jnp.log1p(jnp.exp(-jnp.abs(z))))
        causal_blk = jnp.logical_and(same_chunk, (ci >= ri) if d == 1 else (ci <= ri))
        acs.append(_split_dot(jnp.where(causal_blk, 1.0, 0.0).astype(BF16), dt[d] * a_neg))
        tri.append((ci_c >= ri_c) if d == 1 else (ci_c <= ri_c))
    for d in range(2):
        acs_t.append(acs[d].T)
        dt_t.append(dt[d].T)
    state = [st_ref[0], st_ref[1]]

    for i in range(nch):
        for d in range(2):
            x_ref, bm_ref, cm_ref, _, y_ref = dirs[d]
            rev = d == 1
            lane0 = SSD_HEADS_PER_GROUP * d
            c = nch - 1 - i if rev else i
            r0, r1 = c * ch, (c + 1) * ch
            xc = x_ref[r0:r1, :]
            bmc = bm_ref[r0:r1, :]
            acs_c = acs[d][r0:r1, :]
            acs_tc = acs_t[d][:, r0:r1]
            dt_tc = dt_t[d][:, r0:r1]
            cm16 = cm_ref[r0:r1, :].astype(BF16)
            cb = lax.dot_general(cm16, bmc.astype(BF16), (((1,), (1,)), ((), ())),
                                 preferred_element_type=F32)
            a_last = acs_c[0:1, :] if rev else acs_c[ch - 1:ch, :]
            fac = jnp.concatenate([jnp.exp(acs_c), dt[d][r0:r1, :] * jnp.exp(a_last - acs_c)], axis=0)
            fac = jnp.dot(fac.astype(BF16), expand[d], preferred_element_type=F32)
            dec = _split_dot_r(jnp.broadcast_to(jnp.exp(a_last), (8, 128)), expand[d])[0:1]
            pairs = []
            for pp in range(SSD_HEADS_PER_GROUP // 2):
                mats = []
                for e in (2 * pp, 2 * pp + 1):
                    ln = lane0 + e
                    seg = jnp.exp(jnp.where(tri[d], acs_c[:, ln:ln + 1] - acs_tc[ln:ln + 1, :], -jnp.inf))
                    mats.append((cb * seg * dt_tc[ln:ln + 1, :]).astype(BF16))
                xp = xc[:, 128 * pp:128 * (pp + 1)]
                x2 = jnp.concatenate([jnp.where(lane_c < SSD_HEAD_DIM, xp, 0.0),
                                      jnp.where(lane_c >= SSD_HEAD_DIM, xp, 0.0)], axis=0).astype(BF16)
                pairs.append(jnp.dot(jnp.concatenate(mats, axis=1), x2, preferred_element_type=F32))
            y_off = jnp.dot(cm16, state[d].astype(BF16), preferred_element_type=F32) * fac[0:ch]
            y = jnp.concatenate(pairs, axis=1) + y_off
            if not rev:
                y = y + dsk_ref[...] * xc
            y_ref[r0:r1, :] = y
            upd = jnp.dot(bmc.T.astype(BF16), (xc * fac[ch:2 * ch]).astype(BF16),
                          preferred_element_type=F32)
            state[d] = dec * state[d] + upd
    st_ref[0] = state[0]
    st_ref[1] = state[1]


def _ssd_call(g, xbc, y_odd, dt_bias_slab, a_log_slab, d_skip, d_inner):
    blk = g.blk
    gw = SSD_HEADS_PER_GROUP * SSD_HEAD_DIM
    bm0 = d_inner // 128
    cm0 = bm0 + SSD_GROUPS
    dt0 = 0

    def specs(block_fn):
        return [pl.BlockSpec((blk, gw), lambda b, gi, s: (block_fn(b, s), gi)),
                pl.BlockSpec((blk, 128), lambda b, gi, s: (block_fn(b, s), bm0 + gi)),
                pl.BlockSpec((blk, 128), lambda b, gi, s: (block_fn(b, s), cm0 + gi)),
                pl.BlockSpec((blk, 128), lambda b, gi, s: (block_fn(b, s), dt0 + gi))]

    return pl.pallas_call(
        _ssd_kernel,
        grid=(g.b, SSD_GROUPS, g.nb + 1),
        in_specs=specs(g.fwd_block) + specs(g.bwd_block) + [
            pl.BlockSpec((None, 1, 128), lambda b, gi, s: (gi, 0, 0)),
            pl.BlockSpec((None, 1, 128), lambda b, gi, s: (gi, 0, 0)),
            pl.BlockSpec((None, 1, gw), lambda b, gi, s: (gi, 0, 0))],
        out_specs=[pl.BlockSpec((blk, gw), lambda b, gi, s: (g.fwd_block(b, s), gi)),
                   pl.BlockSpec((blk, gw), lambda b, gi, s: (g.bwd_block(b, s), gi))],
        out_shape=[jax.ShapeDtypeStruct((g.rows, d_inner), F32)] * 2,
        scratch_shapes=[pltpu.VMEM((2, SSD_STATE, gw), F32)],
        compiler_params=_cparams(("parallel", "parallel", "arbitrary")),
        name="ssd_scan",
    )(xbc, xbc, xbc, y_odd, xbc, xbc, xbc, y_odd, dt_bias_slab, a_log_slab,
      d_skip.reshape(SSD_GROUPS, 1, gw))


def _ffn_residual(x, mod_ref, win_ref, wout_ref, ln_ref, alpha, hidden):
    h = (x * (1.0 + mod_ref[4:5, :]) + mod_ref[3:4, :]).astype(BF16)
    step = -(-hidden // (FFN_CHUNKS * MXU_TILE)) * MXU_TILE
    bounds = [(c0, min(c0 + step, hidden)) for c0 in range(0, hidden, step)]

    def gate_up(c0, c1):
        return (jnp.dot(h, win_ref[:, c0:c1], preferred_element_type=F32),
                jnp.dot(h, win_ref[:, hidden + c0:hidden + c1], preferred_element_type=F32))

    acc = None
    gate, up = gate_up(*bounds[0])
    for j, (c0, c1) in enumerate(bounds):
        nxt = gate_up(*bounds[j + 1]) if j + 1 < len(bounds) else None
        act = (_silu(gate) * up).astype(BF16)
        part = jnp.dot(act, wout_ref[c0:c1, :], preferred_element_type=F32)
        acc = part if acc is None else acc + part
        if nxt is not None:
            gate, up = nxt
    v = alpha * x + mod_ref[5:6, :] * acc
    return _layer_norm(v, ln_ref[2:3, :], ln_ref[3:4, :])


def _post_even_kernel(x_ref, mod_ref, of_ref, ob_ref, g_ref, odc_ref, odl_ref, nw_ref, w_ref, ln_ref,
                      win_ref, wout_ref, o_ref, *, alpha, lat_tiles, hidden):
    o = of_ref[...] + ob_ref[...]
    gate = g_ref[...]
    nw = nw_ref[...]
    parts = []
    for h in range(GLA_HEADS):
        oh = o[:, h * GLA_DV:(h + 1) * GLA_DV]
        ms = jnp.mean(oh * oh, axis=-1, keepdims=True)
        parts.append(oh * lax.rsqrt(ms + RMS_EPS) * nw)
    gla = (jnp.concatenate(parts, axis=1) * _silu(gate)).astype(BF16)
    od = jnp.where(pl.program_id(0) < lat_tiles, odl_ref[...], odc_ref[...])
    mixin = jnp.concatenate([gla, od], axis=1)
    mix = jnp.dot(mixin, w_ref[...], preferred_element_type=F32)
    v = alpha * x_ref[...] + mod_ref[2:3, :] * mix
    x1 = _layer_norm(v, ln_ref[0:1, :], ln_ref[1:2, :])
    o_ref[...] = _ffn_residual(x1, mod_ref, win_ref, wout_ref, ln_ref, alpha, hidden)


def _post_even_call(g, s_rows, mod_l, o_f, o_b, ymain, od_ctx, od_lat, gla_nw, w_mix, ln4, w_in, w_out,
                    alpha, latent_only):
    tm, d = g.tm, g.d
    vw = EV_GLA_V
    n_tiles = g.lat_tiles if latent_only else g.n_tiles
    row = lambda i: (i, 0)
    return pl.pallas_call(
        functools.partial(_post_even_kernel, alpha=alpha, lat_tiles=g.lat_tiles, hidden=w_out.shape[0]),
        grid=(n_tiles,),
        in_specs=[pl.BlockSpec((tm, d), row),
                  pl.BlockSpec((None, 6, d), lambda i: (g.mod_row(i), 0, 0)),
                  pl.BlockSpec((tm, vw), row),
                  pl.BlockSpec((tm, vw), row),
                  pl.BlockSpec((tm, vw), lambda i: (i, 2 * EV_GLA_W // vw + 1)),
                  pl.BlockSpec((tm, vw), lambda i: (jnp.maximum(i - g.lat_tiles, 0), 0)),
                  pl.BlockSpec((tm, vw), lambda i: (jnp.minimum(i, g.lat_tiles - 1), 0)),
                  pl.BlockSpec((1, GLA_DV), lambda i: (0, 0)),
                  _resident(w_mix.shape),
                  pl.BlockSpec((4, d), lambda i: (0, 0)),
                  _resident(w_in.shape),
                  _resident(w_out.shape)],
        out_specs=pl.BlockSpec((tm, d), lambda i: (i, 0)),
        out_shape=jax.ShapeDtypeStruct((n_tiles * tm, d), F32),
        compiler_params=_cparams(("parallel",)),
        name="post_even",
    )(s_rows, mod_l, o_f, o_b, ymain, od_ctx, od_lat, gla_nw, w_mix, ln4, w_in, w_out)


def _post_odd_kernel(x_ref, mod_ref, yf_ref, yb_ref, z_ref, nw_ref, w_ref, ln_ref, win_ref, wout_ref,
                     o_ref, *, alpha, hidden):
    y = (yf_ref[...] + yb_ref[...]) * _silu(z_ref[...])
    gw = y.shape[1] // SSD_GROUPS
    parts = []
    for gi in range(SSD_GROUPS):
        yg = y[:, gi * gw:(gi + 1) * gw]
        ms = jnp.mean(yg * yg, axis=-1, keepdims=True)
        parts.append(yg * lax.rsqrt(ms + RMS_EPS))
    yn = (jnp.concatenate(parts, axis=1) * nw_ref[...]).astype(BF16)
    mix = jnp.dot(yn, w_ref[...], preferred_element_type=F32)
    v = alpha * x_ref[...] + mod_ref[2:3, :] * mix
    x1 = _layer_norm(v, ln_ref[0:1, :], ln_ref[1:2, :])
    o_ref[...] = _ffn_residual(x1, mod_ref, win_ref, wout_ref, ln_ref, alpha, hidden)


def _post_odd_call(g, s_rows, mod_l, y_f, y_b, z_gate, norm_w, w_mix, ln4, w_in, w_out, alpha,
                   latent_only):
    tm = g.tm // 2
    per_tile = g.tm // tm
    d = g.d
    di = w_mix.shape[0]
    n_tiles = (g.lat_tiles if latent_only else g.n_tiles) * per_tile
    row = lambda i: (i, 0)
    return pl.pallas_call(
        functools.partial(_post_odd_kernel, alpha=alpha, hidden=w_out.shape[0]),
        grid=(n_tiles,),
        in_specs=[pl.BlockSpec((tm, d), row),
                  pl.BlockSpec((None, 6, d), lambda i: (g.mod_row(i // per_tile), 0, 0)),
                  pl.BlockSpec((tm, di), row),
                  pl.BlockSpec((tm, di), row),
                  pl.BlockSpec((tm, di), row),
                  pl.BlockSpec((1, di), lambda i: (0, 0)),
                  _resident(w_mix.shape),
                  pl.BlockSpec((4, d), lambda i: (0, 0)),
                  _resident(w_in.shape),
                  _resident(w_out.shape)],
        out_specs=pl.BlockSpec((tm, d), lambda i: (i, 0)),
        out_shape=jax.ShapeDtypeStruct((n_tiles * tm, d), F32),
        compiler_params=_cparams(("parallel",)),
        name="post_odd",
    )(s_rows, mod_l, y_f, y_b, z_gate, norm_w, w_mix, ln4, w_in, w_out)


def kernel(x, c, ctx, c_ctx, mod_w, mod_b, ln_g, ln_b, ffn_w_in, ffn_w_out, ev_w_in, ev_w_out,
           gla_w_gate2, gla_b_gate, gla_norm_w, diff_lambda, diff_norm_w, ssd_w_in, ssd_conv_w,
           ssd_conv_b, ssd_dt_bias, ssd_a_log, ssd_d, ssd_norm_w, ssd_w_out):
    batch, seq, d = x.shape
    ctx_len = ctx.shape[1]
    depth = mod_w.shape[0]
    g = _Geom(batch, seq, ctx_len, d)
    alpha = (2 * depth) ** 0.25
    d_inner = ssd_w_out.shape[1]
    conv_ch = ssd_conv_w.shape[2]
    n_heads = ssd_d.shape[1]

    s_rows = jnp.concatenate([x.reshape(batch * seq, d), ctx.reshape(batch * ctx_len, d)], axis=0)
    cc = jnp.zeros((8, d), F32).at[:batch].set(c).at[batch].set(c_ctx)
    mod_all = _mod_call(cc, mod_w, mod_b).reshape(depth, 8, 6, d)
    rope_tab = _rope_table(g)

    for layer in range(depth):
        need_ctx = layer < depth - 1
        mod_l = mod_all[layer]
        ln4 = jnp.stack([ln_g[layer, 0], ln_b[layer, 0], ln_g[layer, 1], ln_b[layer, 1]])
        w_ffn_in = ffn_w_in[layer].astype(BF16)
        w_ffn_out = ffn_w_out[layer].astype(BF16)
        if layer % 2 == 0:
            e = layer // 2
            w = ev_w_in[e]
            w_my = jnp.concatenate([w[:, :EV_REAL_MAIN], jnp.zeros((d, 128 - 2 * GLA_RANK), w.dtype),
                                    w[:, EV_REAL_MAIN:]], axis=1).astype(BF16)
            ymain, k2, qt, vt = _proj_even_call(g, s_rows, mod_l, w_my, rope_tab)
            wg = jnp.zeros((2, 128, EV_GLA_W), F32)
            wg = wg.at[0, :GLA_RANK].set(gla_w_gate2[e, 0]).at[1, GLA_RANK:2 * GLA_RANK].set(gla_w_gate2[e, 1])
            o_f, o_b = _gla_call(g, ymain, wg.astype(BF16), gla_b_gate[e].reshape(2, 1, EV_GLA_W))
            lam_init = 0.8 - 0.6 * math.exp(-0.3 * layer)
            nw_col = diff_norm_w[e].reshape(DIFF_DV, 1)
            od_lat = _attn_call(g, k2, qt, vt, diff_lambda[e], nw_col, lam_init)
            od_ctx = _attn_ctx_call(g, k2, qt, vt, diff_lambda[e], nw_col, lam_init)
            s_rows = _post_even_call(g, s_rows, mod_l, o_f, o_b, ymain, od_ctx, od_lat,
                                     gla_norm_w[e].reshape(1, GLA_DV), ev_w_out[e].astype(BF16), ln4,
                                     w_ffn_in, w_ffn_out, alpha, latent_only=not need_ctx)
        else:
            o = layer // 2
            w = ssd_w_in[o]
            dt_col0 = d_inner + conv_ch
            hpg = SSD_HEADS_PER_GROUP
            slabs = []
            for gi in range(SSD_GROUPS):
                slabs += [w[:, dt_col0 + hpg * gi:dt_col0 + hpg * (gi + 1)],
                          w[:, dt_col0 + n_heads + hpg * gi:dt_col0 + n_heads + hpg * (gi + 1)],
                          jnp.zeros((d, 128 - 2 * hpg), w.dtype)]
            w_my = jnp.concatenate([w[:, :dt_col0]] + slabs, axis=1).astype(BF16)

            def slab_vec(v2):
                rows = [jnp.concatenate([v2[0, hpg * gi:hpg * (gi + 1)], v2[1, hpg * gi:hpg * (gi + 1)],
                                         jnp.zeros((128 - 2 * hpg,), F32)]) for gi in range(SSD_GROUPS)]
                return jnp.stack(rows).reshape(SSD_GROUPS, 1, 128)

            conv_w8 = jnp.concatenate([ssd_conv_w[o], jnp.zeros((8 - SSD_CONV, conv_ch), F32)], axis=0)
            z_gate, xbc, dt_raw = _proj_odd_call(g, s_rows, mod_l, w_my, conv_w8,
                                                 ssd_conv_b[o].reshape(1, conv_ch), d_inner, conv_ch)
            d_skip = jnp.repeat(ssd_d[o], SSD_HEAD_DIM)
            y_f, y_b = _ssd_call(g, xbc, dt_raw, slab_vec(ssd_dt_bias[o]), slab_vec(ssd_a_log[o]),
                                 d_skip, d_inner)
            s_rows = _post_odd_call(g, s_rows, mod_l, y_f, y_b, z_gate, ssd_norm_w[o].reshape(1, d_inner),
                                    ssd_w_out[o].astype(BF16), ln4, w_ffn_in, w_ffn_out, alpha,
                                    latent_only=not need_ctx)
    return s_rows.reshape(batch, seq, d)
```

```python
import functools
import math

import jax
import jax.numpy as jnp
from jax import lax
from jax.experimental import pallas as pl
from jax.experimental.pallas import tpu as pltpu

F32 = jnp.float32
BF16 = jnp.bfloat16

GRID_W = 64
GLA_HEADS, GLA_DK, GLA_DV, GLA_RANK, GLA_CHUNK = 4, 64, 128, 16, 64
GLA_INV_TAU = 1.0 / 16.0
DIFF_HEADS, DIFF_DQK, DIFF_DV = 4, 64, 128
ROPE_BASE = 10000.0
SSD_HEAD_DIM, SSD_GROUPS, SSD_STATE, SSD_CONV, SSD_CHUNK = 64, 4, 128, 5, 128
SSD_HEADS_PER_GROUP = 8
LN_EPS = 1e-6
RMS_EPS = 1e-6

EV_GLA_W = GLA_HEADS * GLA_DK
EV_GLA_V = GLA_HEADS * GLA_DV
EV_MAIN = 2 * EV_GLA_W + 2 * EV_GLA_V + 128
EV_DIFF = DIFF_HEADS * 2 * DIFF_DQK
EV_REAL_MAIN = 2 * EV_GLA_W + 2 * EV_GLA_V + 2 * GLA_RANK

ROW_TILE = 512
MIX_BLOCK = 256
ATTN_TQ = 2048
ATTN_TK = 2048
ATTN_QG = 256
ATTN_KC = 128
ATTN_LAG_LIMIT = 50.0
ATTN_ONES = 16
LOG2E = 1.4426950408889634
PROJ_CHUNK = 512
MXU_TILE = 256
FFN_CHUNKS = 4
VMEM_LIMIT = 56 * 1024 * 1024


def _cparams(sem):
    return pltpu.CompilerParams(dimension_semantics=sem, vmem_limit_bytes=VMEM_LIMIT)


def _resident(shape):
    nd = len(shape)
    return pl.BlockSpec(shape, lambda *_: (0,) * nd, pipeline_mode=pl.Buffered(1))


def _silu(v):
    return v / (1.0 + jnp.exp(-v))


def _layer_norm(v, g, b):
    mu = jnp.mean(v, axis=-1, keepdims=True)
    d = v - mu
    var = jnp.mean(d * d, axis=-1, keepdims=True)
    return d * lax.rsqrt(var + LN_EPS) * g + b


def _split_dot(mat_bf16, v):
    hi = v.astype(BF16)
    lo = (v - hi.astype(F32)).astype(BF16)
    return (jnp.dot(mat_bf16, hi, preferred_element_type=F32)
            + jnp.dot(mat_bf16, lo, preferred_element_type=F32))


def _split_dot_r(v, mat_bf16):
    hi = v.astype(BF16)
    lo = (v - hi.astype(F32)).astype(BF16)
    return (jnp.dot(hi, mat_bf16, preferred_element_type=F32)
            + jnp.dot(lo, mat_bf16, preferred_element_type=F32))


def _mod_kernel(c_ref, w_ref, b_ref, o_ref):
    s = _silu(c_ref[...]).astype(BF16)
    o_ref[...] = jnp.dot(s, w_ref[...].astype(BF16), preferred_element_type=F32) + b_ref[...]


def _mod_call(cc, mod_w, mod_b):
    depth, d, n = mod_w.shape
    cw = d
    return pl.pallas_call(
        _mod_kernel,
        grid=(depth, n // cw),
        in_specs=[pl.BlockSpec((8, d), lambda l, j: (0, 0)),
                  pl.BlockSpec((None, d, cw), lambda l, j: (l, 0, j)),
                  pl.BlockSpec((None, 1, cw), lambda l, j: (l, 0, j))],
        out_specs=pl.BlockSpec((None, 8, cw), lambda l, j: (l, 0, j)),
        out_shape=jax.ShapeDtypeStruct((depth, 8, n), F32),
        compiler_params=_cparams(("parallel", "parallel")),
        name="mod_vectors",
    )(cc, mod_w, mod_b.reshape(depth, 1, n))


class _Geom:
    def __init__(self, batch, seq, ctx_len, d_model):
        self.b, self.t, self.c, self.d = batch, seq, ctx_len, d_model
        self.rows = batch * (seq + ctx_len)
        self.tm = min(ROW_TILE, batch * ctx_len)
        assert (batch * ctx_len) % self.tm == 0 and seq % self.tm == 0
        self.tiles_per_batch = seq // self.tm
        self.n_tiles = self.rows // self.tm
        self.lat_tiles = batch * seq // self.tm
        self.blk = MIX_BLOCK
        assert ctx_len == self.blk and seq % self.blk == 0
        self.nb = seq // self.blk
        self.lat_blocks = batch * self.nb

    def mod_row(self, tile):
        return jnp.where(tile < self.lat_tiles, tile // self.tiles_per_batch, self.b)

    def fwd_block(self, b, s):
        return jnp.where(s == 0, self.lat_blocks + b, self.nb * b + s - 1)

    def bwd_block(self, b, s):
        return jnp.where(s == 0, self.lat_blocks + b, self.nb * b + self.nb - s)


def _proj_even_kernel(x_ref, mod_ref, w_ref, rope_ref, ymain_ref, k2_ref, qt_ref, vt_ref):
    x = x_ref[...]
    h = (x * (1.0 + mod_ref[1:2, :]) + mod_ref[0:1, :]).astype(BF16)
    y = jnp.dot(h, w_ref[...], preferred_element_type=F32)
    ymain_ref[...] = y[:, :EV_MAIN]
    cos = rope_ref[:, 0:128]
    sin_up = rope_ref[:, 128:256]
    sin_dn = rope_ref[:, 256:384]

    def rope(t):
        return t * cos + pltpu.roll(t, 112, 1) * sin_up + pltpu.roll(t, 16, 1) * sin_dn

    for j in range(DIFF_HEADS):
        lo, hi = 128 * j, 128 * (j + 1)
        q = rope(y[:, EV_MAIN + lo:EV_MAIN + hi]) * (DIFF_DQK ** -0.5 * LOG2E)
        qt_ref[lo:hi, :] = q.T.astype(BF16)
        k = rope(y[:, EV_MAIN + EV_DIFF + lo:EV_MAIN + EV_DIFF + hi])
        k2_ref[:, lo:hi] = k.astype(BF16)
        v = y[:, EV_MAIN + 2 * EV_DIFF + lo:EV_MAIN + 2 * EV_DIFF + hi]
        vt_ref[lo:hi, :] = v.T.astype(BF16)


def _proj_even_call(g, s_rows, mod_l, w, rope_tab):
    tm, d = g.tm, g.d
    n = w.shape[1]
    def rope_idx(i):
        return jnp.where(i < g.lat_tiles, 1 + i % g.tiles_per_batch, 0)

    return pl.pallas_call(
        _proj_even_kernel,
        grid=(g.n_tiles,),
        in_specs=[pl.BlockSpec((tm, d), lambda i: (i, 0)),
                  pl.BlockSpec((None, 6, d), lambda i: (g.mod_row(i), 0, 0)),
                  _resident((d, n)),
                  pl.BlockSpec((tm, 384), lambda i: (rope_idx(i), 0))],
        out_specs=[pl.BlockSpec((tm, EV_MAIN), lambda i: (i, 0)),
                   pl.BlockSpec((tm, EV_DIFF), lambda i: (i, 0)),
                   pl.BlockSpec((EV_DIFF, tm), lambda i: (0, i)),
                   pl.BlockSpec((EV_DIFF, tm), lambda i: (0, i))],
        out_shape=[jax.ShapeDtypeStruct((g.rows, EV_MAIN), F32),
                   jax.ShapeDtypeStruct((g.rows, EV_DIFF), BF16),
                   jax.ShapeDtypeStruct((EV_DIFF, g.rows), BF16),
                   jax.ShapeDtypeStruct((EV_DIFF, g.rows), BF16)],
        compiler_params=_cparams(("parallel",)),
        name="proj_even",
    )(s_rows, mod_l, w, rope_tab)


def _rope_table(g):
    rows = g.t // GRID_W
    n_freq = DIFF_DQK // 4
    inv = ROPE_BASE ** (-jnp.arange(n_freq, dtype=F32) / n_freq)
    ang_r = jnp.arange(rows, dtype=F32)[:, None] * inv
    ang_c = jnp.arange(GRID_W, dtype=F32)[:, None] * inv
    per_row = lambda tab: jnp.repeat(tab, GRID_W, axis=0)
    per_col = lambda tab: jnp.tile(tab, (rows, 1))
    cr, sr = per_row(jnp.cos(ang_r)), per_row(jnp.sin(ang_r))
    cc, sc = per_col(jnp.cos(ang_c)), per_col(jnp.sin(ang_c))
    z = jnp.zeros_like(cr)
    cos64 = jnp.concatenate([cr, cr, cc, cc], axis=-1)
    up64 = jnp.concatenate([-sr, z, -sc, z], axis=-1)
    dn64 = jnp.concatenate([z, sr, z, sc], axis=-1)
    tab = jnp.concatenate([cos64, cos64, up64, up64, dn64, dn64], axis=-1)
    ident = jnp.concatenate([jnp.ones((g.tm, 128), F32), jnp.zeros((g.tm, 256), F32)], axis=-1)
    return jnp.concatenate([ident, tab], axis=0)


def _gla_kernel(qf, kf, vf, rf, qb, kb, vb, rb, wg_ref, bg_ref, of_ref, ob_ref, st_ref):
    s = pl.program_id(1)

    @pl.when(s == 0)
    def _():
        st_ref[...] = jnp.zeros_like(st_ref)

    n = MIX_BLOCK
    nch = n // GLA_CHUNK
    ri = lax.broadcasted_iota(jnp.int32, (n, n), 0)
    ci = lax.broadcasted_iota(jnp.int32, (n, n), 1)
    same_chunk = (ri // GLA_CHUNK) == (ci // GLA_CHUNK)
    col_chunk = ci // GLA_CHUNK
    lane_head = lax.broadcasted_iota(jnp.int32, (1, EV_GLA_W), 1) // GLA_DK
    diag_blocks = ((lax.broadcasted_iota(jnp.int32, (EV_GLA_W, EV_GLA_V), 0) // GLA_DK)
                   == (lax.broadcasted_iota(jnp.int32, (EV_GLA_W, EV_GLA_V), 1) // GLA_DV))

    dirs = ((qf, kf, vf, rf, of_ref), (qb, kb, vb, rb, ob_ref))
    causal, la, qd, ki, kd, v16 = [], [], [], [], [], []
    for d, (q_ref, k_ref, v_ref, r_ref, _) in enumerate(dirs):
        causal.append(jnp.logical_and(same_chunk, (ci >= ri) if d == 1 else (ci <= ri)))
        z = jnp.dot(r_ref[...].astype(BF16), wg_ref[d], preferred_element_type=F32) + bg_ref[d]
        la.append((jnp.minimum(z, 0.0) - jnp.log1p(jnp.exp(-jnp.abs(z)))) * GLA_INV_TAU)
    for d, (q_ref, k_ref, v_ref, r_ref, _) in enumerate(dirs):
        cum = _split_dot(jnp.where(causal[d], 1.0, 0.0).astype(BF16), la[d])
        tot = jnp.concatenate(
            [jnp.broadcast_to(jnp.sum(la[d][c * GLA_CHUNK:(c + 1) * GLA_CHUNK], axis=0, keepdims=True),
                              (GLA_CHUNK, EV_GLA_W)) for c in range(nch)], axis=0)
        k = k_ref[...]
        v16.append(v_ref[...].astype(BF16))
        qd.append(q_ref[...] * jnp.exp(cum) * (GLA_DK ** -0.5))
        ki.append((k * jnp.exp(-cum)).astype(BF16))
        kd.append(k * jnp.exp(tot - cum))

    parts = ([], [])
    for h in range(GLA_HEADS):
        for d in range(2):
            qh = jnp.where(lane_head == h, qd[d], 0.0).astype(BF16)
            att = lax.dot_general(qh, ki[d], (((1,), (1,)), ((), ())), preferred_element_type=F32)
            att = jnp.where(causal[d], att, 0.0).astype(BF16)
            parts[d].append(jnp.dot(att, v16[d][:, h * GLA_DV:(h + 1) * GLA_DV],
                                    preferred_element_type=F32))

    la_t = [la[d].T for d in range(2)]
    kd_t = [kd[d].T for d in range(2)]
    qd16 = [qd[d].astype(BF16) for d in range(2)]
    state = [st_ref[0], st_ref[1]]
    o_inter = ([None] * nch, [None] * nch)
    for i in range(nch):
        for d in range(2):
            c = nch - 1 - i if d == 1 else i
            in_chunk = col_chunk == c
            o_inter[d][c] = jnp.dot(qd16[d][c * GLA_CHUNK:(c + 1) * GLA_CHUNK], state[d].astype(BF16),
                                    preferred_element_type=F32)
            kv = jnp.dot(jnp.where(in_chunk, kd_t[d], 0.0).astype(BF16), v16[d],
                         preferred_element_type=F32)
            decay = jnp.exp(jnp.sum(jnp.where(in_chunk, la_t[d], 0.0), axis=1, keepdims=True))
            state[d] = decay * state[d] + jnp.where(diag_blocks, kv, 0.0)
    for d in range(2):
        st_ref[d] = state[d]
        dirs[d][4][...] = jnp.concatenate(parts[d], axis=1) + jnp.concatenate(o_inter[d], axis=0)


def _gla_call(g, ymain, wg, bg):
    blk = g.blk
    qw, vw = EV_GLA_W, EV_GLA_V

    def specs(block_fn):
        return [pl.BlockSpec((blk, qw), lambda b, s: (block_fn(b, s), 0)),
                pl.BlockSpec((blk, qw), lambda b, s: (block_fn(b, s), 1)),
                pl.BlockSpec((blk, vw), lambda b, s: (block_fn(b, s), 1)),
                pl.BlockSpec((blk, 128), lambda b, s: (block_fn(b, s), (EV_MAIN - 128) // 128))]

    return pl.pallas_call(
        _gla_kernel,
        grid=(g.b, g.nb + 1),
        in_specs=specs(g.fwd_block) + specs(g.bwd_block) + [
            pl.BlockSpec((2, 128, qw), lambda b, s: (0, 0, 0)),
            pl.BlockSpec((2, 1, qw), lambda b, s: (0, 0, 0))],
        out_specs=[pl.BlockSpec((blk, vw), lambda b, s: (g.fwd_block(b, s), 0)),
                   pl.BlockSpec((blk, vw), lambda b, s: (g.bwd_block(b, s), 0))],
        out_shape=[jax.ShapeDtypeStruct((g.rows, vw), F32)] * 2,
        scratch_shapes=[pltpu.VMEM((2, qw, vw), F32)],
        compiler_params=_cparams(("parallel", "arbitrary")),
        name="gla_scan",
    )(ymain, ymain, ymain, ymain, ymain, ymain, ymain, ymain, wg, bg)


def _part8(x, op):
    rows = [x[i * 8:(i + 1) * 8] for i in range(x.shape[0] // 8)]
    while len(rows) > 1:
        rows = [op(rows[i], rows[i + 1]) for i in range(0, len(rows), 2)]
    return rows[0]


def _attn_kernel(lam_ref, nw_ref, qt_ref, kc_ref, vct_ref, *rest, lam_init, with_latent):
    if with_latent:
        kx_ref, vxt_ref, o_ref, m_sc, acc_sc, nxt_sc = rest
        ki = pl.program_id(3)
        last = pl.num_programs(3) - 1
    else:
        o_ref, m_sc, acc_sc = rest
    qt = qt_ref[...]
    tq = qt.shape[1]
    row = lax.broadcasted_iota(jnp.int32, qt.shape, 0)
    zero = jnp.zeros_like(qt)
    qt_maps = (jnp.where(row < DIFF_DQK, qt, zero), jnp.where(row >= DIFF_DQK, qt, zero))

    def with_ones(vt):
        return jnp.concatenate([vt, jnp.ones((ATTN_ONES, vt.shape[1]), vt.dtype)], axis=0)

    def exact_update(k, vt1):
        for m in range(2):
            st = jnp.dot(k, qt_maps[m], preferred_element_type=F32)
            m_old = m_sc[m]
            m_new = jnp.maximum(m_old, jnp.max(st, axis=0, keepdims=True))
            alpha = jnp.exp2(m_old - m_new)
            p = jnp.exp2(st - m_new)
            acc_sc[m] = alpha * acc_sc[m] + jnp.dot(vt1, p.astype(BF16), preferred_element_type=F32)
            m_sc[m] = m_new

    def lagged_update(k, vt1):
        tk = k.shape[0]
        refs = [m_sc[0], m_sc[1]]
        groups = [(m, gq) for m in range(2) for gq in range(tq // ATTN_QG)]
        cols_of = lambda gq: slice(gq * ATTN_QG, (gq + 1) * ATTN_QG)
        scores = lambda m, gq: jnp.dot(k, qt_maps[m][:, cols_of(gq)], preferred_element_type=F32)
        mx_g = ([], [])
        st = scores(*groups[0])
        for gi, (m, gq) in enumerate(groups):
            cols = cols_of(gq)
            rg = refs[m][:, cols]
            mx = None
            tiles = []
            for c in range(tk // ATTN_KC):
                tile = st[c * ATTN_KC:(c + 1) * ATTN_KC]
                tiles.append(jnp.exp2(tile - rg).astype(BF16))
                tmx = _part8(tile, jnp.maximum)
                mx = tmx if mx is None else jnp.maximum(mx, tmx)
            mx_g[m].append(mx)
            if gi + 1 < len(groups):
                st = scores(*groups[gi + 1])
            pv = jnp.dot(vt1, jnp.concatenate(tiles, axis=0), preferred_element_type=F32)
            nxt_sc[m, :, cols] = acc_sc[m, :, cols] + pv
        bmax = [jnp.max(jnp.concatenate(mx_g[m], axis=1), axis=0, keepdims=True) for m in range(2)]
        lag = jnp.maximum(jnp.max(bmax[0] - refs[0]), jnp.max(bmax[1] - refs[1]))
        safe = lag <= ATTN_LAG_LIMIT

        @pl.when(safe)
        def _():
            for m in range(2):
                r_new = jnp.maximum(refs[m], bmax[m])
                acc_sc[m] = nxt_sc[m] * jnp.exp2(refs[m] - r_new)
                m_sc[m] = r_new

        @pl.when(jnp.logical_not(safe))
        def _():
            exact_update(k, vt1)

    def init_and_ctx():
        m_sc[...] = jnp.full(m_sc.shape, -jnp.inf, F32)
        acc_sc[...] = jnp.zeros(acc_sc.shape, F32)
        exact_update(kc_ref[...], with_ones(vct_ref[...]))

    def finish():
        lp = lam_ref[...]
        lam = (jnp.exp(jnp.sum(lp[0:1] * lp[1:2], keepdims=True))
               - jnp.exp(jnp.sum(lp[2:3] * lp[3:4], keepdims=True)) + lam_init)
        a0, a1 = acc_sc[0], acc_sc[1]
        ot = (a0[:DIFF_DV] / a0[DIFF_DV:DIFF_DV + 1]
              - lam * (a1[:DIFF_DV] / a1[DIFF_DV:DIFF_DV + 1]))
        ms = jnp.mean(ot * ot, axis=0, keepdims=True)
        on = ot * lax.rsqrt(ms + RMS_EPS) * nw_ref[...] * (1.0 - lam_init)
        o_ref[...] = on.T.astype(o_ref.dtype)

    if with_latent:
        pl.when(ki == 0)(init_and_ctx)
        lagged_update(kx_ref[...], with_ones(vxt_ref[...]))
        pl.when(ki == last)(finish)
    else:
        init_and_ctx()
        finish()


def _attn_call(g, k2, qt, vt, lam_p, norm_w_col, lam_init):
    hd = 2 * DIFF_DQK
    tq = min(ATTN_TQ, g.t)
    tk = min(ATTN_TK, g.t)
    assert g.t % tq == 0 and g.t % tk == 0
    nq, nk = g.t // tq, g.t // tk
    c0 = g.lat_blocks
    scratch = lambda nqry: [pltpu.VMEM((2, 1, nqry), F32),
                            pltpu.VMEM((2, DIFF_DV + ATTN_ONES, nqry), F32)]
    common = [pl.BlockSpec((4, DIFF_DQK), lambda *_: (0, 0)),
              pl.BlockSpec((DIFF_DV, 1), lambda *_: (0, 0))]
    od = pl.pallas_call(
        functools.partial(_attn_kernel, lam_init=lam_init, with_latent=True),
        grid=(g.b, DIFF_HEADS, nq, nk),
        in_specs=common + [
            pl.BlockSpec((hd, tq), lambda b, h, qi, ki: (h, b * nq + qi)),
            pl.BlockSpec((g.c, hd), lambda b, h, qi, ki: (c0 + b, h)),
            pl.BlockSpec((DIFF_DV, g.c), lambda b, h, qi, ki: (h, c0 + b)),
            pl.BlockSpec((tk, hd), lambda b, h, qi, ki: (b * nk + ki, h)),
            pl.BlockSpec((DIFF_DV, tk), lambda b, h, qi, ki: (h, b * nk + ki))],
        out_specs=pl.BlockSpec((tq, DIFF_DV), lambda b, h, qi, ki: (b * nq + qi, h)),
        out_shape=jax.ShapeDtypeStruct((g.b * g.t, DIFF_HEADS * DIFF_DV), BF16),
        scratch_shapes=scratch(tq) + [pltpu.VMEM((2, DIFF_DV + ATTN_ONES, tq), F32)],
        compiler_params=_cparams(("parallel", "parallel", "parallel", "arbitrary")),
        name="diff_attn_latent",
    )(lam_p, norm_w_col, qt, k2, vt, k2, vt)
    return od


def _attn_ctx_call(g, k2, qt, vt, lam_p, norm_w_col, lam_init):
    hd = 2 * DIFF_DQK
    return pl.pallas_call(
        functools.partial(_attn_kernel, lam_init=lam_init, with_latent=False),
        grid=(g.b, DIFF_HEADS),
        in_specs=[pl.BlockSpec((4, DIFF_DQK), lambda *_: (0, 0)),
                  pl.BlockSpec((DIFF_DV, 1), lambda *_: (0, 0)),
                  pl.BlockSpec((hd, g.c), lambda b, h: (h, g.lat_blocks + b)),
                  pl.BlockSpec((g.c, hd), lambda b, h: (g.lat_blocks + b, h)),
                  pl.BlockSpec((DIFF_DV, g.c), lambda b, h: (h, g.lat_blocks + b))],
        out_specs=pl.BlockSpec((g.c, DIFF_DV), lambda b, h: (b, h)),
        out_shape=jax.ShapeDtypeStruct((g.b * g.c, DIFF_HEADS * DIFF_DV), BF16),
        scratch_shapes=[pltpu.VMEM((2, 1, g.c), F32),
                        pltpu.VMEM((2, DIFF_DV + ATTN_ONES, g.c), F32)],
        compiler_params=_cparams(("parallel", "parallel")),
        name="diff_attn_context",
    )(lam_p, norm_w_col, qt, k2, vt)


def _proj_odd_kernel(x_ref, xp_ref, xn_ref, mod_ref, w_ref, cw_ref, cb_ref, z_ref, xbc_ref, dt_ref,
                     seq_ref, *, lat_blocks, nb, d_inner, conv_ch):
    i = pl.program_id(0)
    j = i % nb
    is_lat = i < lat_blocks
    has_prev = jnp.logical_and(is_lat, j > 0).astype(F32)
    has_next = jnp.logical_and(is_lat, j < nb - 1).astype(F32)
    scale = 1.0 + mod_ref[1:2, :]
    shift = mod_ref[0:1, :]
    rows = jnp.concatenate([x_ref[...], xp_ref[...], xn_ref[...]], axis=0)
    h = (rows * scale + shift).astype(BF16)
    n = x_ref.shape[0]
    cw = PROJ_CHUNK
    other = [(z_ref, c0, c0) for c0 in range(0, d_inner, cw)]
    other += [(dt_ref, c0, d_inner + conv_ch + c0) for c0 in range(0, dt_ref.shape[1], cw)]
    pad = SSD_CONV // 2
    n_chunks = conv_ch // cw
    assert len(other) <= n_chunks
    conv_cols = lambda jc: jnp.dot(h, w_ref[:, d_inner + jc * cw:d_inner + (jc + 1) * cw],
                                   preferred_element_type=F32)
    y = conv_cols(0)
    for jc in range(n_chunks):
        cols = slice(jc * cw, (jc + 1) * cw)
        seq_ref[jc, 0:8, :] = y[n:n + 8] * has_prev
        seq_ref[jc, 8:8 + n, :] = y[:n]
        seq_ref[jc, 8 + n:16 + n, :] = y[n + 8:n + 16] * has_next
        if jc + 1 < n_chunks:
            y = conv_cols(jc + 1)
        if jc < len(other):
            o_ref, oc, wc = other[jc]
            o_ref[:, oc:oc + cw] = jnp.dot(h[:n], w_ref[:, wc:wc + cw], preferred_element_type=F32)
        w = cw_ref[:, cols]
        acc = cb_ref[:, cols] + seq_ref[jc, 8:8 + n, :] * w[pad:pad + 1]
        for tap in range(SSD_CONV):
            if tap != pad:
                acc = acc + seq_ref[jc, pl.ds(8 - pad + tap, n), :] * w[tap:tap + 1]
        xbc_ref[:, cols] = _silu(acc)


def _proj_odd_call(g, s_rows, mod_l, w, conv_w8, conv_b, d_inner, conv_ch):
    blk = g.blk
    assert g.tm % blk == 0
    per_tile = g.tm // blk
    n = w.shape[1]
    dt_w = n - d_inner - conv_ch
    nrow8 = g.rows // 8
    per = blk // 8
    row = lambda i: (i, 0)
    return pl.pallas_call(
        functools.partial(_proj_odd_kernel, lat_blocks=g.lat_blocks, nb=g.nb, d_inner=d_inner,
                          conv_ch=conv_ch),
        grid=(g.rows // blk,),
        in_specs=[pl.BlockSpec((blk, g.d), row),
                  pl.BlockSpec((8, g.d), lambda i: (jnp.maximum(i * per - 1, 0), 0)),
                  pl.BlockSpec((8, g.d), lambda i: (jnp.minimum((i + 1) * per, nrow8 - 1), 0)),
                  pl.BlockSpec((None, 6, g.d), lambda i: (g.mod_row(i // per_tile), 0, 0)),
                  _resident((g.d, n)),
                  pl.BlockSpec((8, conv_ch), lambda i: (0, 0)),
                  pl.BlockSpec((1, conv_ch), lambda i: (0, 0))],
        out_specs=[pl.BlockSpec((blk, d_inner), row),
                   pl.BlockSpec((blk, conv_ch), row),
                   pl.BlockSpec((blk, dt_w), row)],
        out_shape=[jax.ShapeDtypeStruct((g.rows, d_inner), F32),
                   jax.ShapeDtypeStruct((g.rows, conv_ch), F32),
                   jax.ShapeDtypeStruct((g.rows, dt_w), F32)],
        scratch_shapes=[pltpu.VMEM((conv_ch // PROJ_CHUNK, blk + 16, PROJ_CHUNK), F32)],
        compiler_params=_cparams(("parallel",)),
        name="proj_odd_conv",
    )(s_rows, s_rows, s_rows, mod_l, w, conv_w8, conv_b)


def _ssd_kernel(xf, bmf, cmf, dtf, xb, bmb, cmb, dtb, bias_ref, alog_ref, dsk_ref, yf_ref, yb_ref,
                st_ref):
    s = pl.program_id(1)

    @pl.when(s == 0)
    def _():
        st_ref[...] = jnp.zeros_like(st_ref)

    n, ch = MIX_BLOCK, SSD_CHUNK
    gw = SSD_HEADS_PER_GROUP * SSD_HEAD_DIM
    ri = lax.broadcasted_iota(jnp.int32, (n, n), 0)
    ci = lax.broadcasted_iota(jnp.int32, (n, n), 1)
    same_chunk = (ri // ch) == (ci // ch)
    ri_c = lax.broadcasted_iota(jnp.int32, (ch, ch), 0)
    ci_c = lax.broadcasted_iota(jnp.int32, (ch, ch), 1)
    lane_c = lax.broadcasted_iota(jnp.int32, (ch, 128), 1)
    exp_row = lax.broadcasted_iota(jnp.int32, (128, gw), 0)
    exp_col = lax.broadcasted_iota(jnp.int32, (128, gw), 1) // SSD_HEAD_DIM

    dirs = ((xf, bmf, cmf, dtf, yf_ref), (xb, bmb, cmb, dtb, yb_ref))
    expand = [jnp.where(exp_row == exp_col + SSD_HEADS_PER_GROUP * d, 1.0, 0.0).astype(BF16)
              for d in range(2)]
    causal_blk = [jnp.where(jnp.logical_and(same_chunk, (ci >= ri) if d == 1 else (ci <= ri)),
                            1.0, 0.0).astype(BF16) for d in range(2)]
    tri = [(ci_c >= ri_c) if d == 1 else (ci_c <= ri_c) for d in range(2)]
    for gi in range(dsk_ref.shape[0]):
        _ssd_group(gi, dirs, bias_ref, alog_ref, dsk_ref, st_ref, expand, causal_blk, tri, lane_c)


def _ssd_group(gi, dirs, bias_ref, alog_ref, dsk_ref, st_ref, expand, causal_blk, tri, lane_c):
    ch = SSD_CHUNK
    nch = MIX_BLOCK // ch
    gw = SSD_HEADS_PER_GROUP * SSD_HEAD_DIM
    slab = slice(128 * gi, 128 * (gi + 1))
    chans = slice(gw * gi, gw * (gi + 1))
    a_neg = -jnp.exp(alog_ref[gi])
    dt, acs, acs_t, dt_t = [], [], [], []
    for d in range(2):
        z = dirs[d][3][:, slab] + bias_ref[gi]
        dt.append(jnp.maximum(z, 0.0) + jnp.log1p(jnp.exp(-jnp.abs(z))))
        acs.append(_split_dot(causal_blk[d], dt[d] * a_neg))
    for d in range(2):
        acs_t.append(acs[d].T)
        dt_t.append(dt[d].T)
    state = [st_ref[0, gi], st_ref[1, gi]]

    for i in range(nch):
        for d in range(2):
            x_ref, bm_ref, cm_ref, _, y_ref = dirs[d]
            rev = d == 1
            lane0 = SSD_HEADS_PER_GROUP * d
            c = nch - 1 - i if rev else i
            r0, r1 = c * ch, (c + 1) * ch
            xc = x_ref[r0:r1, chans]
            bmc = bm_ref[r0:r1, slab]
            acs_c = acs[d][r0:r1, :]
            acs_tc = acs_t[d][:, r0:r1]
            dt_tc = dt_t[d][:, r0:r1]
            cm16 = cm_ref[r0:r1, slab].astype(BF16)
            cb = lax.dot_general(cm16, bmc.astype(BF16), (((1,), (1,)), ((), ())),
                                 preferred_element_type=F32)
            a_last = acs_c[0:1, :] if rev else acs_c[ch - 1:ch, :]
            fac = jnp.concatenate([jnp.exp(acs_c), dt[d][r0:r1, :] * jnp.exp(a_last - acs_c)], axis=0)
            fac = jnp.dot(fac.astype(BF16), expand[d], preferred_element_type=F32)
            dec = _split_dot_r(jnp.broadcast_to(jnp.exp(a_last), (8, 128)), expand[d])[0:1]
            pairs = []
            for pp in range(SSD_HEADS_PER_GROUP // 2):
                mats = []
                for e in (2 * pp, 2 * pp + 1):
                    ln = lane0 + e
                    seg = jnp.exp(jnp.where(tri[d], acs_c[:, ln:ln + 1] - acs_tc[ln:ln + 1, :], -jnp.inf))
                    mats.append((cb * seg * dt_tc[ln:ln + 1, :]).astype(BF16))
                xp = xc[:, 128 * pp:128 * (pp + 1)]
                x2 = jnp.concatenate([jnp.where(lane_c < SSD_HEAD_DIM, xp, 0.0),
                                      jnp.where(lane_c >= SSD_HEAD_DIM, xp, 0.0)], axis=0).astype(BF16)
                pairs.append(jnp.dot(jnp.concatenate(mats, axis=1), x2, preferred_element_type=F32))
            y_off = jnp.dot(cm16, state[d].astype(BF16), preferred_element_type=F32) * fac[0:ch]
            y = jnp.concatenate(pairs, axis=1) + y_off
            if not rev:
                y = y + dsk_ref[gi] * xc
            y_ref[r0:r1, chans] = y
            upd = jnp.dot(bmc.T.astype(BF16), (xc * fac[ch:2 * ch]).astype(BF16),
                          preferred_element_type=F32)
            state[d] = dec * state[d] + upd
    st_ref[0, gi] = state[0]
    st_ref[1, gi] = state[1]


def _ssd_call(g, xbc, y_odd, dt_bias_slab, a_log_slab, d_skip, d_inner):
    blk = g.blk
    gw = SSD_HEADS_PER_GROUP * SSD_HEAD_DIM
    sw = SSD_GROUPS * 128
    assert d_inner % sw == 0
    bm0 = d_inner // sw
    cm0 = bm0 + 1

    def specs(block_fn):
        return [pl.BlockSpec((blk, d_inner), lambda b, s: (block_fn(b, s), 0)),
                pl.BlockSpec((blk, sw), lambda b, s: (block_fn(b, s), bm0)),
                pl.BlockSpec((blk, sw), lambda b, s: (block_fn(b, s), cm0)),
                pl.BlockSpec((blk, sw), lambda b, s: (block_fn(b, s), 0))]

    whole = lambda w: pl.BlockSpec((SSD_GROUPS, 1, w), lambda b, s: (0, 0, 0))
    return pl.pallas_call(
        _ssd_kernel,
        grid=(g.b, g.nb + 1),
        in_specs=specs(g.fwd_block) + specs(g.bwd_block) + [whole(128), whole(128), whole(gw)],
        out_specs=[pl.BlockSpec((blk, d_inner), lambda b, s: (g.fwd_block(b, s), 0)),
                   pl.BlockSpec((blk, d_inner), lambda b, s: (g.bwd_block(b, s), 0))],
        out_shape=[jax.ShapeDtypeStruct((g.rows, d_inner), F32)] * 2,
        scratch_shapes=[pltpu.VMEM((2, SSD_GROUPS, SSD_STATE, gw), F32)],
        compiler_params=_cparams(("parallel", "arbitrary")),
        name="ssd_scan",
    )(xbc, xbc, xbc, y_odd, xbc, xbc, xbc, y_odd, dt_bias_slab, a_log_slab,
      d_skip.reshape(SSD_GROUPS, 1, gw))


def _ffn_residual(x, mod_ref, win_ref, wout_ref, ln_ref, alpha, hidden):
    h = (x * (1.0 + mod_ref[4:5, :]) + mod_ref[3:4, :]).astype(BF16)
    step = -(-hidden // (FFN_CHUNKS * MXU_TILE)) * MXU_TILE
    bounds = [(c0, min(c0 + step, hidden)) for c0 in range(0, hidden, step)]

    def gate_up(c0, c1):
        return (jnp.dot(h, win_ref[:, c0:c1], preferred_element_type=F32),
                jnp.dot(h, win_ref[:, hidden + c0:hidden + c1], preferred_element_type=F32))

    acc = None
    gate, up = gate_up(*bounds[0])
    for j, (c0, c1) in enumerate(bounds):
        nxt = gate_up(*bounds[j + 1]) if j + 1 < len(bounds) else None
        act = (_silu(gate) * up).astype(BF16)
        part = jnp.dot(act, wout_ref[c0:c1, :], preferred_element_type=F32)
        acc = part if acc is None else acc + part
        if nxt is not None:
            gate, up = nxt
    v = alpha * x + mod_ref[5:6, :] * acc
    return _layer_norm(v, ln_ref[2:3, :], ln_ref[3:4, :])


def _post_even_kernel(x_ref, mod_ref, of_ref, ob_ref, g_ref, odc_ref, odl_ref, nw_ref, w_ref, ln_ref,
                      win_ref, wout_ref, o_ref, *, alpha, lat_tiles, hidden):
    o = of_ref[...] + ob_ref[...]
    gate = g_ref[...]
    nw = nw_ref[...]
    parts = []
    for h in range(GLA_HEADS):
        oh = o[:, h * GLA_DV:(h + 1) * GLA_DV]
        ms = jnp.mean(oh * oh, axis=-1, keepdims=True)
        parts.append(oh * lax.rsqrt(ms + RMS_EPS) * nw)
    gla = (jnp.concatenate(parts, axis=1) * _silu(gate)).astype(BF16)
    od = jnp.where(pl.program_id(0) < lat_tiles, odl_ref[...], odc_ref[...])
    mixin = jnp.concatenate([gla, od], axis=1)
    mix = jnp.dot(mixin, w_ref[...], preferred_element_type=F32)
    v = alpha * x_ref[...] + mod_ref[2:3, :] * mix
    x1 = _layer_norm(v, ln_ref[0:1, :], ln_ref[1:2, :])
    o_ref[...] = _ffn_residual(x1, mod_ref, win_ref, wout_ref, ln_ref, alpha, hidden)


def _post_even_call(g, s_rows, mod_l, o_f, o_b, ymain, od_ctx, od_lat, gla_nw, w_mix, ln4, w_in, w_out,
                    alpha, latent_only):
    tm, d = g.tm, g.d
    vw = EV_GLA_V
    n_tiles = g.lat_tiles if latent_only else g.n_tiles
    row = lambda i: (i, 0)
    return pl.pallas_call(
        functools.partial(_post_even_kernel, alpha=alpha, lat_tiles=g.lat_tiles, hidden=w_out.shape[0]),
        grid=(n_tiles,),
        in_specs=[pl.BlockSpec((tm, d), row),
                  pl.BlockSpec((None, 6, d), lambda i: (g.mod_row(i), 0, 0)),
                  pl.BlockSpec((tm, vw), row),
                  pl.BlockSpec((tm, vw), row),
                  pl.BlockSpec((tm, vw), lambda i: (i, 2 * EV_GLA_W // vw + 1)),
                  pl.BlockSpec((tm, vw), lambda i: (jnp.maximum(i - g.lat_tiles, 0), 0)),
                  pl.BlockSpec((tm, vw), lambda i: (jnp.minimum(i, g.lat_tiles - 1), 0)),
                  pl.BlockSpec((1, GLA_DV), lambda i: (0, 0)),
                  _resident(w_mix.shape),
                  pl.BlockSpec((4, d), lambda i: (0, 0)),
                  _resident(w_in.shape),
                  _resident(w_out.shape)],
        out_specs=pl.BlockSpec((tm, d), lambda i: (i, 0)),
        out_shape=jax.ShapeDtypeStruct((n_tiles * tm, d), F32),
        compiler_params=_cparams(("parallel",)),
        name="post_even",
    )(s_rows, mod_l, o_f, o_b, ymain, od_ctx, od_lat, gla_nw, w_mix, ln4, w_in, w_out)


def _post_odd_kernel(x_ref, mod_ref, yf_ref, yb_ref, z_ref, nw_ref, w_ref, ln_ref, win_ref, wout_ref,
                     o_ref, *, alpha, hidden):
    y = (yf_ref[...] + yb_ref[...]) * _silu(z_ref[...])
    gw = y.shape[1] // SSD_GROUPS
    parts = []
    for gi in range(SSD_GROUPS):
        yg = y[:, gi * gw:(gi + 1) * gw]
        ms = jnp.mean(yg * yg, axis=-1, keepdims=True)
        parts.append(yg * lax.rsqrt(ms + RMS_EPS))
    yn = (jnp.concatenate(parts, axis=1) * nw_ref[...]).astype(BF16)
    mix = jnp.dot(yn, w_ref[...], preferred_element_type=F32)
    v = alpha * x_ref[...] + mod_ref[2:3, :] * mix
    x1 = _layer_norm(v, ln_ref[0:1, :], ln_ref[1:2, :])
    o_ref[...] = _ffn_residual(x1, mod_ref, win_ref, wout_ref, ln_ref, alpha, hidden)


def _post_odd_call(g, s_rows, mod_l, y_f, y_b, z_gate, norm_w, w_mix, ln4, w_in, w_out, alpha,
                   latent_only):
    tm = g.tm // 2
    per_tile = g.tm // tm
    d = g.d
    di = w_mix.shape[0]
    n_tiles = (g.lat_tiles if latent_only else g.n_tiles) * per_tile
    row = lambda i: (i, 0)
    return pl.pallas_call(
        functools.partial(_post_odd_kernel, alpha=alpha, hidden=w_out.shape[0]),
        grid=(n_tiles,),
        in_specs=[pl.BlockSpec((tm, d), row),
                  pl.BlockSpec((None, 6, d), lambda i: (g.mod_row(i // per_tile), 0, 0)),
                  pl.BlockSpec((tm, di), row),
                  pl.BlockSpec((tm, di), row),
                  pl.BlockSpec((tm, di), row),
                  pl.BlockSpec((1, di), lambda i: (0, 0)),
                  _resident(w_mix.shape),
                  pl.BlockSpec((4, d), lambda i: (0, 0)),
                  _resident(w_in.shape),
                  _resident(w_out.shape)],
        out_specs=pl.BlockSpec((tm, d), lambda i: (i, 0)),
        out_shape=jax.ShapeDtypeStruct((n_tiles * tm, d), F32),
        compiler_params=_cparams(("parallel",)),
        name="post_odd",
    )(s_rows, mod_l, y_f, y_b, z_gate, norm_w, w_mix, ln4, w_in, w_out)


def kernel(x, c, ctx, c_ctx, mod_w, mod_b, ln_g, ln_b, ffn_w_in, ffn_w_out, ev_w_in, ev_w_out,
           gla_w_gate2, gla_b_gate, gla_norm_w, diff_lambda, diff_norm_w, ssd_w_in, ssd_conv_w,
           ssd_conv_b, ssd_dt_bias, ssd_a_log, ssd_d, ssd_norm_w, ssd_w_out):
    batch, seq, d = x.shape
    ctx_len = ctx.shape[1]
    depth = mod_w.shape[0]
    g = _Geom(batch, seq, ctx_len, d)
    alpha = (2 * depth) ** 0.25
    d_inner = ssd_w_out.shape[1]
    conv_ch = ssd_conv_w.shape[2]
    n_heads = ssd_d.shape[1]

    s_rows = jnp.concatenate([x.reshape(batch * seq, d), ctx.reshape(batch * ctx_len, d)], axis=0)
    cc = jnp.zeros((8, d), F32).at[:batch].set(c).at[batch].set(c_ctx)
    mod_all = _mod_call(cc, mod_w, mod_b).reshape(depth, 8, 6, d)
    rope_tab = _rope_table(g)

    for layer in range(depth):
        need_ctx = layer < depth - 1
        mod_l = mod_all[layer]
        ln4 = jnp.stack([ln_g[layer, 0], ln_b[layer, 0], ln_g[layer, 1], ln_b[layer, 1]])
        w_ffn_in = ffn_w_in[layer].astype(BF16)
        w_ffn_out = ffn_w_out[layer].astype(BF16)
        if layer % 2 == 0:
            e = layer // 2
            w = ev_w_in[e]
            w_my = jnp.concatenate([w[:, :EV_REAL_MAIN], jnp.zeros((d, 128 - 2 * GLA_RANK), w.dtype),
                                    w[:, EV_REAL_MAIN:]], axis=1).astype(BF16)
            ymain, k2, qt, vt = _proj_even_call(g, s_rows, mod_l, w_my, rope_tab)
            wg = jnp.zeros((2, 128, EV_GLA_W), F32)
            wg = wg.at[0, :GLA_RANK].set(gla_w_gate2[e, 0]).at[1, GLA_RANK:2 * GLA_RANK].set(gla_w_gate2[e, 1])
            o_f, o_b = _gla_call(g, ymain, wg.astype(BF16), gla_b_gate[e].reshape(2, 1, EV_GLA_W))
            lam_init = 0.8 - 0.6 * math.exp(-0.3 * layer)
            nw_col = diff_norm_w[e].reshape(DIFF_DV, 1)
            od_lat = _attn_call(g, k2, qt, vt, diff_lambda[e], nw_col, lam_init)
            od_ctx = _attn_ctx_call(g, k2, qt, vt, diff_lambda[e], nw_col, lam_init)
            s_rows = _post_even_call(g, s_rows, mod_l, o_f, o_b, ymain, od_ctx, od_lat,
                                     gla_norm_w[e].reshape(1, GLA_DV), ev_w_out[e].astype(BF16), ln4,
                                     w_ffn_in, w_ffn_out, alpha, latent_only=not need_ctx)
        else:
            o = layer // 2
            w = ssd_w_in[o]
            dt_col0 = d_inner + conv_ch
            hpg = SSD_HEADS_PER_GROUP
            slabs = []
            for gi in range(SSD_GROUPS):
                slabs += [w[:, dt_col0 + hpg * gi:dt_col0 + hpg * (gi + 1)],
                          w[:, dt_col0 + n_heads + hpg * gi:dt_col0 + n_heads + hpg * (gi + 1)],
                          jnp.zeros((d, 128 - 2 * hpg), w.dtype)]
            w_my = jnp.concatenate([w[:, :dt_col0]] + slabs, axis=1).astype(BF16)

            def slab_vec(v2):
                rows = [jnp.concatenate([v2[0, hpg * gi:hpg * (gi + 1)], v2[1, hpg * gi:hpg * (gi + 1)],
                                         jnp.zeros((128 - 2 * hpg,), F32)]) for gi in range(SSD_GROUPS)]
                return jnp.stack(rows).reshape(SSD_GROUPS, 1, 128)

            conv_w8 = jnp.concatenate([ssd_conv_w[o], jnp.zeros((8 - SSD_CONV, conv_ch), F32)], axis=0)
            z_gate, xbc, dt_raw = _proj_odd_call(g, s_rows, mod_l, w_my, conv_w8,
                                                 ssd_conv_b[o].reshape(1, conv_ch), d_inner, conv_ch)
            d_skip = jnp.repeat(ssd_d[o], SSD_HEAD_DIM)
            y_f, y_b = _ssd_call(g, xbc, dt_raw, slab_vec(ssd_dt_bias[o]), slab_vec(ssd_a_log[o]),
                                 d_skip, d_inner)
            s_rows = _post_odd_call(g, s_rows, mod_l, y_f, y_b, z_gate, ssd_norm_w[o].reshape(1, d_inner),
                                    ssd_w_out[o].astype(BF16), ln4, w_ffn_in, w_ffn_out, alpha,
                                    latent_only=not need_ctx)
    return s_rows.reshape(batch, seq, d)
```

```python
import functools
import math

import jax
import jax.numpy as jnp
import numpy as np
from jax import lax
from jax.experimental import pallas as pl
from jax.experimental.pallas import tpu as pltpu

F32 = jnp.float32
BF16 = jnp.bfloat16

GRID_W = 64
GLA_HEADS, GLA_DK, GLA_DV, GLA_RANK, GLA_CHUNK = 4, 64, 128, 16, 64
GLA_INV_TAU = 1.0 / 16.0
DIFF_HEADS, DIFF_DQK, DIFF_DV = 4, 64, 128
ROPE_BASE = 10000.0
SSD_HEAD_DIM, SSD_GROUPS, SSD_STATE, SSD_CONV, SSD_CHUNK = 64, 4, 128, 5, 128
SSD_HEADS_PER_GROUP = 8
LN_EPS = 1e-6
RMS_EPS = 1e-6

EV_GLA_W = GLA_HEADS * GLA_DK
EV_GLA_V = GLA_HEADS * GLA_DV
EV_MAIN = 2 * EV_GLA_W + 2 * EV_GLA_V + 128
EV_DIFF = DIFF_HEADS * 2 * DIFF_DQK
EV_REAL_MAIN = 2 * EV_GLA_W + 2 * EV_GLA_V + 2 * GLA_RANK

ROW_TILE = 512
MIX_BLOCK = 256
ATTN_TQ = 2048
ATTN_TK = 2048
ATTN_QG = 256
ATTN_KC = 128
ATTN_LAG_LIMIT = 50.0
ATTN_ONES = 16
LOG2E = 1.4426950408889634
PROJ_CHUNK = 512
MXU_TILE = 256
FFN_CHUNKS = 4
VMEM_LIMIT = 56 * 1024 * 1024


def _cparams(sem):
    return pltpu.CompilerParams(dimension_semantics=sem, vmem_limit_bytes=VMEM_LIMIT)


def _resident(shape):
    nd = len(shape)
    return pl.BlockSpec(shape, lambda *_: (0,) * nd, pipeline_mode=pl.Buffered(1))


def _silu(v):
    return v / (1.0 + jnp.exp(-v))


def _layer_norm(v, g, b):
    mu = jnp.mean(v, axis=-1, keepdims=True)
    d = v - mu
    var = jnp.mean(d * d, axis=-1, keepdims=True)
    return d * lax.rsqrt(var + LN_EPS) * g + b


def _split_dot(mat_bf16, v):
    hi = v.astype(BF16)
    lo = (v - hi.astype(F32)).astype(BF16)
    return (jnp.dot(mat_bf16, hi, preferred_element_type=F32)
            + jnp.dot(mat_bf16, lo, preferred_element_type=F32))


def _split_dot_r(v, mat_bf16):
    hi = v.astype(BF16)
    lo = (v - hi.astype(F32)).astype(BF16)
    return (jnp.dot(hi, mat_bf16, preferred_element_type=F32)
            + jnp.dot(lo, mat_bf16, preferred_element_type=F32))


def _mod_kernel(c_ref, w_ref, b_ref, o_ref):
    s = _silu(c_ref[...]).astype(BF16)
    o_ref[...] = jnp.dot(s, w_ref[...].astype(BF16), preferred_element_type=F32) + b_ref[...]


def _mod_call(cc, mod_w, mod_b):
    depth, d, n = mod_w.shape
    cw = d
    return pl.pallas_call(
        _mod_kernel,
        grid=(depth, n // cw),
        in_specs=[pl.BlockSpec((8, d), lambda l, j: (0, 0)),
                  pl.BlockSpec((None, d, cw), lambda l, j: (l, 0, j)),
                  pl.BlockSpec((None, 1, cw), lambda l, j: (l, 0, j))],
        out_specs=pl.BlockSpec((None, 8, cw), lambda l, j: (l, 0, j)),
        out_shape=jax.ShapeDtypeStruct((depth, 8, n), F32),
        compiler_params=_cparams(("parallel", "parallel")),
        name="mod_vectors",
    )(cc, mod_w, mod_b.reshape(depth, 1, n))


class _Geom:
    def __init__(self, batch, seq, ctx_len, d_model):
        self.b, self.t, self.c, self.d = batch, seq, ctx_len, d_model
        self.rows = batch * (seq + ctx_len)
        self.tm = min(ROW_TILE, batch * ctx_len)
        assert (batch * ctx_len) % self.tm == 0 and seq % self.tm == 0
        self.tiles_per_batch = seq // self.tm
        self.n_tiles = self.rows // self.tm
        self.lat_tiles = batch * seq // self.tm
        self.blk = MIX_BLOCK
        assert ctx_len == self.blk and seq % self.blk == 0
        self.nb = seq // self.blk
        self.lat_blocks = batch * self.nb

    def mod_row(self, tile):
        return jnp.where(tile < self.lat_tiles, tile // self.tiles_per_batch, self.b)

    def fwd_block(self, b, s):
        return jnp.where(s == 0, self.lat_blocks + b, self.nb * b + s - 1)

    def bwd_block(self, b, s):
        return jnp.where(s == 0, self.lat_blocks + b, self.nb * b + self.nb - s)


def _proj_even_kernel(x_ref, mod_ref, w_ref, rope_ref, ymain_ref, k2_ref, qt_ref, vt_ref):
    x = x_ref[...]
    h = (x * (1.0 + mod_ref[1:2, :]) + mod_ref[0:1, :]).astype(BF16)
    y = jnp.dot(h, w_ref[...], preferred_element_type=F32)
    ymain_ref[...] = y[:, :EV_MAIN]
    cos = rope_ref[:, 0:128]
    sin_up = rope_ref[:, 128:256]
    sin_dn = rope_ref[:, 256:384]

    def rope(t):
        return t * cos + pltpu.roll(t, 112, 1) * sin_up + pltpu.roll(t, 16, 1) * sin_dn

    for j in range(DIFF_HEADS):
        lo, hi = 128 * j, 128 * (j + 1)
        q = rope(y[:, EV_MAIN + lo:EV_MAIN + hi]) * (DIFF_DQK ** -0.5 * LOG2E)
        qt_ref[lo:hi, :] = q.T.astype(BF16)
        k = rope(y[:, EV_MAIN + EV_DIFF + lo:EV_MAIN + EV_DIFF + hi])
        k2_ref[:, lo:hi] = k.astype(BF16)
        v = y[:, EV_MAIN + 2 * EV_DIFF + lo:EV_MAIN + 2 * EV_DIFF + hi]
        vt_ref[lo:hi, :] = v.T.astype(BF16)


def _proj_even_call(g, s_rows, mod_l, w, rope_tab):
    tm, d = g.tm, g.d
    n = w.shape[1]
    def rope_idx(i):
        return jnp.where(i < g.lat_tiles, 1 + i % g.tiles_per_batch, 0)

    return pl.pallas_call(
        _proj_even_kernel,
        grid=(g.n_tiles,),
        in_specs=[pl.BlockSpec((tm, d), lambda i: (i, 0)),
                  pl.BlockSpec((None, 6, d), lambda i: (g.mod_row(i), 0, 0)),
                  _resident((d, n)),
                  pl.BlockSpec((tm, 384), lambda i: (rope_idx(i), 0))],
        out_specs=[pl.BlockSpec((tm, EV_MAIN), lambda i: (i, 0)),
                   pl.BlockSpec((tm, EV_DIFF), lambda i: (i, 0)),
                   pl.BlockSpec((EV_DIFF, tm), lambda i: (0, i)),
                   pl.BlockSpec((EV_DIFF, tm), lambda i: (0, i))],
        out_shape=[jax.ShapeDtypeStruct((g.rows, EV_MAIN), F32),
                   jax.ShapeDtypeStruct((g.rows, EV_DIFF), BF16),
                   jax.ShapeDtypeStruct((EV_DIFF, g.rows), BF16),
                   jax.ShapeDtypeStruct((EV_DIFF, g.rows), BF16)],
        compiler_params=_cparams(("parallel",)),
        name="proj_even",
    )(s_rows, mod_l, w, rope_tab)


def _rope_table(g):
    rows = g.t // GRID_W
    n_freq = DIFF_DQK // 4
    inv = np.float32(ROPE_BASE) ** (-np.arange(n_freq, dtype=np.float32) / np.float32(n_freq))
    ang_r = np.arange(rows, dtype=np.float32)[:, None] * inv
    ang_c = np.arange(GRID_W, dtype=np.float32)[:, None] * inv
    per_row = lambda tab: np.repeat(tab, GRID_W, axis=0)
    per_col = lambda tab: np.tile(tab, (rows, 1))
    cr, sr = per_row(np.cos(ang_r)), per_row(np.sin(ang_r))
    cc, sc = per_col(np.cos(ang_c)), per_col(np.sin(ang_c))
    z = np.zeros_like(cr)
    cos64 = np.concatenate([cr, cr, cc, cc], axis=-1)
    up64 = np.concatenate([-sr, z, -sc, z], axis=-1)
    dn64 = np.concatenate([z, sr, z, sc], axis=-1)
    tab = np.concatenate([cos64, cos64, up64, up64, dn64, dn64], axis=-1)
    ident = np.concatenate([np.ones((g.tm, 128), np.float32), np.zeros((g.tm, 256), np.float32)], axis=-1)
    return jnp.asarray(np.concatenate([ident, tab], axis=0).astype(np.float32))


def _gla_kernel(qf, kf, vf, rf, qb, kb, vb, rb, wg_ref, bg_ref, of_ref, ob_ref, st_ref):
    s = pl.program_id(1)

    @pl.when(s == 0)
    def _():
        st_ref[...] = jnp.zeros_like(st_ref)

    n = MIX_BLOCK
    nch = n // GLA_CHUNK
    ri = lax.broadcasted_iota(jnp.int32, (n, n), 0)
    ci = lax.broadcasted_iota(jnp.int32, (n, n), 1)
    same_chunk = (ri // GLA_CHUNK) == (ci // GLA_CHUNK)
    col_chunk = ci // GLA_CHUNK
    lane_head = lax.broadcasted_iota(jnp.int32, (1, EV_GLA_W), 1) // GLA_DK
    diag_blocks = ((lax.broadcasted_iota(jnp.int32, (EV_GLA_W, EV_GLA_V), 0) // GLA_DK)
                   == (lax.broadcasted_iota(jnp.int32, (EV_GLA_W, EV_GLA_V), 1) // GLA_DV))

    dirs = ((qf, kf, vf, rf, of_ref), (qb, kb, vb, rb, ob_ref))
    causal, la, qd, ki, kd, v16 = [], [], [], [], [], []
    for d, (q_ref, k_ref, v_ref, r_ref, _) in enumerate(dirs):
        causal.append(jnp.logical_and(same_chunk, (ci >= ri) if d == 1 else (ci <= ri)))
        z = jnp.dot(r_ref[...].astype(BF16), wg_ref[d], preferred_element_type=F32) + bg_ref[d]
        la.append((jnp.minimum(z, 0.0) - jnp.log1p(jnp.exp(-jnp.abs(z)))) * GLA_INV_TAU)
    for d, (q_ref, k_ref, v_ref, r_ref, _) in enumerate(dirs):
        cum = _split_dot(jnp.where(causal[d], 1.0, 0.0).astype(BF16), la[d])
        tot = jnp.concatenate(
            [jnp.broadcast_to(jnp.sum(la[d][c * GLA_CHUNK:(c + 1) * GLA_CHUNK], axis=0, keepdims=True),
                              (GLA_CHUNK, EV_GLA_W)) for c in range(nch)], axis=0)
        k = k_ref[...]
        v16.append(v_ref[...].astype(BF16))
        qd.append(q_ref[...] * jnp.exp(cum) * (GLA_DK ** -0.5))
        ki.append((k * jnp.exp(-cum)).astype(BF16))
        kd.append(k * jnp.exp(tot - cum))

    parts = ([], [])
    for h in range(GLA_HEADS):
        for d in range(2):
            qh = jnp.where(lane_head == h, qd[d], 0.0).astype(BF16)
            att = lax.dot_general(qh, ki[d], (((1,), (1,)), ((), ())), preferred_element_type=F32)
            att = jnp.where(causal[d], att, 0.0).astype(BF16)
            parts[d].append(jnp.dot(att, v16[d][:, h * GLA_DV:(h + 1) * GLA_DV],
                                    preferred_element_type=F32))

    la_t = [la[d].T for d in range(2)]
    kd_t = [kd[d].T for d in range(2)]
    qd16 = [qd[d].astype(BF16) for d in range(2)]
    state = [st_ref[0], st_ref[1]]
    o_inter = ([None] * nch, [None] * nch)
    for i in range(nch):
        for d in range(2):
            c = nch - 1 - i if d == 1 else i
            in_chunk = col_chunk == c
            o_inter[d][c] = jnp.dot(qd16[d][c * GLA_CHUNK:(c + 1) * GLA_CHUNK], state[d].astype(BF16),
                                    preferred_element_type=F32)
            kv = jnp.dot(jnp.where(in_chunk, kd_t[d], 0.0).astype(BF16), v16[d],
                         preferred_element_type=F32)
            decay = jnp.exp(jnp.sum(jnp.where(in_chunk, la_t[d], 0.0), axis=1, keepdims=True))
            state[d] = decay * state[d] + jnp.where(diag_blocks, kv, 0.0)
    for d in range(2):
        st_ref[d] = state[d]
        dirs[d][4][...] = jnp.concatenate(parts[d], axis=1) + jnp.concatenate(o_inter[d], axis=0)


def _gla_call(g, ymain, wg, bg):
    blk = g.blk
    qw, vw = EV_GLA_W, EV_GLA_V

    def specs(block_fn):
        return [pl.BlockSpec((blk, qw), lambda b, s: (block_fn(b, s), 0)),
                pl.BlockSpec((blk, qw), lambda b, s: (block_fn(b, s), 1)),
                pl.BlockSpec((blk, vw), lambda b, s: (block_fn(b, s), 1)),
                pl.BlockSpec((blk, 128), lambda b, s: (block_fn(b, s), (EV_MAIN - 128) // 128))]

    return pl.pallas_call(
        _gla_kernel,
        grid=(g.b, g.nb + 1),
        in_specs=specs(g.fwd_block) + specs(g.bwd_block) + [
            pl.BlockSpec((2, 128, qw), lambda b, s: (0, 0, 0)),
            pl.BlockSpec((2, 1, qw), lambda b, s: (0, 0, 0))],
        out_specs=[pl.BlockSpec((blk, vw), lambda b, s: (g.fwd_block(b, s), 0)),
                   pl.BlockSpec((blk, vw), lambda b, s: (g.bwd_block(b, s), 0))],
        out_shape=[jax.ShapeDtypeStruct((g.rows, vw), F32)] * 2,
        scratch_shapes=[pltpu.VMEM((2, qw, vw), F32)],
        compiler_params=_cparams(("parallel", "arbitrary")),
        name="gla_scan",
    )(ymain, ymain, ymain, ymain, ymain, ymain, ymain, ymain, wg, bg)


def _part8(x, op):
    rows = [x[i * 8:(i + 1) * 8] for i in range(x.shape[0] // 8)]
    while len(rows) > 1:
        rows = [op(rows[i], rows[i + 1]) for i in range(0, len(rows), 2)]
    return rows[0]


def _attn_kernel(lam_ref, nw_ref, qt_ref, kc_ref, vct_ref, *rest, lam_init, with_latent):
    if with_latent:
        kx_ref, vxt_ref, o_ref, m_sc, acc_sc, nxt_sc = rest
        ki = pl.program_id(3)
        last = pl.num_programs(3) - 1
    else:
        o_ref, m_sc, acc_sc = rest
    qt = qt_ref[...]
    tq = qt.shape[1]
    row = lax.broadcasted_iota(jnp.int32, qt.shape, 0)
    zero = jnp.zeros_like(qt)
    qt_maps = (jnp.where(row < DIFF_DQK, qt, zero), jnp.where(row >= DIFF_DQK, qt, zero))

    def with_ones(vt):
        return jnp.concatenate([vt, jnp.ones((ATTN_ONES, vt.shape[1]), vt.dtype)], axis=0)

    def exact_update(k, vt1):
        for m in range(2):
            st = jnp.dot(k, qt_maps[m], preferred_element_type=F32)
            m_old = m_sc[m]
            m_new = jnp.maximum(m_old, jnp.max(st, axis=0, keepdims=True))
            alpha = jnp.exp2(m_old - m_new)
            p = jnp.exp2(st - m_new)
            acc_sc[m] = alpha * acc_sc[m] + jnp.dot(vt1, p.astype(BF16), preferred_element_type=F32)
            m_sc[m] = m_new

    def lagged_update(k, vt1):
        tk = k.shape[0]
        refs = [m_sc[0], m_sc[1]]
        groups = [(m, gq) for m in range(2) for gq in range(tq // ATTN_QG)]
        cols_of = lambda gq: slice(gq * ATTN_QG, (gq + 1) * ATTN_QG)
        scores = lambda m, gq: jnp.dot(k, qt_maps[m][:, cols_of(gq)], preferred_element_type=F32)
        mx_g = ([], [])
        st = scores(*groups[0])
        for gi, (m, gq) in enumerate(groups):
            cols = cols_of(gq)
            rg = refs[m][:, cols]
            mx = None
            tiles = []
            for c in range(tk // ATTN_KC):
                tile = st[c * ATTN_KC:(c + 1) * ATTN_KC]
                tiles.append(jnp.exp2(tile - rg).astype(BF16))
                tmx = _part8(tile, jnp.maximum)
                mx = tmx if mx is None else jnp.maximum(mx, tmx)
            mx_g[m].append(mx)
            if gi + 1 < len(groups):
                st = scores(*groups[gi + 1])
            pv = jnp.dot(vt1, jnp.concatenate(tiles, axis=0), preferred_element_type=F32)
            nxt_sc[m, :, cols] = acc_sc[m, :, cols] + pv
        bmax = [jnp.max(jnp.concatenate(mx_g[m], axis=1), axis=0, keepdims=True) for m in range(2)]
        lag = jnp.maximum(jnp.max(bmax[0] - refs[0]), jnp.max(bmax[1] - refs[1]))
        safe = lag <= ATTN_LAG_LIMIT

        @pl.when(safe)
        def _():
            for m in range(2):
                r_new = jnp.maximum(refs[m], bmax[m])
                acc_sc[m] = nxt_sc[m] * jnp.exp2(refs[m] - r_new)
                m_sc[m] = r_new

        @pl.when(jnp.logical_not(safe))
        def _():
            exact_update(k, vt1)

    def init_and_ctx():
        m_sc[...] = jnp.full(m_sc.shape, -jnp.inf, F32)
        acc_sc[...] = jnp.zeros(acc_sc.shape, F32)
        exact_update(kc_ref[...], with_ones(vct_ref[...]))

    def finish():
        lp = lam_ref[...]
        lam = (jnp.exp(jnp.sum(lp[0:1] * lp[1:2], keepdims=True))
               - jnp.exp(jnp.sum(lp[2:3] * lp[3:4], keepdims=True)) + lam_init)
        a0, a1 = acc_sc[0], acc_sc[1]
        ot = (a0[:DIFF_DV] / a0[DIFF_DV:DIFF_DV + 1]
              - lam * (a1[:DIFF_DV] / a1[DIFF_DV:DIFF_DV + 1]))
        ms = jnp.mean(ot * ot, axis=0, keepdims=True)
        on = ot * lax.rsqrt(ms + RMS_EPS) * nw_ref[...] * (1.0 - lam_init)
        o_ref[...] = on.T.astype(o_ref.dtype)

    if with_latent:
        pl.when(ki == 0)(init_and_ctx)
        lagged_update(kx_ref[...], with_ones(vxt_ref[...]))
        pl.when(ki == last)(finish)
    else:
        init_and_ctx()
        finish()


def _attn_call(g, k2, qt, vt, lam_p, norm_w_col, lam_init):
    hd = 2 * DIFF_DQK
    tq = min(ATTN_TQ, g.t)
    tk = min(ATTN_TK, g.t)
    assert g.t % tq == 0 and g.t % tk == 0
    nq, nk = g.t // tq, g.t // tk
    c0 = g.lat_blocks
    scratch = lambda nqry: [pltpu.VMEM((2, 1, nqry), F32),
                            pltpu.VMEM((2, DIFF_DV + ATTN_ONES, nqry), F32)]
    common = [pl.BlockSpec((4, DIFF_DQK), lambda *_: (0, 0)),
              pl.BlockSpec((DIFF_DV, 1), lambda *_: (0, 0))]
    od = pl.pallas_call(
        functools.partial(_attn_kernel, lam_init=lam_init, with_latent=True),
        grid=(g.b, DIFF_HEADS, nq, nk),
        in_specs=common + [
            pl.BlockSpec((hd, tq), lambda b, h, qi, ki: (h, b * nq + qi)),
            pl.BlockSpec((g.c, hd), lambda b, h, qi, ki: (c0 + b, h)),
            pl.BlockSpec((DIFF_DV, g.c), lambda b, h, qi, ki: (h, c0 + b)),
            pl.BlockSpec((tk, hd), lambda b, h, qi, ki: (b * nk + ki, h)),
            pl.BlockSpec((DIFF_DV, tk), lambda b, h, qi, ki: (h, b * nk + ki))],
        out_specs=pl.BlockSpec((tq, DIFF_DV), lambda b, h, qi, ki: (b * nq + qi, h)),
        out_shape=jax.ShapeDtypeStruct((g.b * g.t, DIFF_HEADS * DIFF_DV), BF16),
        scratch_shapes=scratch(tq) + [pltpu.VMEM((2, DIFF_DV + ATTN_ONES, tq), F32)],
        compiler_params=_cparams(("parallel", "parallel", "parallel", "arbitrary")),
        name="diff_attn_latent",
    )(lam_p, norm_w_col, qt, k2, vt, k2, vt)
    return od


def _attn_ctx_call(g, k2, qt, vt, lam_p, norm_w_col, lam_init):
    hd = 2 * DIFF_DQK
    return pl.pallas_call(
        functools.partial(_attn_kernel, lam_init=lam_init, with_latent=False),
        grid=(g.b, DIFF_HEADS),
        in_specs=[pl.BlockSpec((4, DIFF_DQK), lambda *_: (0, 0)),
                  pl.BlockSpec((DIFF_DV, 1), lambda *_: (0, 0)),
                  pl.BlockSpec((hd, g.c), lambda b, h: (h, g.lat_blocks + b)),
                  pl.BlockSpec((g.c, hd), lambda b, h: (g.lat_blocks + b, h)),
                  pl.BlockSpec((DIFF_DV, g.c), lambda b, h: (h, g.lat_blocks + b))],
        out_specs=pl.BlockSpec((g.c, DIFF_DV), lambda b, h: (b, h)),
        out_shape=jax.ShapeDtypeStruct((g.b * g.c, DIFF_HEADS * DIFF_DV), BF16),
        scratch_shapes=[pltpu.VMEM((2, 1, g.c), F32),
                        pltpu.VMEM((2, DIFF_DV + ATTN_ONES, g.c), F32)],
        compiler_params=_cparams(("parallel", "parallel")),
        name="diff_attn_context",
    )(lam_p, norm_w_col, qt, k2, vt)


def _proj_odd_kernel(x_ref, xp_ref, xn_ref, mod_ref, w_ref, cw_ref, cb_ref, z_ref, xbc_ref, dt_ref,
                     seq_ref, *, lat_blocks, nb, d_inner, conv_ch):
    i = pl.program_id(0)
    j = i % nb
    is_lat = i < lat_blocks
    has_prev = jnp.logical_and(is_lat, j > 0).astype(F32)
    has_next = jnp.logical_and(is_lat, j < nb - 1).astype(F32)
    scale = 1.0 + mod_ref[1:2, :]
    shift = mod_ref[0:1, :]
    rows = jnp.concatenate([x_ref[...], xp_ref[...], xn_ref[...]], axis=0)
    h = (rows * scale + shift).astype(BF16)
    n = x_ref.shape[0]
    cw = PROJ_CHUNK
    dt_w = dt_ref.shape[1]
    other = [(z_ref, c0, c0, cw) for c0 in range(0, d_inner, cw)]
    other += [(dt_ref, c0, d_inner + conv_ch + c0, min(cw, dt_w)) for c0 in range(0, dt_w, cw)]
    pad = SSD_CONV // 2
    n_chunks = conv_ch // cw
    assert len(other) <= n_chunks
    conv_cols = lambda jc: jnp.dot(h, w_ref[:, d_inner + jc * cw:d_inner + (jc + 1) * cw],
                                   preferred_element_type=F32)
    y = conv_cols(0)
    for jc in range(n_chunks):
        cols = slice(jc * cw, (jc + 1) * cw)
        seq_ref[jc, 0:8, :] = y[n:n + 8] * has_prev
        seq_ref[jc, 8:8 + n, :] = y[:n]
        seq_ref[jc, 8 + n:16 + n, :] = y[n + 8:n + 16] * has_next
        if jc + 1 < n_chunks:
            y = conv_cols(jc + 1)
        if jc < len(other):
            o_ref, oc, wc, ow = other[jc]
            o_ref[:, oc:oc + ow] = jnp.dot(h[:n], w_ref[:, wc:wc + ow], preferred_element_type=F32)
        w = cw_ref[:, cols]
        acc = cb_ref[:, cols] + seq_ref[jc, 8:8 + n, :] * w[pad:pad + 1]
        for tap in range(SSD_CONV):
            if tap != pad:
                acc = acc + seq_ref[jc, pl.ds(8 - pad + tap, n), :] * w[tap:tap + 1]
        xbc_ref[:, cols] = _silu(acc)


def _proj_odd_call(g, s_rows, mod_l, w, conv_w8, conv_b, d_inner, conv_ch):
    blk = g.blk
    assert g.tm % blk == 0
    per_tile = g.tm // blk
    n = w.shape[1]
    dt_w = n - d_inner - conv_ch
    nrow8 = g.rows // 8
    per = blk // 8
    row = lambda i: (i, 0)
    return pl.pallas_call(
        functools.partial(_proj_odd_kernel, lat_blocks=g.lat_blocks, nb=g.nb, d_inner=d_inner,
                          conv_ch=conv_ch),
        grid=(g.rows // blk,),
        in_specs=[pl.BlockSpec((blk, g.d), row),
                  pl.BlockSpec((8, g.d), lambda i: (jnp.maximum(i * per - 1, 0), 0)),
                  pl.BlockSpec((8, g.d), lambda i: (jnp.minimum((i + 1) * per, nrow8 - 1), 0)),
                  pl.BlockSpec((None, 6, g.d), lambda i: (g.mod_row(i // per_tile), 0, 0)),
                  _resident((g.d, n)),
                  pl.BlockSpec((8, conv_ch), lambda i: (0, 0)),
                  pl.BlockSpec((1, conv_ch), lambda i: (0, 0))],
        out_specs=[pl.BlockSpec((blk, d_inner), row),
                   pl.BlockSpec((blk, conv_ch), row),
                   pl.BlockSpec((blk, dt_w), row)],
        out_shape=[jax.ShapeDtypeStruct((g.rows, d_inner), F32),
                   jax.ShapeDtypeStruct((g.rows, conv_ch), F32),
                   jax.ShapeDtypeStruct((g.rows, dt_w), F32)],
        scratch_shapes=[pltpu.VMEM((conv_ch // PROJ_CHUNK, blk + 16, PROJ_CHUNK), F32)],
        compiler_params=_cparams(("parallel",)),
        name="proj_odd_conv",
    )(s_rows, s_rows, s_rows, mod_l, w, conv_w8, conv_b)


def _ssd_kernel(xf, bmf, cmf, dtf, xb, bmb, cmb, dtb, bias_ref, alog_ref, dsk_ref, yf_ref, yb_ref,
                st_ref):
    s = pl.program_id(1)

    @pl.when(s == 0)
    def _():
        st_ref[...] = jnp.zeros_like(st_ref)

    n, ch = MIX_BLOCK, SSD_CHUNK
    gw = SSD_HEADS_PER_GROUP * SSD_HEAD_DIM
    ri = lax.broadcasted_iota(jnp.int32, (n, n), 0)
    ci = lax.broadcasted_iota(jnp.int32, (n, n), 1)
    same_chunk = (ri // ch) == (ci // ch)
    ri_c = lax.broadcasted_iota(jnp.int32, (ch, ch), 0)
    ci_c = lax.broadcasted_iota(jnp.int32, (ch, ch), 1)
    lane_c = lax.broadcasted_iota(jnp.int32, (ch, 128), 1)
    exp_row = lax.broadcasted_iota(jnp.int32, (128, gw), 0)
    exp_col = lax.broadcasted_iota(jnp.int32, (128, gw), 1) // SSD_HEAD_DIM

    dirs = ((xf, bmf, cmf, dtf, yf_ref), (xb, bmb, cmb, dtb, yb_ref))
    tri = [(ci_c >= ri_c) if d == 1 else (ci_c <= ri_c) for d in range(2)]
    n_heads = SSD_GROUPS * SSD_HEADS_PER_GROUP
    a_neg = -jnp.exp(alog_ref[...])
    dt, acs, acs_t, dt_t = [], [], [], []
    for d in range(2):
        z = dirs[d][3][...] + bias_ref[...]
        dt.append(jnp.maximum(z, 0.0) + jnp.log1p(jnp.exp(-jnp.abs(z))))
        causal_blk = jnp.logical_and(same_chunk, (ci >= ri) if d == 1 else (ci <= ri))
        acs.append(_split_dot(jnp.where(causal_blk, 1.0, 0.0).astype(BF16), dt[d] * a_neg))
    for d in range(2):
        acs_t.append(acs[d].T)
        dt_t.append(dt[d].T)
    fac_tok, dec_row = ([], []), ([], [])
    for d in range(2):
        for c in range(n // ch):
            acs_c = acs[d][c * ch:(c + 1) * ch, :]
            a_last = acs_c[0:1, :] if d == 1 else acs_c[ch - 1:ch, :]
            fac_tok[d].append(jnp.concatenate(
                [jnp.exp(acs_c), dt[d][c * ch:(c + 1) * ch, :] * jnp.exp(a_last - acs_c)], axis=0).astype(BF16))
            dec_row[d].append(jnp.broadcast_to(jnp.exp(a_last), (8, 128)))
    for gi in range(SSD_GROUPS):
        lane0 = [n_heads * d + SSD_HEADS_PER_GROUP * gi for d in range(2)]
        expand = [jnp.where(exp_row == exp_col + lane0[d], 1.0, 0.0).astype(BF16) for d in range(2)]
        _ssd_group(gi, dirs, dsk_ref, st_ref, acs, acs_t, dt_t, fac_tok, dec_row, lane0, expand, tri,
                   lane_c)


def _ssd_group(gi, dirs, dsk_ref, st_ref, acs, acs_t, dt_t, fac_tok, dec_row, lane0s, expand, tri,
               lane_c):
    ch = SSD_CHUNK
    nch = MIX_BLOCK // ch
    gw = SSD_HEADS_PER_GROUP * SSD_HEAD_DIM
    slab = slice(128 * gi, 128 * (gi + 1))
    chans = slice(gw * gi, gw * (gi + 1))
    state = [st_ref[0, gi], st_ref[1, gi]]

    for i in range(nch):
        for d in range(2):
            x_ref, bm_ref, cm_ref, _, y_ref = dirs[d]
            rev = d == 1
            lane0 = lane0s[d]
            c = nch - 1 - i if rev else i
            r0, r1 = c * ch, (c + 1) * ch
            xc = x_ref[r0:r1, chans]
            bmc = bm_ref[r0:r1, slab]
            acs_c = acs[d][r0:r1, :]
            acs_tc = acs_t[d][:, r0:r1]
            dt_tc = dt_t[d][:, r0:r1]
            cm16 = cm_ref[r0:r1, slab].astype(BF16)
            cb = lax.dot_general(cm16, bmc.astype(BF16), (((1,), (1,)), ((), ())),
                                 preferred_element_type=F32)
            fac = jnp.dot(fac_tok[d][c], expand[d], preferred_element_type=F32)
            dec = _split_dot_r(dec_row[d][c], expand[d])[0:1]
            pairs = []
            for pp in range(SSD_HEADS_PER_GROUP // 2):
                mats = []
                for e in (2 * pp, 2 * pp + 1):
                    ln = lane0 + e
                    seg = jnp.exp(jnp.where(tri[d], acs_c[:, ln:ln + 1] - acs_tc[ln:ln + 1, :], -jnp.inf))
                    mats.append((cb * seg * dt_tc[ln:ln + 1, :]).astype(BF16))
                xp = xc[:, 128 * pp:128 * (pp + 1)]
                x2 = jnp.concatenate([jnp.where(lane_c < SSD_HEAD_DIM, xp, 0.0),
                                      jnp.where(lane_c >= SSD_HEAD_DIM, xp, 0.0)], axis=0).astype(BF16)
                pairs.append(jnp.dot(jnp.concatenate(mats, axis=1), x2, preferred_element_type=F32))
            y_off = jnp.dot(cm16, state[d].astype(BF16), preferred_element_type=F32) * fac[0:ch]
            y = jnp.concatenate(pairs, axis=1) + y_off
            if not rev:
                y = y + dsk_ref[gi] * xc
            y_ref[r0:r1, chans] = y
            upd = jnp.dot(bmc.T.astype(BF16), (xc * fac[ch:2 * ch]).astype(BF16),
                          preferred_element_type=F32)
            state[d] = dec * state[d] + upd
    st_ref[0, gi] = state[0]
    st_ref[1, gi] = state[1]


def _ssd_call(g, xbc, dt_raw, dt_bias_row, a_log_row, d_skip, d_inner):
    blk = g.blk
    gw = SSD_HEADS_PER_GROUP * SSD_HEAD_DIM
    sw = SSD_GROUPS * 128
    assert d_inner % sw == 0
    bm0 = d_inner // sw
    cm0 = bm0 + 1

    def specs(block_fn):
        return [pl.BlockSpec((blk, d_inner), lambda b, s: (block_fn(b, s), 0)),
                pl.BlockSpec((blk, sw), lambda b, s: (block_fn(b, s), bm0)),
                pl.BlockSpec((blk, sw), lambda b, s: (block_fn(b, s), cm0)),
                pl.BlockSpec((blk, 128), lambda b, s: (block_fn(b, s), 0))]

    row128 = pl.BlockSpec((1, 128), lambda b, s: (0, 0))
    return pl.pallas_call(
        _ssd_kernel,
        grid=(g.b, g.nb + 1),
        in_specs=specs(g.fwd_block) + specs(g.bwd_block) + [
            row128, row128, pl.BlockSpec((SSD_GROUPS, 1, gw), lambda b, s: (0, 0, 0))],
        out_specs=[pl.BlockSpec((blk, d_inner), lambda b, s: (g.fwd_block(b, s), 0)),
                   pl.BlockSpec((blk, d_inner), lambda b, s: (g.bwd_block(b, s), 0))],
        out_shape=[jax.ShapeDtypeStruct((g.rows, d_inner), F32)] * 2,
        scratch_shapes=[pltpu.VMEM((2, SSD_GROUPS, SSD_STATE, gw), F32)],
        compiler_params=_cparams(("parallel", "arbitrary")),
        name="ssd_scan",
    )(xbc, xbc, xbc, dt_raw, xbc, xbc, xbc, dt_raw, dt_bias_row, a_log_row,
      d_skip.reshape(SSD_GROUPS, 1, gw))


def _ffn_residual(x, mod_ref, win_ref, wout_ref, ln_ref, alpha, hidden):
    h = (x * (1.0 + mod_ref[4:5, :]) + mod_ref[3:4, :]).astype(BF16)
    step = -(-hidden // (FFN_CHUNKS * MXU_TILE)) * MXU_TILE
    bounds = [(c0, min(c0 + step, hidden)) for c0 in range(0, hidden, step)]

    def gate_up(c0, c1):
        return (jnp.dot(h, win_ref[:, c0:c1], preferred_element_type=F32),
                jnp.dot(h, win_ref[:, hidden + c0:hidden + c1], preferred_element_type=F32))

    acc = None
    gate, up = gate_up(*bounds[0])
    for j, (c0, c1) in enumerate(bounds):
        nxt = gate_up(*bounds[j + 1]) if j + 1 < len(bounds) else None
        act = (_silu(gate) * up).astype(BF16)
        part = jnp.dot(act, wout_ref[c0:c1, :], preferred_element_type=F32)
        acc = part if acc is None else acc + part
        if nxt is not None:
            gate, up = nxt
    v = alpha * x + mod_ref[5:6, :] * acc
    return _layer_norm(v, ln_ref[2:3, :], ln_ref[3:4, :])


def _post_even_kernel(x_ref, mod_ref, of_ref, ob_ref, g_ref, odc_ref, odl_ref, nw_ref, w_ref, ln_ref,
                      win_ref, wout_ref, o_ref, *, alpha, lat_tiles, hidden):
    o = of_ref[...] + ob_ref[...]
    gate = g_ref[...]
    nw = nw_ref[...]
    parts = []
    for h in range(GLA_HEADS):
        oh = o[:, h * GLA_DV:(h + 1) * GLA_DV]
        ms = jnp.mean(oh * oh, axis=-1, keepdims=True)
        parts.append(oh * lax.rsqrt(ms + RMS_EPS) * nw)
    gla = (jnp.concatenate(parts, axis=1) * _silu(gate)).astype(BF16)
    od = jnp.where(pl.program_id(0) < lat_tiles, odl_ref[...], odc_ref[...])
    mixin = jnp.concatenate([gla, od], axis=1)
    mix = jnp.dot(mixin, w_ref[...], preferred_element_type=F32)
    v = alpha * x_ref[...] + mod_ref[2:3, :] * mix
    x1 = _layer_norm(v, ln_ref[0:1, :], ln_ref[1:2, :])
    o_ref[...] = _ffn_residual(x1, mod_ref, win_ref, wout_ref, ln_ref, alpha, hidden)


def _post_even_call(g, s_rows, mod_l, o_f, o_b, ymain, od_ctx, od_lat, gla_nw, w_mix, ln4, w_in, w_out,
                    alpha, latent_only):
    tm, d = g.tm, g.d
    vw = EV_GLA_V
    n_tiles = g.lat_tiles if latent_only else g.n_tiles
    row = lambda i: (i, 0)
    return pl.pallas_call(
        functools.partial(_post_even_kernel, alpha=alpha, lat_tiles=g.lat_tiles, hidden=w_out.shape[0]),
        grid=(n_tiles,),
        in_specs=[pl.BlockSpec((tm, d), row),
                  pl.BlockSpec((None, 6, d), lambda i: (g.mod_row(i), 0, 0)),
                  pl.BlockSpec((tm, vw), row),
                  pl.BlockSpec((tm, vw), row),
                  pl.BlockSpec((tm, vw), lambda i: (i, 2 * EV_GLA_W // vw + 1)),
                  pl.BlockSpec((tm, vw), lambda i: (jnp.maximum(i - g.lat_tiles, 0), 0)),
                  pl.BlockSpec((tm, vw), lambda i: (jnp.minimum(i, g.lat_tiles - 1), 0)),
                  pl.BlockSpec((1, GLA_DV), lambda i: (0, 0)),
                  _resident(w_mix.shape),
                  pl.BlockSpec((4, d), lambda i: (0, 0)),
                  _resident(w_in.shape),
                  _resident(w_out.shape)],
        out_specs=pl.BlockSpec((tm, d), lambda i: (i, 0)),
        out_shape=jax.ShapeDtypeStruct((n_tiles * tm, d), F32),
        compiler_params=_cparams(("parallel",)),
        name="post_even",
    )(s_rows, mod_l, o_f, o_b, ymain, od_ctx, od_lat, gla_nw, w_mix, ln4, w_in, w_out)


def _post_odd_kernel(x_ref, mod_ref, yf_ref, yb_ref, z_ref, nw_ref, w_ref, ln_ref, win_ref, wout_ref,
                     o_ref, *, alpha, hidden):
    y = (yf_ref[...] + yb_ref[...]) * _silu(z_ref[...])
    gw = y.shape[1] // SSD_GROUPS
    parts = []
    for gi in range(SSD_GROUPS):
        yg = y[:, gi * gw:(gi + 1) * gw]
        ms = jnp.mean(yg * yg, axis=-1, keepdims=True)
        parts.append(yg * lax.rsqrt(ms + RMS_EPS))
    yn = (jnp.concatenate(parts, axis=1) * nw_ref[...]).astype(BF16)
    mix = jnp.dot(yn, w_ref[...], preferred_element_type=F32)
    v = alpha * x_ref[...] + mod_ref[2:3, :] * mix
    x1 = _layer_norm(v, ln_ref[0:1, :], ln_ref[1:2, :])
    o_ref[...] = _ffn_residual(x1, mod_ref, win_ref, wout_ref, ln_ref, alpha, hidden)


def _post_odd_call(g, s_rows, mod_l, y_f, y_b, z_gate, norm_w, w_mix, ln4, w_in, w_out, alpha,
                   latent_only):
    tm = g.tm // 2
    per_tile = g.tm // tm
    d = g.d
    di = w_mix.shape[0]
    n_tiles = (g.lat_tiles if latent_only else g.n_tiles) * per_tile
    row = lambda i: (i, 0)
    return pl.pallas_call(
        functools.partial(_post_odd_kernel, alpha=alpha, hidden=w_out.shape[0]),
        grid=(n_tiles,),
        in_specs=[pl.BlockSpec((tm, d), row),
                  pl.BlockSpec((None, 6, d), lambda i: (g.mod_row(i // per_tile), 0, 0)),
                  pl.BlockSpec((tm, di), row),
                  pl.BlockSpec((tm, di), row),
                  pl.BlockSpec((tm, di), row),
                  pl.BlockSpec((1, di), lambda i: (0, 0)),
                  _resident(w_mix.shape),
                  pl.BlockSpec((4, d), lambda i: (0, 0)),
                  _resident(w_in.shape),
                  _resident(w_out.shape)],
        out_specs=pl.BlockSpec((tm, d), lambda i: (i, 0)),
        out_shape=jax.ShapeDtypeStruct((n_tiles * tm, d), F32),
        compiler_params=_cparams(("parallel",)),
        name="post_odd",
    )(s_rows, mod_l, y_f, y_b, z_gate, norm_w, w_mix, ln4, w_in, w_out)


def kernel(x, c, ctx, c_ctx, mod_w, mod_b, ln_g, ln_b, ffn_w_in, ffn_w_out, ev_w_in, ev_w_out,
           gla_w_gate2, gla_b_gate, gla_norm_w, diff_lambda, diff_norm_w, ssd_w_in, ssd_conv_w,
           ssd_conv_b, ssd_dt_bias, ssd_a_log, ssd_d, ssd_norm_w, ssd_w_out):
    batch, seq, d = x.shape
    ctx_len = ctx.shape[1]
    depth = mod_w.shape[0]
    g = _Geom(batch, seq, ctx_len, d)
    alpha = (2 * depth) ** 0.25
    d_inner = ssd_w_out.shape[1]
    conv_ch = ssd_conv_w.shape[2]
    n_heads = ssd_d.shape[1]

    s_rows = jnp.concatenate([x.reshape(batch * seq, d), ctx.reshape(batch * ctx_len, d)], axis=0)
    cc = jnp.zeros((8, d), F32).at[:batch].set(c).at[batch].set(c_ctx)
    mod_all = _mod_call(cc, mod_w, mod_b).reshape(depth, 8, 6, d)
    rope_tab = _rope_table(g)

    for layer in range(depth):
        need_ctx = layer < depth - 1
        mod_l = mod_all[layer]
        ln4 = jnp.stack([ln_g[layer, 0], ln_b[layer, 0], ln_g[layer, 1], ln_b[layer, 1]])
        w_ffn_in = ffn_w_in[layer].astype(BF16)
        w_ffn_out = ffn_w_out[layer].astype(BF16)
        if layer % 2 == 0:
            e = layer // 2
            w = ev_w_in[e]
            w_my = jnp.concatenate([w[:, :EV_REAL_MAIN], jnp.zeros((d, 128 - 2 * GLA_RANK), w.dtype),
                                    w[:, EV_REAL_MAIN:]], axis=1).astype(BF16)
            ymain, k2, qt, vt = _proj_even_call(g, s_rows, mod_l, w_my, rope_tab)
            wg = jnp.zeros((2, 128, EV_GLA_W), F32)
            wg = wg.at[0, :GLA_RANK].set(gla_w_gate2[e, 0]).at[1, GLA_RANK:2 * GLA_RANK].set(gla_w_gate2[e, 1])
            o_f, o_b = _gla_call(g, ymain, wg.astype(BF16), gla_b_gate[e].reshape(2, 1, EV_GLA_W))
            lam_init = 0.8 - 0.6 * math.exp(-0.3 * layer)
            nw_col = diff_norm_w[e].reshape(DIFF_DV, 1)
            od_lat = _attn_call(g, k2, qt, vt, diff_lambda[e], nw_col, lam_init)
            od_ctx = _attn_ctx_call(g, k2, qt, vt, diff_lambda[e], nw_col, lam_init)
            s_rows = _post_even_call(g, s_rows, mod_l, o_f, o_b, ymain, od_ctx, od_lat,
                                     gla_norm_w[e].reshape(1, GLA_DV), ev_w_out[e].astype(BF16), ln4,
                                     w_ffn_in, w_ffn_out, alpha, latent_only=not need_ctx)
        else:
            o = layer // 2
            w = ssd_w_in[o]
            assert n_heads == SSD_GROUPS * SSD_HEADS_PER_GROUP and 2 * n_heads <= 128
            w_my = jnp.pad(w.astype(BF16), ((0, 0), (0, 128 - 2 * n_heads)))

            def lane_row(v2):
                return jnp.pad(v2.reshape(1, 2 * n_heads), ((0, 0), (0, 128 - 2 * n_heads)))

            conv_w8 = jnp.concatenate([ssd_conv_w[o], jnp.zeros((8 - SSD_CONV, conv_ch), F32)], axis=0)
            z_gate, xbc, dt_raw = _proj_odd_call(g, s_rows, mod_l, w_my, conv_w8,
                                                 ssd_conv_b[o].reshape(1, conv_ch), d_inner, conv_ch)
            d_skip = jnp.repeat(ssd_d[o], SSD_HEAD_DIM)
            y_f, y_b = _ssd_call(g, xbc, dt_raw, lane_row(ssd_dt_bias[o]), lane_row(ssd_a_log[o]),
                                 d_skip, d_inner)
            s_rows = _post_odd_call(g, s_rows, mod_l, y_f, y_b, z_gate, ssd_norm_w[o].reshape(1, d_inner),
                                    ssd_w_out[o].astype(BF16), ln4, w_ffn_in, w_ffn_out, alpha,
                                    latent_only=not need_ctx)
    return s_rows.reshape(batch, seq, d)
```

```python
import functools
import math

import jax
import jax.numpy as jnp
import numpy as np
from jax import lax
from jax.experimental import pallas as pl
from jax.experimental.pallas import tpu as pltpu

F32 = jnp.float32
BF16 = jnp.bfloat16

GRID_W = 64
GLA_HEADS, GLA_DK, GLA_DV, GLA_RANK, GLA_CHUNK = 4, 64, 128, 16, 64
GLA_INV_TAU = 1.0 / 16.0
DIFF_HEADS, DIFF_DQK, DIFF_DV = 4, 64, 128
ROPE_BASE = 10000.0
SSD_HEAD_DIM, SSD_GROUPS, SSD_STATE, SSD_CONV, SSD_CHUNK = 64, 4, 128, 5, 128
SSD_HEADS_PER_GROUP = 8
LN_EPS = 1e-6
RMS_EPS = 1e-6

EV_GLA_W = GLA_HEADS * GLA_DK
EV_GLA_V = GLA_HEADS * GLA_DV
EV_MAIN = 2 * EV_GLA_W + 2 * EV_GLA_V + 128
EV_DIFF = DIFF_HEADS * 2 * DIFF_DQK
EV_REAL_MAIN = 2 * EV_GLA_W + 2 * EV_GLA_V + 2 * GLA_RANK

ROW_TILE = 512
MIX_BLOCK = 256
ATTN_TQ = 2048
ATTN_TK = 2048
ATTN_QG = 256
ATTN_KC = 128
ATTN_LAG_LIMIT = 50.0
ATTN_ONES = 16
LOG2E = 1.4426950408889634
PROJ_CHUNK = 256
MXU_TILE = 256
FFN_CHUNKS = 4
VMEM_LIMIT = 56 * 1024 * 1024


def _cparams(sem):
    return pltpu.CompilerParams(dimension_semantics=sem, vmem_limit_bytes=VMEM_LIMIT)


def _resident(shape):
    nd = len(shape)
    return pl.BlockSpec(shape, lambda *_: (0,) * nd, pipeline_mode=pl.Buffered(1))


def _silu(v):
    return v / (1.0 + jnp.exp(-v))


def _layer_norm(v, g, b):
    mu = jnp.mean(v, axis=-1, keepdims=True)
    d = v - mu
    var = jnp.mean(d * d, axis=-1, keepdims=True)
    return d * lax.rsqrt(var + LN_EPS) * g + b


def _split_dot(mat_bf16, v):
    hi = v.astype(BF16)
    lo = (v - hi.astype(F32)).astype(BF16)
    return (jnp.dot(mat_bf16, hi, preferred_element_type=F32)
            + jnp.dot(mat_bf16, lo, preferred_element_type=F32))


def _split_dot_r(v, mat_bf16):
    hi = v.astype(BF16)
    lo = (v - hi.astype(F32)).astype(BF16)
    return (jnp.dot(hi, mat_bf16, preferred_element_type=F32)
            + jnp.dot(lo, mat_bf16, preferred_element_type=F32))


def _mod_kernel(c_ref, w_ref, b_ref, o_ref):
    s = _silu(c_ref[...]).astype(BF16)
    o_ref[...] = jnp.dot(s, w_ref[...].astype(BF16), preferred_element_type=F32) + b_ref[...]


def _mod_call(cc, mod_w, mod_b):
    depth, d, n = mod_w.shape
    cw = d
    return pl.pallas_call(
        _mod_kernel,
        grid=(depth, n // cw),
        in_specs=[pl.BlockSpec((8, d), lambda l, j: (0, 0)),
                  pl.BlockSpec((None, d, cw), lambda l, j: (l, 0, j)),
                  pl.BlockSpec((None, 1, cw), lambda l, j: (l, 0, j))],
        out_specs=pl.BlockSpec((None, 8, cw), lambda l, j: (l, 0, j)),
        out_shape=jax.ShapeDtypeStruct((depth, 8, n), F32),
        compiler_params=_cparams(("parallel", "parallel")),
        name="mod_vectors",
    )(cc, mod_w, mod_b.reshape(depth, 1, n))


class _Geom:
    def __init__(self, batch, seq, ctx_len, d_model):
        self.b, self.t, self.c, self.d = batch, seq, ctx_len, d_model
        self.rows = batch * (seq + ctx_len)
        self.tm = min(ROW_TILE, batch * ctx_len)
        assert (batch * ctx_len) % self.tm == 0 and seq % self.tm == 0
        self.tiles_per_batch = seq // self.tm
        self.n_tiles = self.rows // self.tm
        self.lat_tiles = batch * seq // self.tm
        self.blk = MIX_BLOCK
        assert ctx_len == self.blk and seq % self.blk == 0
        self.nb = seq // self.blk
        self.lat_blocks = batch * self.nb

    def mod_row(self, tile):
        return jnp.where(tile < self.lat_tiles, tile // self.tiles_per_batch, self.b)

    def fwd_block(self, b, s):
        return jnp.where(s == 0, self.lat_blocks + b, self.nb * b + s - 1)

    def bwd_block(self, b, s):
        return jnp.where(s == 0, self.lat_blocks + b, self.nb * b + self.nb - s)


def _row_inputs(g, rows, tm):
    if isinstance(rows, tuple):
        lat_tiles = g.lat_tiles * (g.tm // tm)
        return ([pl.BlockSpec((tm, g.d), lambda i: (jnp.minimum(i, lat_tiles - 1), 0)),
                 pl.BlockSpec((tm, g.d), lambda i: (jnp.maximum(i - lat_tiles, 0), 0))], list(rows))
    return [pl.BlockSpec((tm, g.d), lambda i: (i, 0))], [rows]


def _row_tile(x_refs, lat_tiles):
    if len(x_refs) == 2:
        return jnp.where(pl.program_id(0) < lat_tiles, x_refs[0][...], x_refs[1][...])
    return x_refs[0][...]


def _proj_even_kernel(*refs, n_x, lat_tiles):
    mod_ref, w_ref, rope_ref, ymain_ref, k2_ref, qt_ref, vt_ref = refs[n_x:]
    x = _row_tile(refs[:n_x], lat_tiles)
    h = (x * (1.0 + mod_ref[1:2, :]) + mod_ref[0:1, :]).astype(BF16)
    y = jnp.dot(h, w_ref[...], preferred_element_type=F32)
    ymain_ref[...] = y[:, :EV_MAIN]
    cos = rope_ref[:, 0:128]
    sin_up = rope_ref[:, 128:256]
    sin_dn = rope_ref[:, 256:384]

    def rope(t):
        return t * cos + pltpu.roll(t, 112, 1) * sin_up + pltpu.roll(t, 16, 1) * sin_dn

    for j in range(DIFF_HEADS):
        lo, hi = 128 * j, 128 * (j + 1)
        q = rope(y[:, EV_MAIN + lo:EV_MAIN + hi]) * (DIFF_DQK ** -0.5 * LOG2E)
        qt_ref[lo:hi, :] = q.T.astype(BF16)
        k = rope(y[:, EV_MAIN + EV_DIFF + lo:EV_MAIN + EV_DIFF + hi])
        k2_ref[:, lo:hi] = k.astype(BF16)
        v = y[:, EV_MAIN + 2 * EV_DIFF + lo:EV_MAIN + 2 * EV_DIFF + hi]
        vt_ref[lo:hi, :] = v.T.astype(BF16)


def _proj_even_call(g, s_rows, mod_l, w, rope_tab):
    tm, d = g.tm, g.d
    n = w.shape[1]
    def rope_idx(i):
        return jnp.where(i < g.lat_tiles, 1 + i % g.tiles_per_batch, 0)

    x_specs, x_args = _row_inputs(g, s_rows, tm)
    return pl.pallas_call(
        functools.partial(_proj_even_kernel, n_x=len(x_args), lat_tiles=g.lat_tiles),
        grid=(g.n_tiles,),
        in_specs=x_specs + [
            pl.BlockSpec((None, 6, d), lambda i: (g.mod_row(i), 0, 0)),
            _resident((d, n)),
            pl.BlockSpec((tm, 384), lambda i: (rope_idx(i), 0))],
        out_specs=[pl.BlockSpec((tm, EV_MAIN), lambda i: (i, 0)),
                   pl.BlockSpec((tm, EV_DIFF), lambda i: (i, 0)),
                   pl.BlockSpec((EV_DIFF, tm), lambda i: (0, i)),
                   pl.BlockSpec((EV_DIFF, tm), lambda i: (0, i))],
        out_shape=[jax.ShapeDtypeStruct((g.rows, EV_MAIN), F32),
                   jax.ShapeDtypeStruct((g.rows, EV_DIFF), BF16),
                   jax.ShapeDtypeStruct((EV_DIFF, g.rows), BF16),
                   jax.ShapeDtypeStruct((EV_DIFF, g.rows), BF16)],
        compiler_params=_cparams(("parallel",)),
        name="proj_even",
    )(*x_args, mod_l, w, rope_tab)


def _rope_table(g):
    rows = g.t // GRID_W
    n_freq = DIFF_DQK // 4
    inv = np.float32(ROPE_BASE) ** (-np.arange(n_freq, dtype=np.float32) / np.float32(n_freq))
    ang_r = np.arange(rows, dtype=np.float32)[:, None] * inv
    ang_c = np.arange(GRID_W, dtype=np.float32)[:, None] * inv
    per_row = lambda tab: np.repeat(tab, GRID_W, axis=0)
    per_col = lambda tab: np.tile(tab, (rows, 1))
    cr, sr = per_row(np.cos(ang_r)), per_row(np.sin(ang_r))
    cc, sc = per_col(np.cos(ang_c)), per_col(np.sin(ang_c))
    z = np.zeros_like(cr)
    cos64 = np.concatenate([cr, cr, cc, cc], axis=-1)
    up64 = np.concatenate([-sr, z, -sc, z], axis=-1)
    dn64 = np.concatenate([z, sr, z, sc], axis=-1)
    tab = np.concatenate([cos64, cos64, up64, up64, dn64, dn64], axis=-1)
    ident = np.concatenate([np.ones((g.tm, 128), np.float32), np.zeros((g.tm, 256), np.float32)], axis=-1)
    return jnp.asarray(np.concatenate([ident, tab], axis=0).astype(np.float32))


def _gla_kernel(qf, kf, vf, rf, qb, kb, vb, rb, wg_ref, bg_ref, of_ref, ob_ref, st_ref):
    s = pl.program_id(1)

    @pl.when(s == 0)
    def _():
        st_ref[...] = jnp.zeros_like(st_ref)

    n = MIX_BLOCK
    nch = n // GLA_CHUNK
    ri = lax.broadcasted_iota(jnp.int32, (n, n), 0)
    ci = lax.broadcasted_iota(jnp.int32, (n, n), 1)
    same_chunk = (ri // GLA_CHUNK) == (ci // GLA_CHUNK)
    col_chunk = ci // GLA_CHUNK
    lane_head = lax.broadcasted_iota(jnp.int32, (1, EV_GLA_W), 1) // GLA_DK
    diag_blocks = ((lax.broadcasted_iota(jnp.int32, (EV_GLA_W, EV_GLA_V), 0) // GLA_DK)
                   == (lax.broadcasted_iota(jnp.int32, (EV_GLA_W, EV_GLA_V), 1) // GLA_DV))

    dirs = ((qf, kf, vf, rf, of_ref), (qb, kb, vb, rb, ob_ref))
    causal, la, qd, ki, kd, v16 = [], [], [], [], [], []
    for d, (q_ref, k_ref, v_ref, r_ref, _) in enumerate(dirs):
        causal.append(jnp.logical_and(same_chunk, (ci >= ri) if d == 1 else (ci <= ri)))
        z = jnp.dot(r_ref[...].astype(BF16), wg_ref[d], preferred_element_type=F32) + bg_ref[d]
        la.append((jnp.minimum(z, 0.0) - jnp.log1p(jnp.exp(-jnp.abs(z)))) * GLA_INV_TAU)
    def decayed(d):
        q_ref, k_ref, v_ref, _, _ = dirs[d]
        cum = _split_dot(jnp.where(causal[d], 1.0, 0.0).astype(BF16), la[d])
        tot = jnp.concatenate(
            [jnp.broadcast_to(jnp.sum(la[d][c * GLA_CHUNK:(c + 1) * GLA_CHUNK], axis=0, keepdims=True),
                              (GLA_CHUNK, EV_GLA_W)) for c in range(nch)], axis=0)
        k = k_ref[...]
        v16.append(v_ref[...].astype(BF16))
        qd.append(q_ref[...] * jnp.exp(cum) * (GLA_DK ** -0.5))
        ki.append((k * jnp.exp(-cum)).astype(BF16))
        kd.append(k * jnp.exp(tot - cum))

    parts = ([], [])

    def intra(d, h):
        qh = jnp.where(lane_head == h, qd[d], 0.0).astype(BF16)
        att = lax.dot_general(qh, ki[d], (((1,), (1,)), ((), ())), preferred_element_type=F32)
        att = jnp.where(causal[d], att, 0.0).astype(BF16)
        parts[d].append(jnp.dot(att, v16[d][:, h * GLA_DV:(h + 1) * GLA_DV], preferred_element_type=F32))

    decayed(0)
    intra(0, 0)
    intra(0, 1)
    decayed(1)
    for h in range(2, GLA_HEADS):
        intra(0, h)
    for h in range(GLA_HEADS):
        intra(1, h)

    la_t = [la[d].T for d in range(2)]
    kd_t = [kd[d].T for d in range(2)]
    qd16 = [qd[d].astype(BF16) for d in range(2)]
    state = [st_ref[0], st_ref[1]]
    o_inter = ([None] * nch, [None] * nch)
    for i in range(nch):
        for d in range(2):
            c = nch - 1 - i if d == 1 else i
            in_chunk = col_chunk == c
            o_inter[d][c] = jnp.dot(qd16[d][c * GLA_CHUNK:(c + 1) * GLA_CHUNK], state[d].astype(BF16),
                                    preferred_element_type=F32)
            kv = jnp.dot(jnp.where(in_chunk, kd_t[d], 0.0).astype(BF16), v16[d],
                         preferred_element_type=F32)
            decay = jnp.exp(jnp.sum(jnp.where(in_chunk, la_t[d], 0.0), axis=1, keepdims=True))
            state[d] = decay * state[d] + jnp.where(diag_blocks, kv, 0.0)
    for d in range(2):
        st_ref[d] = state[d]
        dirs[d][4][...] = jnp.concatenate(parts[d], axis=1) + jnp.concatenate(o_inter[d], axis=0)


def _gla_call(g, ymain, wg, bg):
    blk = g.blk
    qw, vw = EV_GLA_W, EV_GLA_V

    def specs(block_fn):
        return [pl.BlockSpec((blk, qw), lambda b, s: (block_fn(b, s), 0)),
                pl.BlockSpec((blk, qw), lambda b, s: (block_fn(b, s), 1)),
                pl.BlockSpec((blk, vw), lambda b, s: (block_fn(b, s), 1)),
                pl.BlockSpec((blk, 128), lambda b, s: (block_fn(b, s), (EV_MAIN - 128) // 128))]

    return pl.pallas_call(
        _gla_kernel,
        grid=(g.b, g.nb + 1),
        in_specs=specs(g.fwd_block) + specs(g.bwd_block) + [
            pl.BlockSpec((2, 128, qw), lambda b, s: (0, 0, 0)),
            pl.BlockSpec((2, 1, qw), lambda b, s: (0, 0, 0))],
        out_specs=[pl.BlockSpec((blk, vw), lambda b, s: (g.fwd_block(b, s), 0)),
                   pl.BlockSpec((blk, vw), lambda b, s: (g.bwd_block(b, s), 0))],
        out_shape=[jax.ShapeDtypeStruct((g.rows, vw), F32)] * 2,
        scratch_shapes=[pltpu.VMEM((2, qw, vw), F32)],
        compiler_params=_cparams(("parallel", "arbitrary")),
        name="gla_scan",
    )(ymain, ymain, ymain, ymain, ymain, ymain, ymain, ymain, wg, bg)


def _part8(x, op):
    rows = [x[i * 8:(i + 1) * 8] for i in range(x.shape[0] // 8)]
    while len(rows) > 1:
        rows = [op(rows[i], rows[i + 1]) for i in range(0, len(rows), 2)]
    return rows[0]


def _attn_kernel(lam_ref, nw_ref, qt_ref, kc_ref, vct_ref, *rest, lam_init, with_latent):
    if with_latent:
        kx_ref, vxt_ref, o_ref, m_sc, acc_sc, nxt_sc = rest
        ki = pl.program_id(3)
        last = pl.num_programs(3) - 1
    else:
        o_ref, m_sc, acc_sc = rest
    qt = qt_ref[...]
    tq = qt.shape[1]
    row = lax.broadcasted_iota(jnp.int32, qt.shape, 0)
    zero = jnp.zeros_like(qt)
    qt_maps = (jnp.where(row < DIFF_DQK, qt, zero), jnp.where(row >= DIFF_DQK, qt, zero))

    def with_ones(vt):
        return jnp.concatenate([vt, jnp.ones((ATTN_ONES, vt.shape[1]), vt.dtype)], axis=0)

    def exact_update(k, vt1):
        for m in range(2):
            st = jnp.dot(k, qt_maps[m], preferred_element_type=F32)
            m_old = m_sc[m]
            m_new = jnp.maximum(m_old, jnp.max(st, axis=0, keepdims=True))
            alpha = jnp.exp2(m_old - m_new)
            p = jnp.exp2(st - m_new)
            acc_sc[m] = alpha * acc_sc[m] + jnp.dot(vt1, p.astype(BF16), preferred_element_type=F32)
            m_sc[m] = m_new

    def lagged_update(k, vt1):
        tk = k.shape[0]
        refs = [m_sc[0], m_sc[1]]
        groups = [(m, gq) for m in range(2) for gq in range(tq // ATTN_QG)]
        cols_of = lambda gq: slice(gq * ATTN_QG, (gq + 1) * ATTN_QG)
        scores = lambda m, gq: jnp.dot(k, qt_maps[m][:, cols_of(gq)], preferred_element_type=F32)
        mx_g = ([], [])
        st = scores(*groups[0])
        for gi, (m, gq) in enumerate(groups):
            cols = cols_of(gq)
            rg = refs[m][:, cols]
            mx = None
            tiles = []
            for c in range(tk // ATTN_KC):
                tile = st[c * ATTN_KC:(c + 1) * ATTN_KC]
                tiles.append(jnp.exp2(tile - rg).astype(BF16))
                tmx = _part8(tile, jnp.maximum)
                mx = tmx if mx is None else jnp.maximum(mx, tmx)
            mx_g[m].append(mx)
            if gi + 1 < len(groups):
                st = scores(*groups[gi + 1])
            pv = jnp.dot(vt1, jnp.concatenate(tiles, axis=0), preferred_element_type=F32)
            nxt_sc[m, :, cols] = acc_sc[m, :, cols] + pv
        bmax = [jnp.max(jnp.concatenate(mx_g[m], axis=1), axis=0, keepdims=True) for m in range(2)]
        lag = jnp.maximum(jnp.max(bmax[0] - refs[0]), jnp.max(bmax[1] - refs[1]))
        safe = lag <= ATTN_LAG_LIMIT

        @pl.when(safe)
        def _():
            for m in range(2):
                r_new = jnp.maximum(refs[m], bmax[m])
                acc_sc[m] = nxt_sc[m] * jnp.exp2(refs[m] - r_new)
                m_sc[m] = r_new

        @pl.when(jnp.logical_not(safe))
        def _():
            exact_update(k, vt1)

    def init_and_ctx():
        m_sc[...] = jnp.full(m_sc.shape, -jnp.inf, F32)
        acc_sc[...] = jnp.zeros(acc_sc.shape, F32)
        exact_update(kc_ref[...], with_ones(vct_ref[...]))

    def finish():
        lp = lam_ref[...]
        lam = (jnp.exp(jnp.sum(lp[0:1] * lp[1:2], keepdims=True))
               - jnp.exp(jnp.sum(lp[2:3] * lp[3:4], keepdims=True)) + lam_init)
        a0, a1 = acc_sc[0], acc_sc[1]
        ot = (a0[:DIFF_DV] / a0[DIFF_DV:DIFF_DV + 1]
              - lam * (a1[:DIFF_DV] / a1[DIFF_DV:DIFF_DV + 1]))
        ms = jnp.mean(ot * ot, axis=0, keepdims=True)
        on = ot * lax.rsqrt(ms + RMS_EPS) * nw_ref[...] * (1.0 - lam_init)
        o_ref[...] = on.T.astype(o_ref.dtype)

    if with_latent:
        pl.when(ki == 0)(init_and_ctx)
        lagged_update(kx_ref[...], with_ones(vxt_ref[...]))
        pl.when(ki == last)(finish)
    else:
        init_and_ctx()
        finish()


def _attn_call(g, k2, qt, vt, lam_p, norm_w_col, lam_init):
    hd = 2 * DIFF_DQK
    tq = min(ATTN_TQ, g.t)
    tk = min(ATTN_TK, g.t)
    assert g.t % tq == 0 and g.t % tk == 0
    nq, nk = g.t // tq, g.t // tk
    c0 = g.lat_blocks
    scratch = lambda nqry: [pltpu.VMEM((2, 1, nqry), F32),
                            pltpu.VMEM((2, DIFF_DV + ATTN_ONES, nqry), F32)]
    common = [pl.BlockSpec((4, DIFF_DQK), lambda *_: (0, 0)),
              pl.BlockSpec((DIFF_DV, 1), lambda *_: (0, 0))]
    od = pl.pallas_call(
        functools.partial(_attn_kernel, lam_init=lam_init, with_latent=True),
        grid=(g.b, DIFF_HEADS, nq, nk),
        in_specs=common + [
            pl.BlockSpec((hd, tq), lambda b, h, qi, ki: (h, b * nq + qi)),
            pl.BlockSpec((g.c, hd), lambda b, h, qi, ki: (c0 + b, h)),
            pl.BlockSpec((DIFF_DV, g.c), lambda b, h, qi, ki: (h, c0 + b)),
            pl.BlockSpec((tk, hd), lambda b, h, qi, ki: (b * nk + ki, h)),
            pl.BlockSpec((DIFF_DV, tk), lambda b, h, qi, ki: (h, b * nk + ki))],
        out_specs=pl.BlockSpec((tq, DIFF_DV), lambda b, h, qi, ki: (b * nq + qi, h)),
        out_shape=jax.ShapeDtypeStruct((g.b * g.t, DIFF_HEADS * DIFF_DV), BF16),
        scratch_shapes=scratch(tq) + [pltpu.VMEM((2, DIFF_DV + ATTN_ONES, tq), F32)],
        compiler_params=_cparams(("parallel", "parallel", "parallel", "arbitrary")),
        name="diff_attn_latent",
    )(lam_p, norm_w_col, qt, k2, vt, k2, vt)
    return od


def _attn_ctx_call(g, k2, qt, vt, lam_p, norm_w_col, lam_init):
    hd = 2 * DIFF_DQK
    return pl.pallas_call(
        functools.partial(_attn_kernel, lam_init=lam_init, with_latent=False),
        grid=(g.b, DIFF_HEADS),
        in_specs=[pl.BlockSpec((4, DIFF_DQK), lambda *_: (0, 0)),
                  pl.BlockSpec((DIFF_DV, 1), lambda *_: (0, 0)),
                  pl.BlockSpec((hd, g.c), lambda b, h: (h, g.lat_blocks + b)),
                  pl.BlockSpec((g.c, hd), lambda b, h: (g.lat_blocks + b, h)),
                  pl.BlockSpec((DIFF_DV, g.c), lambda b, h: (h, g.lat_blocks + b))],
        out_specs=pl.BlockSpec((g.c, DIFF_DV), lambda b, h: (b, h)),
        out_shape=jax.ShapeDtypeStruct((g.b * g.c, DIFF_HEADS * DIFF_DV), BF16),
        scratch_shapes=[pltpu.VMEM((2, 1, g.c), F32),
                        pltpu.VMEM((2, DIFF_DV + ATTN_ONES, g.c), F32)],
        compiler_params=_cparams(("parallel", "parallel")),
        name="diff_attn_context",
    )(lam_p, norm_w_col, qt, k2, vt)


def _proj_odd_kernel(x_ref, xp_ref, xn_ref, mod_ref, w_ref, cw_ref, cb_ref, z_ref, xbc_ref, dt_ref,
                     seq_ref, *, lat_blocks, nb, d_inner, conv_ch):
    i = pl.program_id(0)
    j = i % nb
    is_lat = i < lat_blocks
    has_prev = jnp.logical_and(is_lat, j > 0).astype(F32)
    has_next = jnp.logical_and(is_lat, j < nb - 1).astype(F32)
    scale = 1.0 + mod_ref[1:2, :]
    shift = mod_ref[0:1, :]
    rows = jnp.concatenate([x_ref[...], xp_ref[...], xn_ref[...]], axis=0)
    h = (rows * scale + shift).astype(BF16)
    n = x_ref.shape[0]
    cw = PROJ_CHUNK
    dt_w = dt_ref.shape[1]
    other = [(z_ref, c0, c0, cw) for c0 in range(0, d_inner, cw)]
    other += [(dt_ref, c0, d_inner + conv_ch + c0, min(cw, dt_w)) for c0 in range(0, dt_w, cw)]
    pad = SSD_CONV // 2
    n_chunks = conv_ch // cw
    assert len(other) <= n_chunks
    conv_cols = lambda jc: jnp.dot(h, w_ref[:, d_inner + jc * cw:d_inner + (jc + 1) * cw],
                                   preferred_element_type=F32)
    y = conv_cols(0)
    for jc in range(n_chunks):
        cols = slice(jc * cw, (jc + 1) * cw)
        seq_ref[jc, 0:8, :] = y[n:n + 8] * has_prev
        seq_ref[jc, 8:8 + n, :] = y[:n]
        seq_ref[jc, 8 + n:16 + n, :] = y[n + 8:n + 16] * has_next
        if jc + 1 < n_chunks:
            y = conv_cols(jc + 1)
        if jc < len(other):
            o_ref, oc, wc, ow = other[jc]
            o_ref[:, oc:oc + ow] = jnp.dot(h[:n], w_ref[:, wc:wc + ow], preferred_element_type=F32)
        w = cw_ref[:, cols]
        acc = cb_ref[:, cols] + seq_ref[jc, 8:8 + n, :] * w[pad:pad + 1]
        for tap in range(SSD_CONV):
            if tap != pad:
                acc = acc + seq_ref[jc, pl.ds(8 - pad + tap, n), :] * w[tap:tap + 1]
        xbc_ref[:, cols] = _silu(acc)


def _proj_odd_call(g, s_rows, mod_l, w, conv_w8, conv_b, d_inner, conv_ch):
    blk = g.blk
    assert g.tm % blk == 0
    per_tile = g.tm // blk
    n = w.shape[1]
    dt_w = n - d_inner - conv_ch
    nrow8 = g.rows // 8
    per = blk // 8
    row = lambda i: (i, 0)
    return pl.pallas_call(
        functools.partial(_proj_odd_kernel, lat_blocks=g.lat_blocks, nb=g.nb, d_inner=d_inner,
                          conv_ch=conv_ch),
        grid=(g.rows // blk,),
        in_specs=[pl.BlockSpec((blk, g.d), row),
                  pl.BlockSpec((8, g.d), lambda i: (jnp.maximum(i * per - 1, 0), 0)),
                  pl.BlockSpec((8, g.d), lambda i: (jnp.minimum((i + 1) * per, nrow8 - 1), 0)),
                  pl.BlockSpec((None, 6, g.d), lambda i: (g.mod_row(i // per_tile), 0, 0)),
                  _resident((g.d, n)),
                  pl.BlockSpec((8, conv_ch), lambda i: (0, 0)),
                  pl.BlockSpec((1, conv_ch), lambda i: (0, 0))],
        out_specs=[pl.BlockSpec((blk, d_inner), row),
                   pl.BlockSpec((blk, conv_ch), row),
                   pl.BlockSpec((blk, dt_w), row)],
        out_shape=[jax.ShapeDtypeStruct((g.rows, d_inner), F32),
                   jax.ShapeDtypeStruct((g.rows, conv_ch), F32),
                   jax.ShapeDtypeStruct((g.rows, dt_w), F32)],
        scratch_shapes=[pltpu.VMEM((conv_ch // PROJ_CHUNK, blk + 16, PROJ_CHUNK), F32)],
        compiler_params=_cparams(("parallel",)),
        name="proj_odd_conv",
    )(s_rows, s_rows, s_rows, mod_l, w, conv_w8, conv_b)


def _ssd_kernel(xf, bmf, cmf, dtf, xb, bmb, cmb, dtb, bias_ref, alog_ref, dsk_ref, yf_ref, yb_ref,
                st_ref):
    s = pl.program_id(1)

    @pl.when(s == 0)
    def _():
        st_ref[...] = jnp.zeros_like(st_ref)

    n, ch = MIX_BLOCK, SSD_CHUNK
    gw = SSD_HEADS_PER_GROUP * SSD_HEAD_DIM
    ri = lax.broadcasted_iota(jnp.int32, (n, n), 0)
    ci = lax.broadcasted_iota(jnp.int32, (n, n), 1)
    same_chunk = (ri // ch) == (ci // ch)
    ri_c = lax.broadcasted_iota(jnp.int32, (ch, ch), 0)
    ci_c = lax.broadcasted_iota(jnp.int32, (ch, ch), 1)
    lane_c = lax.broadcasted_iota(jnp.int32, (ch, 128), 1)
    exp_row = lax.broadcasted_iota(jnp.int32, (128, gw), 0)
    exp_col = lax.broadcasted_iota(jnp.int32, (128, gw), 1) // SSD_HEAD_DIM

    dirs = ((xf, bmf, cmf, dtf, yf_ref), (xb, bmb, cmb, dtb, yb_ref))
    tri = [(ci_c >= ri_c) if d == 1 else (ci_c <= ri_c) for d in range(2)]
    n_heads = SSD_GROUPS * SSD_HEADS_PER_GROUP
    a_neg = -jnp.exp(alog_ref[...])
    dt, acs, acs_t, dt_t = [], [], [], []
    for d in range(2):
        z = dirs[d][3][...] + bias_ref[...]
        dt.append(jnp.maximum(z, 0.0) + jnp.log1p(jnp.exp(-jnp.abs(z))))
        causal_blk = jnp.logical_and(same_chunk, (ci >= ri) if d == 1 else (ci <= ri))
        acs.append(_split_dot(jnp.where(causal_blk, 1.0, 0.0).astype(BF16), dt[d] * a_neg))
    for d in range(2):
        acs_t.append(acs[d].T)
        dt_t.append(dt[d].T)
    fac_tok, dec_row = ([], []), ([], [])
    for d in range(2):
        for c in range(n // ch):
            acs_c = acs[d][c * ch:(c + 1) * ch, :]
            a_last = acs_c[0:1, :] if d == 1 else acs_c[ch - 1:ch, :]
            fac_tok[d].append(jnp.concatenate(
                [jnp.exp(acs_c), dt[d][c * ch:(c + 1) * ch, :] * jnp.exp(a_last - acs_c)], axis=0).astype(BF16))
            dec_row[d].append(jnp.broadcast_to(jnp.exp(a_last), (8, 128)))
    for gi in range(SSD_GROUPS):
        lane0 = [n_heads * d + SSD_HEADS_PER_GROUP * gi for d in range(2)]
        expand = [jnp.where(exp_row == exp_col + lane0[d], 1.0, 0.0).astype(BF16) for d in range(2)]
        _ssd_group(gi, dirs, dsk_ref, st_ref, acs, acs_t, dt_t, fac_tok, dec_row, lane0, expand, tri,
                   lane_c)


def _ssd_group(gi, dirs, dsk_ref, st_ref, acs, acs_t, dt_t, fac_tok, dec_row, lane0s, expand, tri,
               lane_c):
    ch = SSD_CHUNK
    nch = MIX_BLOCK // ch
    gw = SSD_HEADS_PER_GROUP * SSD_HEAD_DIM
    slab = slice(128 * gi, 128 * (gi + 1))
    chans = slice(gw * gi, gw * (gi + 1))
    state = [st_ref[0, gi], st_ref[1, gi]]

    for i in range(nch):
        for d in range(2):
            x_ref, bm_ref, cm_ref, _, y_ref = dirs[d]
            rev = d == 1
            lane0 = lane0s[d]
            c = nch - 1 - i if rev else i
            r0, r1 = c * ch, (c + 1) * ch
            xc = x_ref[r0:r1, chans]
            bmc = bm_ref[r0:r1, slab]
            acs_c = acs[d][r0:r1, :]
            acs_tc = acs_t[d][:, r0:r1]
            dt_tc = dt_t[d][:, r0:r1]
            cm16 = cm_ref[r0:r1, slab].astype(BF16)
            cb = lax.dot_general(cm16, bmc.astype(BF16), (((1,), (1,)), ((), ())),
                                 preferred_element_type=F32)
            fac = jnp.dot(fac_tok[d][c], expand[d], preferred_element_type=F32)
            dec = _split_dot_r(dec_row[d][c], expand[d])[0:1]
            pairs = []
            for pp in range(SSD_HEADS_PER_GROUP // 2):
                mats = []
                for e in (2 * pp, 2 * pp + 1):
                    ln = lane0 + e
                    seg = jnp.exp(jnp.where(tri[d], acs_c[:, ln:ln + 1] - acs_tc[ln:ln + 1, :], -jnp.inf))
                    mats.append((cb * seg * dt_tc[ln:ln + 1, :]).astype(BF16))
                xp = xc[:, 128 * pp:128 * (pp + 1)]
                x2 = jnp.concatenate([jnp.where(lane_c < SSD_HEAD_DIM, xp, 0.0),
                                      jnp.where(lane_c >= SSD_HEAD_DIM, xp, 0.0)], axis=0).astype(BF16)
                pairs.append(jnp.dot(jnp.concatenate(mats, axis=1), x2, preferred_element_type=F32))
            y_off = jnp.dot(cm16, state[d].astype(BF16), preferred_element_type=F32) * fac[0:ch]
            y = jnp.concatenate(pairs, axis=1) + y_off
            if not rev:
                y = y + dsk_ref[gi] * xc
            y_ref[r0:r1, chans] = y
            upd = jnp.dot(bmc.T.astype(BF16), (xc * fac[ch:2 * ch]).astype(BF16),
                          preferred_element_type=F32)
            state[d] = dec * state[d] + upd
    st_ref[0, gi] = state[0]
    st_ref[1, gi] = state[1]


def _ssd_call(g, xbc, dt_raw, dt_bias_row, a_log_row, d_skip, d_inner):
    blk = g.blk
    gw = SSD_HEADS_PER_GROUP * SSD_HEAD_DIM
    sw = SSD_GROUPS * 128
    assert d_inner % sw == 0
    bm0 = d_inner // sw
    cm0 = bm0 + 1

    def specs(block_fn):
        return [pl.BlockSpec((blk, d_inner), lambda b, s: (block_fn(b, s), 0)),
                pl.BlockSpec((blk, sw), lambda b, s: (block_fn(b, s), bm0)),
                pl.BlockSpec((blk, sw), lambda b, s: (block_fn(b, s), cm0)),
                pl.BlockSpec((blk, 128), lambda b, s: (block_fn(b, s), 0))]

    row128 = pl.BlockSpec((1, 128), lambda b, s: (0, 0))
    return pl.pallas_call(
        _ssd_kernel,
        grid=(g.b, g.nb + 1),
        in_specs=specs(g.fwd_block) + specs(g.bwd_block) + [
            row128, row128, pl.BlockSpec((SSD_GROUPS, 1, gw), lambda b, s: (0, 0, 0))],
        out_specs=[pl.BlockSpec((blk, d_inner), lambda b, s: (g.fwd_block(b, s), 0)),
                   pl.BlockSpec((blk, d_inner), lambda b, s: (g.bwd_block(b, s), 0))],
        out_shape=[jax.ShapeDtypeStruct((g.rows, d_inner), F32)] * 2,
        scratch_shapes=[pltpu.VMEM((2, SSD_GROUPS, SSD_STATE, gw), F32)],
        compiler_params=_cparams(("parallel", "arbitrary")),
        name="ssd_scan",
    )(xbc, xbc, xbc, dt_raw, xbc, xbc, xbc, dt_raw, dt_bias_row, a_log_row,
      d_skip.reshape(SSD_GROUPS, 1, gw))


def _ffn_residual(x, mod_ref, win_ref, wout_ref, ln_ref, alpha, hidden):
    h = (x * (1.0 + mod_ref[4:5, :]) + mod_ref[3:4, :]).astype(BF16)
    step = -(-hidden // (FFN_CHUNKS * MXU_TILE)) * MXU_TILE
    bounds = [(c0, min(c0 + step, hidden)) for c0 in range(0, hidden, step)]

    def gate_up(c0, c1):
        return (jnp.dot(h, win_ref[:, c0:c1], preferred_element_type=F32),
                jnp.dot(h, win_ref[:, hidden + c0:hidden + c1], preferred_element_type=F32))

    acc = None
    gate, up = gate_up(*bounds[0])
    for j, (c0, c1) in enumerate(bounds):
        nxt = gate_up(*bounds[j + 1]) if j + 1 < len(bounds) else None
        act = (_silu(gate) * up).astype(BF16)
        part = jnp.dot(act, wout_ref[c0:c1, :], preferred_element_type=F32)
        acc = part if acc is None else acc + part
        if nxt is not None:
            gate, up = nxt
    v = alpha * x + mod_ref[5:6, :] * acc
    return _layer_norm(v, ln_ref[2:3, :], ln_ref[3:4, :])


def _post_even_kernel(*refs, n_x, alpha, lat_tiles, hidden):
    (mod_ref, of_ref, ob_ref, g_ref, odc_ref, odl_ref, nw_ref, w_ref, ln_ref, win_ref, wout_ref,
     o_ref) = refs[n_x:]
    x = _row_tile(refs[:n_x], lat_tiles)
    o = of_ref[...] + ob_ref[...]
    gate = g_ref[...]
    nw = nw_ref[...]
    parts = []
    for h in range(GLA_HEADS):
        oh = o[:, h * GLA_DV:(h + 1) * GLA_DV]
        ms = jnp.mean(oh * oh, axis=-1, keepdims=True)
        parts.append(oh * lax.rsqrt(ms + RMS_EPS) * nw)
    gla = (jnp.concatenate(parts, axis=1) * _silu(gate)).astype(BF16)
    od = jnp.where(pl.program_id(0) < lat_tiles, odl_ref[...], odc_ref[...])
    mixin = jnp.concatenate([gla, od], axis=1)
    mix = jnp.dot(mixin, w_ref[...], preferred_element_type=F32)
    v = alpha * x + mod_ref[2:3, :] * mix
    x1 = _layer_norm(v, ln_ref[0:1, :], ln_ref[1:2, :])
    o_ref[...] = _ffn_residual(x1, mod_ref, win_ref, wout_ref, ln_ref, alpha, hidden)


def _post_even_call(g, s_rows, mod_l, o_f, o_b, ymain, od_ctx, od_lat, gla_nw, w_mix, ln4, w_in, w_out,
                    alpha, latent_only):
    tm, d = g.tm, g.d
    vw = EV_GLA_V
    n_tiles = g.lat_tiles if latent_only else g.n_tiles
    row = lambda i: (i, 0)
    x_specs, x_args = _row_inputs(g, s_rows, tm)
    return pl.pallas_call(
        functools.partial(_post_even_kernel, n_x=len(x_args), alpha=alpha, lat_tiles=g.lat_tiles,
                          hidden=w_out.shape[0]),
        grid=(n_tiles,),
        in_specs=x_specs + [
                  pl.BlockSpec((None, 6, d), lambda i: (g.mod_row(i), 0, 0)),
                  pl.BlockSpec((tm, vw), row),
                  pl.BlockSpec((tm, vw), row),
                  pl.BlockSpec((tm, vw), lambda i: (i, 2 * EV_GLA_W // vw + 1)),
                  pl.BlockSpec((tm, vw), lambda i: (jnp.maximum(i - g.lat_tiles, 0), 0)),
                  pl.BlockSpec((tm, vw), lambda i: (jnp.minimum(i, g.lat_tiles - 1), 0)),
                  pl.BlockSpec((1, GLA_DV), lambda i: (0, 0)),
                  _resident(w_mix.shape),
                  pl.BlockSpec((4, d), lambda i: (0, 0)),
                  _resident(w_in.shape),
                  _resident(w_out.shape)],
        out_specs=pl.BlockSpec((tm, d), lambda i: (i, 0)),
        out_shape=jax.ShapeDtypeStruct((n_tiles * tm, d), F32),
        compiler_params=_cparams(("parallel",)),
        name="post_even",
    )(*x_args, mod_l, o_f, o_b, ymain, od_ctx, od_lat, gla_nw, w_mix, ln4, w_in, w_out)


def _post_odd_kernel(x_ref, mod_ref, yf_ref, yb_ref, z_ref, nw_ref, w_ref, ln_ref, win_ref, wout_ref,
                     o_ref, *, alpha, hidden):
    y = (yf_ref[...] + yb_ref[...]) * _silu(z_ref[...])
    gw = y.shape[1] // SSD_GROUPS
    parts = []
    for gi in range(SSD_GROUPS):
        yg = y[:, gi * gw:(gi + 1) * gw]
        ms = jnp.mean(yg * yg, axis=-1, keepdims=True)
        parts.append(yg * lax.rsqrt(ms + RMS_EPS))
    yn = (jnp.concatenate(parts, axis=1) * nw_ref[...]).astype(BF16)
    mix = jnp.dot(yn, w_ref[...], preferred_element_type=F32)
    v = alpha * x_ref[...] + mod_ref[2:3, :] * mix
    x1 = _layer_norm(v, ln_ref[0:1, :], ln_ref[1:2, :])
    o_ref[...] = _ffn_residual(x1, mod_ref, win_ref, wout_ref, ln_ref, alpha, hidden)


def _post_odd_call(g, s_rows, mod_l, y_f, y_b, z_gate, norm_w, w_mix, ln4, w_in, w_out, alpha,
                   latent_only):
    tm = g.tm // 2
    per_tile = g.tm // tm
    d = g.d
    di = w_mix.shape[0]
    n_tiles = (g.lat_tiles if latent_only else g.n_tiles) * per_tile
    row = lambda i: (i, 0)
    return pl.pallas_call(
        functools.partial(_post_odd_kernel, alpha=alpha, hidden=w_out.shape[0]),
        grid=(n_tiles,),
        in_specs=[pl.BlockSpec((tm, d), row),
                  pl.BlockSpec((None, 6, d), lambda i: (g.mod_row(i // per_tile), 0, 0)),
                  pl.BlockSpec((tm, di), row),
                  pl.BlockSpec((tm, di), row),
                  pl.BlockSpec((tm, di), row),
                  pl.BlockSpec((1, di), lambda i: (0, 0)),
                  _resident(w_mix.shape),
                  pl.BlockSpec((4, d), lambda i: (0, 0)),
                  _resident(w_in.shape),
                  _resident(w_out.shape)],
        out_specs=pl.BlockSpec((tm, d), lambda i: (i, 0)),
        out_shape=jax.ShapeDtypeStruct((n_tiles * tm, d), F32),
        compiler_params=_cparams(("parallel",)),
        name="post_odd",
    )(s_rows, mod_l, y_f, y_b, z_gate, norm_w, w_mix, ln4, w_in, w_out)


def kernel(x, c, ctx, c_ctx, mod_w, mod_b, ln_g, ln_b, ffn_w_in, ffn_w_out, ev_w_in, ev_w_out,
           gla_w_gate2, gla_b_gate, gla_norm_w, diff_lambda, diff_norm_w, ssd_w_in, ssd_conv_w,
           ssd_conv_b, ssd_dt_bias, ssd_a_log, ssd_d, ssd_norm_w, ssd_w_out):
    batch, seq, d = x.shape
    ctx_len = ctx.shape[1]
    depth = mod_w.shape[0]
    g = _Geom(batch, seq, ctx_len, d)
    alpha = (2 * depth) ** 0.25
    d_inner = ssd_w_out.shape[1]
    conv_ch = ssd_conv_w.shape[2]
    n_heads = ssd_d.shape[1]

    assert depth >= 1
    s_rows = (x.reshape(batch * seq, d), ctx.reshape(batch * ctx_len, d))
    cc = jnp.zeros((8, d), F32).at[:batch].set(c).at[batch].set(c_ctx)
    mod_all = _mod_call(cc, mod_w, mod_b).reshape(depth, 8, 6, d)
    rope_tab = _rope_table(g)

    for layer in range(depth):
        need_ctx = layer < depth - 1
        mod_l = mod_all[layer]
        ln4 = jnp.stack([ln_g[layer, 0], ln_b[layer, 0], ln_g[layer, 1], ln_b[layer, 1]])
        w_ffn_in = ffn_w_in[layer].astype(BF16)
        w_ffn_out = ffn_w_out[layer].astype(BF16)
        if layer % 2 == 0:
            e = layer // 2
            w = ev_w_in[e]
            w_my = jnp.concatenate([w[:, :EV_REAL_MAIN], jnp.zeros((d, 128 - 2 * GLA_RANK), w.dtype),
                                    w[:, EV_REAL_MAIN:]], axis=1).astype(BF16)
            ymain, k2, qt, vt = _proj_even_call(g, s_rows, mod_l, w_my, rope_tab)
            wg = jnp.zeros((2, 128, EV_GLA_W), F32)
            wg = wg.at[0, :GLA_RANK].set(gla_w_gate2[e, 0]).at[1, GLA_RANK:2 * GLA_RANK].set(gla_w_gate2[e, 1])
            o_f, o_b = _gla_call(g, ymain, wg.astype(BF16), gla_b_gate[e].reshape(2, 1, EV_GLA_W))
            lam_init = 0.8 - 0.6 * math.exp(-0.3 * layer)
            nw_col = diff_norm_w[e].reshape(DIFF_DV, 1)
            od_lat = _attn_call(g, k2, qt, vt, diff_lambda[e], nw_col, lam_init)
            od_ctx = _attn_ctx_call(g, k2, qt, vt, diff_lambda[e], nw_col, lam_init)
            s_rows = _post_even_call(g, s_rows, mod_l, o_f, o_b, ymain, od_ctx, od_lat,
                                     gla_norm_w[e].reshape(1, GLA_DV), ev_w_out[e].astype(BF16), ln4,
                                     w_ffn_in, w_ffn_out, alpha, latent_only=not need_ctx)
        else:
            o = layer // 2
            w = ssd_w_in[o]
            assert n_heads == SSD_GROUPS * SSD_HEADS_PER_GROUP and 2 * n_heads <= 128
            w_my = jnp.pad(w.astype(BF16), ((0, 0), (0, 128 - 2 * n_heads)))

            def lane_row(v2):
                return jnp.pad(v2.reshape(1, 2 * n_heads), ((0, 0), (0, 128 - 2 * n_heads)))

            conv_w8 = jnp.concatenate([ssd_conv_w[o], jnp.zeros((8 - SSD_CONV, conv_ch), F32)], axis=0)
            z_gate, xbc, dt_raw = _proj_odd_call(g, s_rows, mod_l, w_my, conv_w8,
                                                 ssd_conv_b[o].reshape(1, conv_ch), d_inner, conv_ch)
            d_skip = jnp.repeat(ssd_d[o], SSD_HEAD_DIM)
            y_f, y_b = _ssd_call(g, xbc, dt_raw, lane_row(ssd_dt_bias[o]), lane_row(ssd_a_log[o]),
                                 d_skip, d_inner)
            s_rows = _post_odd_call(g, s_rows, mod_l, y_f, y_b, z_gate, ssd_norm_w[o].reshape(1, d_inner),
                                    ssd_w_out[o].astype(BF16), ln4, w_ffn_in, w_ffn_out, alpha,
                                    latent_only=not need_ctx)
    return s_rows.reshape(batch, seq, d)
```

```python
import functools
import math

import jax
import jax.numpy as jnp
import numpy as np
from jax import lax
from jax.experimental import pallas as pl
from jax.experimental.pallas import tpu as pltpu

F32 = jnp.float32
BF16 = jnp.bfloat16

GRID_W = 64
GLA_HEADS, GLA_DK, GLA_DV, GLA_RANK, GLA_CHUNK = 4, 64, 128, 16, 64
GLA_INV_TAU = 1.0 / 16.0
DIFF_HEADS, DIFF_DQK, DIFF_DV = 4, 64, 128
ROPE_BASE = 10000.0
SSD_HEAD_DIM, SSD_GROUPS, SSD_STATE, SSD_CONV, SSD_CHUNK = 64, 4, 128, 5, 128
SSD_HEADS_PER_GROUP = 8
LN_EPS = 1e-6
RMS_EPS = 1e-6

EV_GLA_W = GLA_HEADS * GLA_DK
EV_GLA_V = GLA_HEADS * GLA_DV
EV_MAIN = 2 * EV_GLA_W + 2 * EV_GLA_V + 128
EV_DIFF = DIFF_HEADS * 2 * DIFF_DQK
EV_REAL_MAIN = 2 * EV_GLA_W + 2 * EV_GLA_V + 2 * GLA_RANK

ROW_TILE = 512
MIX_BLOCK = 256
ATTN_TQ = 2048
ATTN_TK = 2048
ATTN_QG = 256
ATTN_KC = 128
ATTN_LAG_LIMIT = 50.0
ATTN_ONES = 16
LOG2E = 1.4426950408889634
PROJ_CHUNK = 256
MXU_TILE = 256
FFN_CHUNKS = 4
VMEM_LIMIT = 56 * 1024 * 1024


def _cparams(sem):
    return pltpu.CompilerParams(dimension_semantics=sem, vmem_limit_bytes=VMEM_LIMIT)


def _resident(shape, layer=None):
    if layer is not None:
        tail = len(shape) - 1
        return pl.BlockSpec((None,) + tuple(shape[1:]), lambda *_: (layer,) + (0,) * tail,
                            pipeline_mode=pl.Buffered(1))
    nd = len(shape)
    return pl.BlockSpec(shape, lambda *_: (0,) * nd, pipeline_mode=pl.Buffered(1))


def _silu(v):
    return v / (1.0 + jnp.exp(-v))


def _layer_norm(v, g, b):
    mu = jnp.mean(v, axis=-1, keepdims=True)
    d = v - mu
    var = jnp.mean(d * d, axis=-1, keepdims=True)
    return d * lax.rsqrt(var + LN_EPS) * g + b


def _split_dot(mat_bf16, v):
    hi = v.astype(BF16)
    lo = (v - hi.astype(F32)).astype(BF16)
    return (jnp.dot(mat_bf16, hi, preferred_element_type=F32)
            + jnp.dot(mat_bf16, lo, preferred_element_type=F32))


def _split_dot_r(v, mat_bf16):
    hi = v.astype(BF16)
    lo = (v - hi.astype(F32)).astype(BF16)
    return (jnp.dot(hi, mat_bf16, preferred_element_type=F32)
            + jnp.dot(lo, mat_bf16, preferred_element_type=F32))


def _mod_kernel(c_ref, w_ref, b_ref, o_ref):
    s = _silu(c_ref[...]).astype(BF16)
    o_ref[...] = jnp.dot(s, w_ref[...].astype(BF16), preferred_element_type=F32) + b_ref[...]


def _mod_call(cc, mod_w, mod_b):
    depth, d, n = mod_w.shape
    cw = d
    return pl.pallas_call(
        _mod_kernel,
        grid=(depth, n // cw),
        in_specs=[pl.BlockSpec((8, d), lambda l, j: (0, 0)),
                  pl.BlockSpec((None, d, cw), lambda l, j: (l, 0, j)),
                  pl.BlockSpec((None, 1, cw), lambda l, j: (l, 0, j))],
        out_specs=pl.BlockSpec((None, 8, cw), lambda l, j: (l, 0, j)),
        out_shape=jax.ShapeDtypeStruct((depth, 8, n), F32),
        compiler_params=_cparams(("parallel", "parallel")),
        name="mod_vectors",
    )(cc, mod_w, mod_b.reshape(depth, 1, n))


class _Geom:
    def __init__(self, batch, seq, ctx_len, d_model):
        self.b, self.t, self.c, self.d = batch, seq, ctx_len, d_model
        self.rows = batch * (seq + ctx_len)
        self.tm = min(ROW_TILE, batch * ctx_len)
        assert (batch * ctx_len) % self.tm == 0 and seq % self.tm == 0
        self.tiles_per_batch = seq // self.tm
        self.n_tiles = self.rows // self.tm
        self.lat_tiles = batch * seq // self.tm
        self.blk = MIX_BLOCK
        assert ctx_len == self.blk and seq % self.blk == 0
        self.nb = seq // self.blk
        self.lat_blocks = batch * self.nb

    def mod_row(self, tile):
        return jnp.where(tile < self.lat_tiles, tile // self.tiles_per_batch, self.b)

    def fwd_block(self, b, s):
        return jnp.where(s == 0, self.lat_blocks + b, self.nb * b + s - 1)

    def bwd_block(self, b, s):
        return jnp.where(s == 0, self.lat_blocks + b, self.nb * b + self.nb - s)


def _row_inputs(g, rows, tm):
    if isinstance(rows, tuple):
        lat_tiles = g.lat_tiles * (g.tm // tm)
        return ([pl.BlockSpec((tm, g.d), lambda i: (jnp.minimum(i, lat_tiles - 1), 0)),
                 pl.BlockSpec((tm, g.d), lambda i: (jnp.maximum(i - lat_tiles, 0), 0))], list(rows))
    return [pl.BlockSpec((tm, g.d), lambda i: (i, 0))], [rows]


def _row_tile(x_refs, lat_tiles):
    if len(x_refs) == 2:
        return jnp.where(pl.program_id(0) < lat_tiles, x_refs[0][...], x_refs[1][...])
    return x_refs[0][...]


def _proj_even_kernel(*refs, n_x, lat_tiles):
    mod_ref, w_ref, rope_ref, ymain_ref, k2_ref, qt_ref, vt_ref = refs[n_x:]
    x = _row_tile(refs[:n_x], lat_tiles)
    h = (x * (1.0 + mod_ref[1:2, :]) + mod_ref[0:1, :]).astype(BF16)
    y = jnp.dot(h, w_ref[...], preferred_element_type=F32)
    ymain_ref[...] = y[:, :EV_MAIN]
    cos = rope_ref[:, 0:128]
    sin_up = rope_ref[:, 128:256]
    sin_dn = rope_ref[:, 256:384]

    def rope(t):
        return t * cos + pltpu.roll(t, 112, 1) * sin_up + pltpu.roll(t, 16, 1) * sin_dn

    for j in range(DIFF_HEADS):
        lo, hi = 128 * j, 128 * (j + 1)
        q = rope(y[:, EV_MAIN + lo:EV_MAIN + hi]) * (DIFF_DQK ** -0.5 * LOG2E)
        qt_ref[lo:hi, :] = q.T.astype(BF16)
        k = rope(y[:, EV_MAIN + EV_DIFF + lo:EV_MAIN + EV_DIFF + hi])
        k2_ref[:, lo:hi] = k.astype(BF16)
        v = y[:, EV_MAIN + 2 * EV_DIFF + lo:EV_MAIN + 2 * EV_DIFF + hi]
        vt_ref[lo:hi, :] = v.T.astype(BF16)


def _proj_even_call(g, s_rows, mod_l, w, rope_tab):
    tm, d = g.tm, g.d
    n = w.shape[1]
    def rope_idx(i):
        return jnp.where(i < g.lat_tiles, 1 + i % g.tiles_per_batch, 0)

    x_specs, x_args = _row_inputs(g, s_rows, tm)
    return pl.pallas_call(
        functools.partial(_proj_even_kernel, n_x=len(x_args), lat_tiles=g.lat_tiles),
        grid=(g.n_tiles,),
        in_specs=x_specs + [
            pl.BlockSpec((None, 6, d), lambda i: (g.mod_row(i), 0, 0)),
            _resident((d, n)),
            pl.BlockSpec((tm, 384), lambda i: (rope_idx(i), 0))],
        out_specs=[pl.BlockSpec((tm, EV_MAIN), lambda i: (i, 0)),
                   pl.BlockSpec((tm, EV_DIFF), lambda i: (i, 0)),
                   pl.BlockSpec((EV_DIFF, tm), lambda i: (0, i)),
                   pl.BlockSpec((EV_DIFF, tm), lambda i: (0, i))],
        out_shape=[jax.ShapeDtypeStruct((g.rows, EV_MAIN), F32),
                   jax.ShapeDtypeStruct((g.rows, EV_DIFF), BF16),
                   jax.ShapeDtypeStruct((EV_DIFF, g.rows), BF16),
                   jax.ShapeDtypeStruct((EV_DIFF, g.rows), BF16)],
        compiler_params=_cparams(("parallel",)),
        name="proj_even",
    )(*x_args, mod_l, w, rope_tab)


def _rope_table(g):
    rows = g.t // GRID_W
    n_freq = DIFF_DQK // 4
    inv = np.float32(ROPE_BASE) ** (-np.arange(n_freq, dtype=np.float32) / np.float32(n_freq))
    ang_r = np.arange(rows, dtype=np.float32)[:, None] * inv
    ang_c = np.arange(GRID_W, dtype=np.float32)[:, None] * inv
    per_row = lambda tab: np.repeat(tab, GRID_W, axis=0)
    per_col = lambda tab: np.tile(tab, (rows, 1))
    cr, sr = per_row(np.cos(ang_r)), per_row(np.sin(ang_r))
    cc, sc = per_col(np.cos(ang_c)), per_col(np.sin(ang_c))
    z = np.zeros_like(cr)
    cos64 = np.concatenate([cr, cr, cc, cc], axis=-1)
    up64 = np.concatenate([-sr, z, -sc, z], axis=-1)
    dn64 = np.concatenate([z, sr, z, sc], axis=-1)
    tab = np.concatenate([cos64, cos64, up64, up64, dn64, dn64], axis=-1)
    ident = np.concatenate([np.ones((g.tm, 128), np.float32), np.zeros((g.tm, 256), np.float32)], axis=-1)
    return jnp.asarray(np.concatenate([ident, tab], axis=0).astype(np.float32))


def _gla_kernel(qf, kf, vf, rf, qb, kb, vb, rb, wg_ref, bg_ref, of_ref, ob_ref, st_ref):
    s = pl.program_id(1)

    @pl.when(s == 0)
    def _():
        st_ref[...] = jnp.zeros_like(st_ref)

    n = MIX_BLOCK
    nch = n // GLA_CHUNK
    ri = lax.broadcasted_iota(jnp.int32, (n, n), 0)
    ci = lax.broadcasted_iota(jnp.int32, (n, n), 1)
    same_chunk = (ri // GLA_CHUNK) == (ci // GLA_CHUNK)
    col_chunk = ci // GLA_CHUNK
    lane_head = lax.broadcasted_iota(jnp.int32, (1, EV_GLA_W), 1) // GLA_DK
    diag_blocks = ((lax.broadcasted_iota(jnp.int32, (EV_GLA_W, EV_GLA_V), 0) // GLA_DK)
                   == (lax.broadcasted_iota(jnp.int32, (EV_GLA_W, EV_GLA_V), 1) // GLA_DV))

    dirs = ((qf, kf, vf, rf, of_ref), (qb, kb, vb, rb, ob_ref))
    causal, la, qd, ki, kd, v16 = [], [], [], [], [], []
    for d, (q_ref, k_ref, v_ref, r_ref, _) in enumerate(dirs):
        causal.append(jnp.logical_and(same_chunk, (ci >= ri) if d == 1 else (ci <= ri)))
        z = jnp.dot(r_ref[...].astype(BF16), wg_ref[d], preferred_element_type=F32) + bg_ref[d]
        la.append((jnp.minimum(z, 0.0) - jnp.log1p(jnp.exp(-jnp.abs(z)))) * GLA_INV_TAU)
    def decayed(d):
        q_ref, k_ref, v_ref, _, _ = dirs[d]
        cum = _split_dot(jnp.where(causal[d], 1.0, 0.0).astype(BF16), la[d])
        tot = jnp.concatenate(
            [jnp.broadcast_to(jnp.sum(la[d][c * GLA_CHUNK:(c + 1) * GLA_CHUNK], axis=0, keepdims=True),
                              (GLA_CHUNK, EV_GLA_W)) for c in range(nch)], axis=0)
        k = k_ref[...]
        v16.append(v_ref[...].astype(BF16))
        qd.append(q_ref[...] * jnp.exp(cum) * (GLA_DK ** -0.5))
        ki.append((k * jnp.exp(-cum)).astype(BF16))
        kd.append(k * jnp.exp(tot - cum))

    parts = ([], [])

    def intra(d, h):
        qh = jnp.where(lane_head == h, qd[d], 0.0).astype(BF16)
        att = lax.dot_general(qh, ki[d], (((1,), (1,)), ((), ())), preferred_element_type=F32)
        att = jnp.where(causal[d], att, 0.0).astype(BF16)
        parts[d].append(jnp.dot(att, v16[d][:, h * GLA_DV:(h + 1) * GLA_DV], preferred_element_type=F32))

    decayed(0)
    intra(0, 0)
    intra(0, 1)
    decayed(1)
    for h in range(2, GLA_HEADS):
        intra(0, h)
    for h in range(GLA_HEADS):
        intra(1, h)

    la_t = [la[d].T for d in range(2)]
    kd_t = [kd[d].T for d in range(2)]
    qd16 = [qd[d].astype(BF16) for d in range(2)]
    state = [st_ref[0], st_ref[1]]
    o_inter = ([None] * nch, [None] * nch)
    for i in range(nch):
        for d in range(2):
            c = nch - 1 - i if d == 1 else i
            in_chunk = col_chunk == c
            o_inter[d][c] = jnp.dot(qd16[d][c * GLA_CHUNK:(c + 1) * GLA_CHUNK], state[d].astype(BF16),
                                    preferred_element_type=F32)
            kv = jnp.dot(jnp.where(in_chunk, kd_t[d], 0.0).astype(BF16), v16[d],
                         preferred_element_type=F32)
            decay = jnp.exp(jnp.sum(jnp.where(in_chunk, la_t[d], 0.0), axis=1, keepdims=True))
            state[d] = decay * state[d] + jnp.where(diag_blocks, kv, 0.0)
    for d in range(2):
        st_ref[d] = state[d]
        dirs[d][4][...] = jnp.concatenate(parts[d], axis=1) + jnp.concatenate(o_inter[d], axis=0)


def _gla_call(g, ymain, wg, bg):
    blk = g.blk
    qw, vw = EV_GLA_W, EV_GLA_V

    def specs(block_fn):
        return [pl.BlockSpec((blk, qw), lambda b, s: (block_fn(b, s), 0)),
                pl.BlockSpec((blk, qw), lambda b, s: (block_fn(b, s), 1)),
                pl.BlockSpec((blk, vw), lambda b, s: (block_fn(b, s), 1)),
                pl.BlockSpec((blk, 128), lambda b, s: (block_fn(b, s), (EV_MAIN - 128) // 128))]

    return pl.pallas_call(
        _gla_kernel,
        grid=(g.b, g.nb + 1),
        in_specs=specs(g.fwd_block) + specs(g.bwd_block) + [
            pl.BlockSpec((2, 128, qw), lambda b, s: (0, 0, 0)),
            pl.BlockSpec((2, 1, qw), lambda b, s: (0, 0, 0))],
        out_specs=[pl.BlockSpec((blk, vw), lambda b, s: (g.fwd_block(b, s), 0)),
                   pl.BlockSpec((blk, vw), lambda b, s: (g.bwd_block(b, s), 0))],
        out_shape=[jax.ShapeDtypeStruct((g.rows, vw), F32)] * 2,
        scratch_shapes=[pltpu.VMEM((2, qw, vw), F32)],
        compiler_params=_cparams(("parallel", "arbitrary")),
        name="gla_scan",
    )(ymain, ymain, ymain, ymain, ymain, ymain, ymain, ymain, wg, bg)


def _part8(x, op):
    rows = [x[i * 8:(i + 1) * 8] for i in range(x.shape[0] // 8)]
    while len(rows) > 1:
        rows = [op(rows[i], rows[i + 1]) for i in range(0, len(rows), 2)]
    return rows[0]


def _attn_kernel(lam_ref, nw_ref, qt_ref, kc_ref, vct_ref, *rest, lam_init, with_latent):
    if with_latent:
        kx_ref, vxt_ref, o_ref, m_sc, acc_sc, nxt_sc = rest
        ki = pl.program_id(3)
        last = pl.num_programs(3) - 1
    else:
        o_ref, m_sc, acc_sc = rest
    qt = qt_ref[...]
    tq = qt.shape[1]
    row = lax.broadcasted_iota(jnp.int32, qt.shape, 0)
    zero = jnp.zeros_like(qt)
    qt_maps = (jnp.where(row < DIFF_DQK, qt, zero), jnp.where(row >= DIFF_DQK, qt, zero))

    def with_ones(vt):
        return jnp.concatenate([vt, jnp.ones((ATTN_ONES, vt.shape[1]), vt.dtype)], axis=0)

    def exact_update(k, vt1):
        for m in range(2):
            st = jnp.dot(k, qt_maps[m], preferred_element_type=F32)
            m_old = m_sc[m]
            m_new = jnp.maximum(m_old, jnp.max(st, axis=0, keepdims=True))
            alpha = jnp.exp2(m_old - m_new)
            p = jnp.exp2(st - m_new)
            acc_sc[m] = alpha * acc_sc[m] + jnp.dot(vt1, p.astype(BF16), preferred_element_type=F32)
            m_sc[m] = m_new

    def lagged_update(k, vt1):
        tk = k.shape[0]
        refs = [m_sc[0], m_sc[1]]
        groups = [(m, gq) for m in range(2) for gq in range(tq // ATTN_QG)]
        cols_of = lambda gq: slice(gq * ATTN_QG, (gq + 1) * ATTN_QG)
        scores = lambda m, gq: jnp.dot(k, qt_maps[m][:, cols_of(gq)], preferred_element_type=F32)
        mx_g = ([], [])
        st = scores(*groups[0])
        for gi, (m, gq) in enumerate(groups):
            cols = cols_of(gq)
            rg = refs[m][:, cols]
            mx = None
            tiles = []
            for c in range(tk // ATTN_KC):
                tile = st[c * ATTN_KC:(c + 1) * ATTN_KC]
                tiles.append(jnp.exp2(tile - rg).astype(BF16))
                tmx = _part8(tile, jnp.maximum)
                mx = tmx if mx is None else jnp.maximum(mx, tmx)
            mx_g[m].append(mx)
            if gi + 1 < len(groups):
                st = scores(*groups[gi + 1])
            pv = jnp.dot(vt1, jnp.concatenate(tiles, axis=0), preferred_element_type=F32)
            nxt_sc[m, :, cols] = acc_sc[m, :, cols] + pv
        bmax = [jnp.max(jnp.concatenate(mx_g[m], axis=1), axis=0, keepdims=True) for m in range(2)]
        lag = jnp.maximum(jnp.max(bmax[0] - refs[0]), jnp.max(bmax[1] - refs[1]))
        safe = lag <= ATTN_LAG_LIMIT

        @pl.when(safe)
        def _():
            for m in range(2):
                r_new = jnp.maximum(refs[m], bmax[m])
                acc_sc[m] = nxt_sc[m] * jnp.exp2(refs[m] - r_new)
                m_sc[m] = r_new

        @pl.when(jnp.logical_not(safe))
        def _():
            exact_update(k, vt1)

    def init_and_ctx():
        m_sc[...] = jnp.full(m_sc.shape, -jnp.inf, F32)
        acc_sc[...] = jnp.zeros(acc_sc.shape, F32)
        exact_update(kc_ref[...], with_ones(vct_ref[...]))

    def finish():
        lp = lam_ref[...]
        lam = (jnp.exp(jnp.sum(lp[0:1] * lp[1:2], keepdims=True))
               - jnp.exp(jnp.sum(lp[2:3] * lp[3:4], keepdims=True)) + lam_init)
        a0, a1 = acc_sc[0], acc_sc[1]
        ot = (a0[:DIFF_DV] / a0[DIFF_DV:DIFF_DV + 1]
              - lam * (a1[:DIFF_DV] / a1[DIFF_DV:DIFF_DV + 1]))
        ms = jnp.mean(ot * ot, axis=0, keepdims=True)
        on = ot * lax.rsqrt(ms + RMS_EPS) * nw_ref[...] * (1.0 - lam_init)
        o_ref[...] = on.T.astype(o_ref.dtype)

    if with_latent:
        pl.when(ki == 0)(init_and_ctx)
        lagged_update(kx_ref[...], with_ones(vxt_ref[...]))
        pl.when(ki == last)(finish)
    else:
        init_and_ctx()
        finish()


def _attn_call(g, k2, qt, vt, lam_p, norm_w_col, lam_init):
    hd = 2 * DIFF_DQK
    tq = min(ATTN_TQ, g.t)
    tk = min(ATTN_TK, g.t)
    assert g.t % tq == 0 and g.t % tk == 0
    nq, nk = g.t // tq, g.t // tk
    c0 = g.lat_blocks
    scratch = lambda nqry: [pltpu.VMEM((2, 1, nqry), F32),
                            pltpu.VMEM((2, DIFF_DV + ATTN_ONES, nqry), F32)]
    common = [pl.BlockSpec((4, DIFF_DQK), lambda *_: (0, 0)),
              pl.BlockSpec((DIFF_DV, 1), lambda *_: (0, 0))]
    od = pl.pallas_call(
        functools.partial(_attn_kernel, lam_init=lam_init, with_latent=True),
        grid=(g.b, DIFF_HEADS, nq, nk),
        in_specs=common + [
            pl.BlockSpec((hd, tq), lambda b, h, qi, ki: (h, b * nq + qi)),
            pl.BlockSpec((g.c, hd), lambda b, h, qi, ki: (c0 + b, h)),
            pl.BlockSpec((DIFF_DV, g.c), lambda b, h, qi, ki: (h, c0 + b)),
            pl.BlockSpec((tk, hd), lambda b, h, qi, ki: (b * nk + ki, h)),
            pl.BlockSpec((DIFF_DV, tk), lambda b, h, qi, ki: (h, b * nk + ki))],
        out_specs=pl.BlockSpec((tq, DIFF_DV), lambda b, h, qi, ki: (b * nq + qi, h)),
        out_shape=jax.ShapeDtypeStruct((g.b * g.t, DIFF_HEADS * DIFF_DV), BF16),
        scratch_shapes=scratch(tq) + [pltpu.VMEM((2, DIFF_DV + ATTN_ONES, tq), F32)],
        compiler_params=_cparams(("parallel", "parallel", "parallel", "arbitrary")),
        name="diff_attn_latent",
    )(lam_p, norm_w_col, qt, k2, vt, k2, vt)
    return od


def _attn_ctx_call(g, k2, qt, vt, lam_p, norm_w_col, lam_init):
    hd = 2 * DIFF_DQK
    return pl.pallas_call(
        functools.partial(_attn_kernel, lam_init=lam_init, with_latent=False),
        grid=(g.b, DIFF_HEADS),
        in_specs=[pl.BlockSpec((4, DIFF_DQK), lambda *_: (0, 0)),
                  pl.BlockSpec((DIFF_DV, 1), lambda *_: (0, 0)),
                  pl.BlockSpec((hd, g.c), lambda b, h: (h, g.lat_blocks + b)),
                  pl.BlockSpec((g.c, hd), lambda b, h: (g.lat_blocks + b, h)),
                  pl.BlockSpec((DIFF_DV, g.c), lambda b, h: (h, g.lat_blocks + b))],
        out_specs=pl.BlockSpec((g.c, DIFF_DV), lambda b, h: (b, h)),
        out_shape=jax.ShapeDtypeStruct((g.b * g.c, DIFF_HEADS * DIFF_DV), BF16),
        scratch_shapes=[pltpu.VMEM((2, 1, g.c), F32),
                        pltpu.VMEM((2, DIFF_DV + ATTN_ONES, g.c), F32)],
        compiler_params=_cparams(("parallel", "parallel")),
        name="diff_attn_context",
    )(lam_p, norm_w_col, qt, k2, vt)


def _proj_odd_kernel(x_ref, xp_ref, xn_ref, mod_ref, w_ref, cw_ref, cb_ref, z_ref, xbc_ref, dt_ref,
                     seq_ref, *, lat_blocks, nb, d_inner, conv_ch):
    i = pl.program_id(0)
    j = i % nb
    is_lat = i < lat_blocks
    has_prev = jnp.logical_and(is_lat, j > 0).astype(F32)
    has_next = jnp.logical_and(is_lat, j < nb - 1).astype(F32)
    scale = 1.0 + mod_ref[1:2, :]
    shift = mod_ref[0:1, :]
    rows = jnp.concatenate([x_ref[...], xp_ref[...], xn_ref[...]], axis=0)
    h = (rows * scale + shift).astype(BF16)
    n = x_ref.shape[0]
    cw = PROJ_CHUNK
    dt_w = dt_ref.shape[1]
    other = [(z_ref, c0, c0, cw) for c0 in range(0, d_inner, cw)]
    other += [(dt_ref, c0, d_inner + conv_ch + c0, min(cw, dt_w)) for c0 in range(0, dt_w, cw)]
    pad = SSD_CONV // 2
    n_chunks = conv_ch // cw
    assert len(other) <= n_chunks
    conv_cols = lambda jc: jnp.dot(h, w_ref[:, d_inner + jc * cw:d_inner + (jc + 1) * cw],
                                   preferred_element_type=F32)
    y = conv_cols(0)
    for jc in range(n_chunks):
        cols = slice(jc * cw, (jc + 1) * cw)
        seq_ref[jc, 0:8, :] = y[n:n + 8] * has_prev
        seq_ref[jc, 8:8 + n, :] = y[:n]
        seq_ref[jc, 8 + n:16 + n, :] = y[n + 8:n + 16] * has_next
        if jc + 1 < n_chunks:
            y = conv_cols(jc + 1)
        if jc < len(other):
            o_ref, oc, wc, ow = other[jc]
            o_ref[:, oc:oc + ow] = jnp.dot(h[:n], w_ref[:, wc:wc + ow], preferred_element_type=F32)
        w = cw_ref[:, cols]
        acc = cb_ref[:, cols] + seq_ref[jc, 8:8 + n, :] * w[pad:pad + 1]
        for tap in range(SSD_CONV):
            if tap != pad:
                acc = acc + seq_ref[jc, pl.ds(8 - pad + tap, n), :] * w[tap:tap + 1]
        xbc_ref[:, cols] = _silu(acc)


def _proj_odd_call(g, s_rows, mod_l, w, conv_w8, conv_b, d_inner, conv_ch):
    blk = g.blk
    assert g.tm % blk == 0
    per_tile = g.tm // blk
    n = w.shape[1]
    dt_w = n - d_inner - conv_ch
    nrow8 = g.rows // 8
    per = blk // 8
    row = lambda i: (i, 0)
    return pl.pallas_call(
        functools.partial(_proj_odd_kernel, lat_blocks=g.lat_blocks, nb=g.nb, d_inner=d_inner,
                          conv_ch=conv_ch),
        grid=(g.rows // blk,),
        in_specs=[pl.BlockSpec((blk, g.d), row),
                  pl.BlockSpec((8, g.d), lambda i: (jnp.maximum(i * per - 1, 0), 0)),
                  pl.BlockSpec((8, g.d), lambda i: (jnp.minimum((i + 1) * per, nrow8 - 1), 0)),
                  pl.BlockSpec((None, 6, g.d), lambda i: (g.mod_row(i // per_tile), 0, 0)),
                  _resident((g.d, n)),
                  pl.BlockSpec((8, conv_ch), lambda i: (0, 0)),
                  pl.BlockSpec((1, conv_ch), lambda i: (0, 0))],
        out_specs=[pl.BlockSpec((blk, d_inner), row),
                   pl.BlockSpec((blk, conv_ch), row),
                   pl.BlockSpec((blk, dt_w), row)],
        out_shape=[jax.ShapeDtypeStruct((g.rows, d_inner), F32),
                   jax.ShapeDtypeStruct((g.rows, conv_ch), F32),
                   jax.ShapeDtypeStruct((g.rows, dt_w), F32)],
        scratch_shapes=[pltpu.VMEM((conv_ch // PROJ_CHUNK, blk + 16, PROJ_CHUNK), F32)],
        compiler_params=_cparams(("parallel",)),
        name="proj_odd_conv",
    )(s_rows, s_rows, s_rows, mod_l, w, conv_w8, conv_b)


def _ssd_kernel(xf, bmf, cmf, dtf, xb, bmb, cmb, dtb, bias_ref, alog_ref, dsk_ref, yf_ref, yb_ref,
                st_ref):
    s = pl.program_id(1)

    @pl.when(s == 0)
    def _():
        st_ref[...] = jnp.zeros_like(st_ref)

    n, ch = MIX_BLOCK, SSD_CHUNK
    gw = SSD_HEADS_PER_GROUP * SSD_HEAD_DIM
    ri = lax.broadcasted_iota(jnp.int32, (n, n), 0)
    ci = lax.broadcasted_iota(jnp.int32, (n, n), 1)
    same_chunk = (ri // ch) == (ci // ch)
    ri_c = lax.broadcasted_iota(jnp.int32, (ch, ch), 0)
    ci_c = lax.broadcasted_iota(jnp.int32, (ch, ch), 1)
    lane_c = lax.broadcasted_iota(jnp.int32, (ch, 128), 1)
    exp_row = lax.broadcasted_iota(jnp.int32, (128, gw), 0)
    exp_col = lax.broadcasted_iota(jnp.int32, (128, gw), 1) // SSD_HEAD_DIM

    dirs = ((xf, bmf, cmf, dtf, yf_ref), (xb, bmb, cmb, dtb, yb_ref))
    tri = [(ci_c >= ri_c) if d == 1 else (ci_c <= ri_c) for d in range(2)]
    n_heads = SSD_GROUPS * SSD_HEADS_PER_GROUP
    a_neg = -jnp.exp(alog_ref[...])
    dt, acs, acs_t, dt_t = [], [], [], []
    for d in range(2):
        z = dirs[d][3][...] + bias_ref[...]
        dt.append(jnp.maximum(z, 0.0) + jnp.log1p(jnp.exp(-jnp.abs(z))))
        causal_blk = jnp.logical_and(same_chunk, (ci >= ri) if d == 1 else (ci <= ri))
        acs.append(_split_dot(jnp.where(causal_blk, 1.0, 0.0).astype(BF16), dt[d] * a_neg))
    for d in range(2):
        acs_t.append(acs[d].T)
        dt_t.append(dt[d].T)
    fac_tok, dec_row = ([], []), ([], [])
    for d in range(2):
        for c in range(n // ch):
            acs_c = acs[d][c * ch:(c + 1) * ch, :]
            a_last = acs_c[0:1, :] if d == 1 else acs_c[ch - 1:ch, :]
            fac_tok[d].append(jnp.concatenate(
                [jnp.exp(acs_c), dt[d][c * ch:(c + 1) * ch, :] * jnp.exp(a_last - acs_c)], axis=0).astype(BF16))
            dec_row[d].append(jnp.broadcast_to(jnp.exp(a_last), (8, 128)))
    for gi in range(SSD_GROUPS):
        lane0 = [n_heads * d + SSD_HEADS_PER_GROUP * gi for d in range(2)]
        expand = [jnp.where(exp_row == exp_col + lane0[d], 1.0, 0.0).astype(BF16) for d in range(2)]
        _ssd_group(gi, dirs, dsk_ref, st_ref, acs, acs_t, dt_t, fac_tok, dec_row, lane0, expand, tri,
                   lane_c)


def _ssd_group(gi, dirs, dsk_ref, st_ref, acs, acs_t, dt_t, fac_tok, dec_row, lane0s, expand, tri,
               lane_c):
    ch = SSD_CHUNK
    nch = MIX_BLOCK // ch
    gw = SSD_HEADS_PER_GROUP * SSD_HEAD_DIM
    slab = slice(128 * gi, 128 * (gi + 1))
    chans = slice(gw * gi, gw * (gi + 1))
    state = [st_ref[0, gi], st_ref[1, gi]]

    for i in range(nch):
        for d in range(2):
            x_ref, bm_ref, cm_ref, _, y_ref = dirs[d]
            rev = d == 1
            lane0 = lane0s[d]
            c = nch - 1 - i if rev else i
            r0, r1 = c * ch, (c + 1) * ch
            xc = x_ref[r0:r1, chans]
            bmc = bm_ref[r0:r1, slab]
            acs_c = acs[d][r0:r1, :]
            acs_tc = acs_t[d][:, r0:r1]
            dt_tc = dt_t[d][:, r0:r1]
            cm16 = cm_ref[r0:r1, slab].astype(BF16)
            cb = lax.dot_general(cm16, bmc.astype(BF16), (((1,), (1,)), ((), ())),
                                 preferred_element_type=F32)
            fac = jnp.dot(fac_tok[d][c], expand[d], preferred_element_type=F32)
            dec = _split_dot_r(dec_row[d][c], expand[d])[0:1]
            pairs = []
            for pp in range(SSD_HEADS_PER_GROUP // 2):
                mats = []
                for e in (2 * pp, 2 * pp + 1):
                    ln = lane0 + e
                    seg = jnp.exp(jnp.where(tri[d], acs_c[:, ln:ln + 1] - acs_tc[ln:ln + 1, :], -jnp.inf))
                    mats.append((cb * seg * dt_tc[ln:ln + 1, :]).astype(BF16))
                xp = xc[:, 128 * pp:128 * (pp + 1)]
                x2 = jnp.concatenate([jnp.where(lane_c < SSD_HEAD_DIM, xp, 0.0),
                                      jnp.where(lane_c >= SSD_HEAD_DIM, xp, 0.0)], axis=0).astype(BF16)
                pairs.append(jnp.dot(jnp.concatenate(mats, axis=1), x2, preferred_element_type=F32))
            y_off = jnp.dot(cm16, state[d].astype(BF16), preferred_element_type=F32) * fac[0:ch]
            y = jnp.concatenate(pairs, axis=1) + y_off
            if not rev:
                y = y + dsk_ref[gi] * xc
            y_ref[r0:r1, chans] = y
            upd = jnp.dot(bmc.T.astype(BF16), (xc * fac[ch:2 * ch]).astype(BF16),
                          preferred_element_type=F32)
            state[d] = dec * state[d] + upd
    st_ref[0, gi] = state[0]
    st_ref[1, gi] = state[1]


def _ssd_call(g, xbc, dt_raw, dt_bias_row, a_log_row, d_skip, d_inner):
    blk = g.blk
    gw = SSD_HEADS_PER_GROUP * SSD_HEAD_DIM
    sw = SSD_GROUPS * 128
    assert d_inner % sw == 0
    bm0 = d_inner // sw
    cm0 = bm0 + 1

    def specs(block_fn):
        return [pl.BlockSpec((blk, d_inner), lambda b, s: (block_fn(b, s), 0)),
                pl.BlockSpec((blk, sw), lambda b, s: (block_fn(b, s), bm0)),
                pl.BlockSpec((blk, sw), lambda b, s: (block_fn(b, s), cm0)),
                pl.BlockSpec((blk, 128), lambda b, s: (block_fn(b, s), 0))]

    row128 = pl.BlockSpec((1, 128), lambda b, s: (0, 0))
    return pl.pallas_call(
        _ssd_kernel,
        grid=(g.b, g.nb + 1),
        in_specs=specs(g.fwd_block) + specs(g.bwd_block) + [
            row128, row128, pl.BlockSpec((SSD_GROUPS, 1, gw), lambda b, s: (0, 0, 0))],
        out_specs=[pl.BlockSpec((blk, d_inner), lambda b, s: (g.fwd_block(b, s), 0)),
                   pl.BlockSpec((blk, d_inner), lambda b, s: (g.bwd_block(b, s), 0))],
        out_shape=[jax.ShapeDtypeStruct((g.rows, d_inner), F32)] * 2,
        scratch_shapes=[pltpu.VMEM((2, SSD_GROUPS, SSD_STATE, gw), F32)],
        compiler_params=_cparams(("parallel", "arbitrary")),
        name="ssd_scan",
    )(xbc, xbc, xbc, dt_raw, xbc, xbc, xbc, dt_raw, dt_bias_row, a_log_row,
      d_skip.reshape(SSD_GROUPS, 1, gw))


def _ffn_residual(x, mod_ref, win_ref, wout_ref, ln_ref, alpha, hidden):
    h = (x * (1.0 + mod_ref[4:5, :]) + mod_ref[3:4, :]).astype(BF16)
    step = -(-hidden // (FFN_CHUNKS * MXU_TILE)) * MXU_TILE
    bounds = [(c0, min(c0 + step, hidden)) for c0 in range(0, hidden, step)]

    def gate_up(c0, c1):
        return (jnp.dot(h, win_ref[:, c0:c1], preferred_element_type=F32),
                jnp.dot(h, win_ref[:, hidden + c0:hidden + c1], preferred_element_type=F32))

    acc = None
    gate, up = gate_up(*bounds[0])
    for j, (c0, c1) in enumerate(bounds):
        nxt = gate_up(*bounds[j + 1]) if j + 1 < len(bounds) else None
        act = (_silu(gate) * up).astype(BF16)
        part = jnp.dot(act, wout_ref[c0:c1, :], preferred_element_type=F32)
        acc = part if acc is None else acc + part
        if nxt is not None:
            gate, up = nxt
    v = alpha * x + mod_ref[5:6, :] * acc
    return _layer_norm(v, ln_ref[2:3, :], ln_ref[3:4, :])


def _post_even_kernel(*refs, n_x, alpha, lat_tiles, hidden):
    (mod_ref, of_ref, ob_ref, g_ref, odc_ref, odl_ref, nw_ref, w_ref, ln_ref, win_ref, wout_ref,
     o_ref) = refs[n_x:]
    x = _row_tile(refs[:n_x], lat_tiles)
    o = of_ref[...] + ob_ref[...]
    gate = g_ref[...]
    nw = nw_ref[...]
    parts = []
    for h in range(GLA_HEADS):
        oh = o[:, h * GLA_DV:(h + 1) * GLA_DV]
        ms = jnp.mean(oh * oh, axis=-1, keepdims=True)
        parts.append(oh * lax.rsqrt(ms + RMS_EPS) * nw)
    gla = (jnp.concatenate(parts, axis=1) * _silu(gate)).astype(BF16)
    od = jnp.where(pl.program_id(0) < lat_tiles, odl_ref[...], odc_ref[...])
    mixin = jnp.concatenate([gla, od], axis=1)
    mix = jnp.dot(mixin, w_ref[...], preferred_element_type=F32)
    v = alpha * x + mod_ref[2:3, :] * mix
    x1 = _layer_norm(v, ln_ref[0:1, :], ln_ref[1:2, :])
    o_ref[...] = _ffn_residual(x1, mod_ref, win_ref, wout_ref, ln_ref, alpha, hidden)


def _post_even_call(g, s_rows, mod_l, o_f, o_b, ymain, od_ctx, od_lat, gla_nw, w_mix, ln4, w_in, w_out,
                    layer, alpha, latent_only):
    tm, d = g.tm, g.d
    vw = EV_GLA_V
    n_tiles = g.lat_tiles if latent_only else g.n_tiles
    row = lambda i: (i, 0)
    x_specs, x_args = _row_inputs(g, s_rows, tm)
    return pl.pallas_call(
        functools.partial(_post_even_kernel, n_x=len(x_args), alpha=alpha, lat_tiles=g.lat_tiles,
                          hidden=w_out.shape[1]),
        grid=(n_tiles,),
        in_specs=x_specs + [
                  pl.BlockSpec((None, 6, d), lambda i: (g.mod_row(i), 0, 0)),
                  pl.BlockSpec((tm, vw), row),
                  pl.BlockSpec((tm, vw), row),
                  pl.BlockSpec((tm, vw), lambda i: (i, 2 * EV_GLA_W // vw + 1)),
                  pl.BlockSpec((tm, vw), lambda i: (jnp.maximum(i - g.lat_tiles, 0), 0)),
                  pl.BlockSpec((tm, vw), lambda i: (jnp.minimum(i, g.lat_tiles - 1), 0)),
                  pl.BlockSpec((1, GLA_DV), lambda i: (0, 0)),
                  _resident(w_mix.shape),
                  pl.BlockSpec((4, d), lambda i: (0, 0)),
                  _resident(w_in.shape, layer),
                  _resident(w_out.shape, layer)],
        out_specs=pl.BlockSpec((tm, d), lambda i: (i, 0)),
        out_shape=jax.ShapeDtypeStruct((n_tiles * tm, d), F32),
        compiler_params=_cparams(("parallel",)),
        name="post_even",
    )(*x_args, mod_l, o_f, o_b, ymain, od_ctx, od_lat, gla_nw, w_mix, ln4, w_in, w_out)


def _post_odd_kernel(x_ref, mod_ref, yf_ref, yb_ref, z_ref, nw_ref, w_ref, ln_ref, win_ref, wout_ref,
                     o_ref, *, alpha, hidden):
    y = (yf_ref[...] + yb_ref[...]) * _silu(z_ref[...])
    gw = y.shape[1] // SSD_GROUPS
    parts = []
    for gi in range(SSD_GROUPS):
        yg = y[:, gi * gw:(gi + 1) * gw]
        ms = jnp.mean(yg * yg, axis=-1, keepdims=True)
        parts.append(yg * lax.rsqrt(ms + RMS_EPS))
    yn = (jnp.concatenate(parts, axis=1) * nw_ref[...]).astype(BF16)
    mix = jnp.dot(yn, w_ref[...], preferred_element_type=F32)
    v = alpha * x_ref[...] + mod_ref[2:3, :] * mix
    x1 = _layer_norm(v, ln_ref[0:1, :], ln_ref[1:2, :])
    o_ref[...] = _ffn_residual(x1, mod_ref, win_ref, wout_ref, ln_ref, alpha, hidden)


def _post_odd_call(g, s_rows, mod_l, y_f, y_b, z_gate, norm_w, w_mix, ln4, w_in, w_out, layer, alpha,
                   latent_only):
    tm = g.tm // 2
    per_tile = g.tm // tm
    d = g.d
    di = w_mix.shape[0]
    n_tiles = (g.lat_tiles if latent_only else g.n_tiles) * per_tile
    row = lambda i: (i, 0)
    return pl.pallas_call(
        functools.partial(_post_odd_kernel, alpha=alpha, hidden=w_out.shape[1]),
        grid=(n_tiles,),
        in_specs=[pl.BlockSpec((tm, d), row),
                  pl.BlockSpec((None, 6, d), lambda i: (g.mod_row(i // per_tile), 0, 0)),
                  pl.BlockSpec((tm, di), row),
                  pl.BlockSpec((tm, di), row),
                  pl.BlockSpec((tm, di), row),
                  pl.BlockSpec((1, di), lambda i: (0, 0)),
                  _resident(w_mix.shape),
                  pl.BlockSpec((4, d), lambda i: (0, 0)),
                  _resident(w_in.shape, layer),
                  _resident(w_out.shape, layer)],
        out_specs=pl.BlockSpec((tm, d), lambda i: (i, 0)),
        out_shape=jax.ShapeDtypeStruct((n_tiles * tm, d), F32),
        compiler_params=_cparams(("parallel",)),
        name="post_odd",
    )(s_rows, mod_l, y_f, y_b, z_gate, norm_w, w_mix, ln4, w_in, w_out)


def kernel(x, c, ctx, c_ctx, mod_w, mod_b, ln_g, ln_b, ffn_w_in, ffn_w_out, ev_w_in, ev_w_out,
           gla_w_gate2, gla_b_gate, gla_norm_w, diff_lambda, diff_norm_w, ssd_w_in, ssd_conv_w,
           ssd_conv_b, ssd_dt_bias, ssd_a_log, ssd_d, ssd_norm_w, ssd_w_out):
    batch, seq, d = x.shape
    ctx_len = ctx.shape[1]
    depth = mod_w.shape[0]
    g = _Geom(batch, seq, ctx_len, d)
    alpha = (2 * depth) ** 0.25
    d_inner = ssd_w_out.shape[1]
    conv_ch = ssd_conv_w.shape[2]
    n_heads = ssd_d.shape[1]

    assert depth >= 1
    s_rows = (x.reshape(batch * seq, d), ctx.reshape(batch * ctx_len, d))
    cc = jnp.zeros((8, d), F32).at[:batch].set(c).at[batch].set(c_ctx)
    mod_all = _mod_call(cc, mod_w, mod_b).reshape(depth, 8, 6, d)
    rope_tab = _rope_table(g)
    w_ffn_in = ffn_w_in.astype(BF16)
    w_ffn_out = ffn_w_out.astype(BF16)

    for layer in range(depth):
        need_ctx = layer < depth - 1
        mod_l = mod_all[layer]
        ln4 = jnp.stack([ln_g[layer, 0], ln_b[layer, 0], ln_g[layer, 1], ln_b[layer, 1]])
        if layer % 2 == 0:
            e = layer // 2
            w = ev_w_in[e]
            w_my = jnp.concatenate([w[:, :EV_REAL_MAIN], jnp.zeros((d, 128 - 2 * GLA_RANK), w.dtype),
                                    w[:, EV_REAL_MAIN:]], axis=1).astype(BF16)
            ymain, k2, qt, vt = _proj_even_call(g, s_rows, mod_l, w_my, rope_tab)
            wg = jnp.zeros((2, 128, EV_GLA_W), F32)
            wg = wg.at[0, :GLA_RANK].set(gla_w_gate2[e, 0]).at[1, GLA_RANK:2 * GLA_RANK].set(gla_w_gate2[e, 1])
            o_f, o_b = _gla_call(g, ymain, wg.astype(BF16), gla_b_gate[e].reshape(2, 1, EV_GLA_W))
            lam_init = 0.8 - 0.6 * math.exp(-0.3 * layer)
            nw_col = diff_norm_w[e].reshape(DIFF_DV, 1)
            od_lat = _attn_call(g, k2, qt, vt, diff_lambda[e], nw_col, lam_init)
            od_ctx = _attn_ctx_call(g, k2, qt, vt, diff_lambda[e], nw_col, lam_init)
            s_rows = _post_even_call(g, s_rows, mod_l, o_f, o_b, ymain, od_ctx, od_lat,
                                     gla_norm_w[e].reshape(1, GLA_DV), ev_w_out[e].astype(BF16), ln4,
                                     w_ffn_in, w_ffn_out, layer, alpha, latent_only=not need_ctx)
        else:
            o = layer // 2
            w = ssd_w_in[o]
            assert n_heads == SSD_GROUPS * SSD_HEADS_PER_GROUP and 2 * n_heads <= 128
            w_my = jnp.pad(w.astype(BF16), ((0, 0), (0, 128 - 2 * n_heads)))

            def lane_row(v2):
                return jnp.pad(v2.reshape(1, 2 * n_heads), ((0, 0), (0, 128 - 2 * n_heads)))

            conv_w8 = jnp.concatenate([ssd_conv_w[o], jnp.zeros((8 - SSD_CONV, conv_ch), F32)], axis=0)
            z_gate, xbc, dt_raw = _proj_odd_call(g, s_rows, mod_l, w_my, conv_w8,
                                                 ssd_conv_b[o].reshape(1, conv_ch), d_inner, conv_ch)
            d_skip = jnp.repeat(ssd_d[o], SSD_HEAD_DIM)
            y_f, y_b = _ssd_call(g, xbc, dt_raw, lane_row(ssd_dt_bias[o]), lane_row(ssd_a_log[o]),
                                 d_skip, d_inner)
            s_rows = _post_odd_call(g, s_rows, mod_l, y_f, y_b, z_gate, ssd_norm_w[o].reshape(1, d_inner),
                                    ssd_w_out[o].astype(BF16), ln4, w_ffn_in, w_ffn_out, layer, alpha,
                                    latent_only=not need_ctx)
    return s_rows.reshape(batch, seq, d)
```
